```python
import math
import jax, jax.numpy as jnp
from jax import lax
import numpy as np

D_MODEL = 1024
BATCH = 8
SEQ = 4096
DEPTH = 2

CHUNK = 64

MIX_WIDTH = D_MODEL
GROUP_WIDTH = MIX_WIDTH // 4
HEAD_DIM = 64
EPS = 1e-6

SSM_WIDTH = GROUP_WIDTH
SSM_CH_PER_GROUP = 16
SSM_GROUPS = SSM_WIDTH // SSM_CH_PER_GROUP
SSM_STATE = 64
DT_MIN = 1e-3
DT_MAX = 1e-1

POOL_WIDTH = GROUP_WIDTH
POOL_WINDOWS = (2, 4, 8, 16)
POOL_CH = POOL_WIDTH // len(POOL_WINDOWS)

FOX_HEADS = GROUP_WIDTH // HEAD_DIM
FOX_WIDTH = FOX_HEADS * HEAD_DIM
Q_BLOCK = 128

MEM_LEN = 256
MEM_HEADS = 4
MEM_WIDTH = MEM_HEADS * HEAD_DIM

N_EXPERTS = 32
TOP_K = 4
D_EXPERT = D_MODEL
SWIGLU_LIMIT = 7.0
SWIGLU_ALPHA = 1.702
MOE_BLOCK = 128

OFF_A = 0
OFF_B = OFF_A + SSM_WIDTH
OFF_Q = OFF_B + POOL_WIDTH
OFF_K = OFF_Q + FOX_WIDTH
OFF_V = OFF_K + FOX_WIDTH
OFF_F = OFF_V + FOX_WIDTH
OFF_QM = OFF_F + FOX_HEADS
IN_COLS = OFF_QM + MEM_WIDTH

kernel_name = 'hybrid_s5_pool_fox_mem_moe_block'


def rms_norm(x, g):
    xf = x.astype(jnp.float32)
    y = xf * lax.rsqrt(jnp.mean(xf * xf, axis=-1, keepdims=True) + EPS)
    return (y * g.astype(jnp.float32)).astype(x.dtype)


def s5_mixer(u, lam_re, lam_im, log_dt, b_re, b_im, c_re, c_im, d, glu_w, glu_b):
    bsz, seq, _ = u.shape
    f32 = jnp.float32
    uf = u.astype(f32).reshape(bsz, seq, SSM_GROUPS, SSM_CH_PER_GROUP)
    lr = lam_re.astype(f32)
    li = lam_im.astype(f32)
    dt = jnp.exp(log_dt.astype(f32))[:, None]
    mag = jnp.exp(lr * dt)
    a_re = mag * jnp.cos(li * dt)
    a_im = mag * jnp.sin(li * dt)
    den = lr * lr + li * li
    n_re = a_re - 1.0
    n_im = a_im
    k_re = (n_re * lr + n_im * li) / den
    k_im = (n_im * lr - n_re * li) / den
    br = b_re.astype(f32)
    bi = b_im.astype(f32)
    bb_re = k_re[..., None] * br - k_im[..., None] * bi
    bb_im = k_re[..., None] * bi + k_im[..., None] * br
    x_re = jnp.einsum('blgh,gph->blgp', uf, bb_re)
    x_im = jnp.einsum('blgh,gph->blgp', uf, bb_im)
    a_re_t = jnp.broadcast_to(a_re, (1, seq) + a_re.shape)
    a_im_t = jnp.broadcast_to(a_im, (1, seq) + a_im.shape)

    def combine(e1, e2):
        a1r, a1i, b1r, b1i = e1
        a2r, a2i, b2r, b2i = e2
        return (a2r * a1r - a2i * a1i,
                a2r * a1i + a2i * a1r,
                a2r * b1r - a2i * b1i + b2r,
                a2r * b1i + a2i * b1r + b2i)

    _, _, h_re, h_im = lax.associative_scan(combine, (a_re_t, a_im_t, x_re, x_im), axis=1)
    y = (jnp.einsum('blgp,ghp->blgh', h_re, c_re.astype(f32))
         - jnp.einsum('blgp,ghp->blgh', h_im, c_im.astype(f32))
         + d.astype(f32) * uf)
    z = jax.nn.gelu(y.reshape(bsz, seq, SSM_WIDTH))
    out = z * jax.nn.sigmoid(z @ glu_w.astype(f32) + glu_b.astype(f32))
    return out.astype(u.dtype)


def pool_mixer(v, pool_w):
    bsz, seq, _ = v.shape
    vg = v.astype(jnp.float32).reshape(bsz, seq, len(POOL_WINDOWS), POOL_CH)
    cs0 = jnp.pad(jnp.cumsum(vg, axis=1), ((0, 0), (1, 0), (0, 0), (0, 0)))
    pos = jnp.arange(1, seq + 1, dtype=jnp.float32)[None, :, None]
    outs = []
    for gi, w in enumerate(POOL_WINDOWS):
        upper = cs0[:, 1:, gi]
        lower = jnp.pad(cs0[:, :seq + 1 - w, gi], ((0, 0), (w - 1, 0), (0, 0)))
        mean = (upper - lower) / jnp.minimum(pos, float(w))
        outs.append((mean - vg[:, :, gi]) @ pool_w[gi].astype(jnp.float32))
    return jnp.concatenate(outs, axis=-1).astype(v.dtype)


def fox_attention(q, k, v, f_logit, forget_b, q_g, k_g):
    bsz, seq = q.shape[0], q.shape[1]
    q = rms_norm(q, q_g)
    k = rms_norm(k, k_g)
    log_f = jax.nn.log_sigmoid(f_logit.astype(jnp.float32) + forget_b.astype(jnp.float32))
    c = jnp.moveaxis(jnp.cumsum(log_f, axis=1), 1, 2)
    scale = HEAD_DIM ** -0.5
    outs = []
    for blk in range(seq // Q_BLOCK):
        q0 = blk * Q_BLOCK
        q1 = q0 + Q_BLOCK
        s = jnp.einsum('bqhd,bkhd->bhqk', q[:, q0:q1], k[:, :q1],
                       preferred_element_type=jnp.float32) * scale
        s = s + (c[:, :, q0:q1, None] - c[:, :, None, :q1])
        mask = jnp.arange(q0, q1)[:, None] >= jnp.arange(q1)[None, :]
        s = jnp.where(mask, s, -jnp.inf)
        p = jax.nn.softmax(s, axis=-1).astype(v.dtype)
        outs.append(jnp.einsum('bhqk,bkhd->bqhd', p, v[:, :q1]))
    return jnp.concatenate(outs, axis=1).reshape(bsz, seq, FOX_WIDTH)


def memory_attention(qm, mem_k, mem_v, q_g, k_g):
    bsz, seq = qm.shape[0], qm.shape[1]
    qm = rms_norm(qm, q_g)
    mk = rms_norm(mem_k, k_g)
    s = jnp.einsum('blhd,bmhd->bhlm', qm, mk, preferred_element_type=jnp.float32) * (HEAD_DIM ** -0.5)
    p = jax.nn.softmax(s, axis=-1).astype(mem_v.dtype)
    return jnp.einsum('bhlm,bmhd->blhd', p, mem_v).reshape(bsz, seq, MEM_WIDTH)


def moe_ffn(h, router_w, router_b, w_gate, b_gate, w_up, b_up, w_down, b_down):
    bsz, seq, dm = h.shape
    n_tok = bsz * seq
    ht = h.reshape(n_tok, dm)
    logits = (ht @ router_w + router_b).astype(jnp.float32)
    top_val, top_idx = lax.top_k(logits, TOP_K)
    gates = jax.nn.softmax(top_val, axis=-1)
    n = n_tok * TOP_K
    flat_e = top_idx.reshape(n)
    flat_tok = jnp.arange(n, dtype=jnp.int32) // TOP_K
    flat_g = gates.reshape(n)
    order = jnp.argsort(flat_e)
    sorted_e = flat_e[order]
    counts = jnp.bincount(flat_e, length=N_EXPERTS)
    padded = (counts + MOE_BLOCK - 1) // MOE_BLOCK * MOE_BLOCK
    start = jnp.cumsum(counts) - counts
    pad_end = jnp.cumsum(padded)
    pad_start = pad_end - padded
    dest = pad_start[sorted_e] + jnp.arange(n, dtype=jnp.int32) - start[sorted_e]
    cap = n + N_EXPERTS * MOE_BLOCK
    n_blocks = cap // MOE_BLOCK
    buf_tok = jnp.zeros((cap,), jnp.int32).at[dest].set(flat_tok[order])
    buf_gate = jnp.zeros((cap,), jnp.float32).at[dest].set(flat_g[order])
    block_e = jnp.minimum(
        jnp.searchsorted(pad_end, jnp.arange(n_blocks, dtype=jnp.int32) * MOE_BLOCK, side='right'),
        N_EXPERTS - 1)

    def run_block(args):
        tok, g, e = args
        xb = ht[tok]
        gl = xb @ w_gate[e] + b_gate[e]
        up = xb @ w_up[e] + b_up[e]
        gl = jnp.minimum(gl, SWIGLU_LIMIT)
        up = jnp.clip(up, -SWIGLU_LIMIT, SWIGLU_LIMIT)
        act = gl * jax.nn.sigmoid(SWIGLU_ALPHA * gl)
        y = ((up + 1.0) * act) @ w_down[e] + b_down[e]
        return y * g[:, None].astype(y.dtype)

    ys = lax.map(run_block, (buf_tok.reshape(n_blocks, MOE_BLOCK),
                             buf_gate.reshape(n_blocks, MOE_BLOCK), block_e))
    out = jnp.zeros((n_tok, dm), ys.dtype).at[buf_tok].add(ys.reshape(cap, dm))
    return out.reshape(bsz, seq, dm).astype(h.dtype)


def hybrid_layer(x, mem, mix_norm_g, w_in, ssm_lambda_re, ssm_lambda_im, ssm_log_dt,
                 ssm_b_re, ssm_b_im, ssm_c_re, ssm_c_im, ssm_d, ssm_glu_w, ssm_glu_b, ssm_out_g,
                 pool_w, pool_scale, fox_forget_b, fox_q_g, fox_k_g, fox_out_g,
                 mem_norm_g, mem_w_kv, mem_q_g, mem_k_g, mem_out_g, w_out,
                 ffn_norm_g, router_w, router_b, exp_w_gate, exp_b_gate, exp_w_up, exp_b_up,
                 exp_w_down, exp_b_down):
    bsz, seq, _ = x.shape
    h = rms_norm(x, mix_norm_g)
    proj = h @ w_in
    u_a = proj[..., OFF_A:OFF_B]
    u_b = proj[..., OFF_B:OFF_Q]
    q = proj[..., OFF_Q:OFF_K].reshape(bsz, seq, FOX_HEADS, HEAD_DIM)
    k = proj[..., OFF_K:OFF_V].reshape(bsz, seq, FOX_HEADS, HEAD_DIM)
    v = proj[..., OFF_V:OFF_F].reshape(bsz, seq, FOX_HEADS, HEAD_DIM)
    f_logit = proj[..., OFF_F:OFF_QM]
    qm = proj[..., OFF_QM:IN_COLS].reshape(bsz, seq, MEM_HEADS, HEAD_DIM)

    y_a = s5_mixer(u_a, ssm_lambda_re, ssm_lambda_im, ssm_log_dt, ssm_b_re, ssm_b_im,
                   ssm_c_re, ssm_c_im, ssm_d, ssm_glu_w, ssm_glu_b)
    y_b = pool_mixer(u_b, pool_w)
    y_c = fox_attention(q, k, v, f_logit, fox_forget_b, fox_q_g, fox_k_g)
    kv = rms_norm(mem, mem_norm_g) @ mem_w_kv
    n_mem = mem.shape[1]
    mem_k = kv[..., :MEM_WIDTH].reshape(bsz, n_mem, MEM_HEADS, HEAD_DIM)
    mem_v = kv[..., MEM_WIDTH:].reshape(bsz, n_mem, MEM_HEADS, HEAD_DIM)
    y_m = memory_attention(qm, mem_k, mem_v, mem_q_g, mem_k_g)

    merged = jnp.concatenate([rms_norm(y_a, ssm_out_g), rms_norm(y_b, pool_scale),
                              rms_norm(y_c, fox_out_g), rms_norm(y_m, mem_out_g)], axis=-1)
    x = x + merged @ w_out
    x = x + moe_ffn(rms_norm(x, ffn_norm_g), router_w, router_b, exp_w_gate, exp_b_gate,
                    exp_w_up, exp_b_up, exp_w_down, exp_b_down)
    return x


def setup_inputs(seed: int = 0) -> dict:
    key = jax.random.key(seed)
    ks = iter(jax.random.split(key, 48))
    f32 = jnp.float32

    def normal(shape, scale):
        return jax.random.normal(next(ks), shape, f32) * scale

    def gain(shape):
        return 1.0 + normal(shape, 0.02)

    L = DEPTH
    x = normal((BATCH, SEQ, D_MODEL), 1.0)
    mem = normal((BATCH, MEM_LEN, D_MODEL), 1.0)
    mix_norm_g = gain((L, D_MODEL))
    w_in = normal((L, D_MODEL, IN_COLS), D_MODEL ** -0.5)
    ssm_lambda_re = -0.5 + normal((L, SSM_GROUPS, SSM_STATE), 0.01)
    ssm_lambda_im = jnp.pi * jnp.arange(SSM_STATE, dtype=f32) + normal((L, SSM_GROUPS, SSM_STATE), 0.01)
    ssm_log_dt = jax.random.uniform(next(ks), (L, SSM_GROUPS), f32, math.log(DT_MIN), math.log(DT_MAX))
    ssm_b_re = normal((L, SSM_GROUPS, SSM_STATE, SSM_CH_PER_GROUP), (2.0 * SSM_CH_PER_GROUP) ** -0.5)
    ssm_b_im = normal((L, SSM_GROUPS, SSM_STATE, SSM_CH_PER_GROUP), (2.0 * SSM_CH_PER_GROUP) ** -0.5)
    ssm_c_re = normal((L, SSM_GROUPS, SSM_CH_PER_GROUP, SSM_STATE), (2.0 * SSM_STATE) ** -0.5)
    ssm_c_im = normal((L, SSM_GROUPS, SSM_CH_PER_GROUP, SSM_STATE), (2.0 * SSM_STATE) ** -0.5)
    ssm_d = normal((L, SSM_GROUPS, SSM_CH_PER_GROUP), 1.0)
    ssm_glu_w = normal((L, SSM_WIDTH, SSM_WIDTH), SSM_WIDTH ** -0.5)
    ssm_glu_b = normal((L, SSM_WIDTH), 0.02)
    ssm_out_g = gain((L, SSM_WIDTH))
    pool_w = normal((L, len(POOL_WINDOWS), POOL_CH, POOL_CH), POOL_CH ** -0.5)
    pool_scale = gain((L, POOL_WIDTH))
    fox_forget_b = 3.0 + normal((L, FOX_HEADS), 0.5)
    fox_q_g = gain((L, HEAD_DIM))
    fox_k_g = gain((L, HEAD_DIM))
    fox_out_g = gain((L, FOX_WIDTH))
    mem_norm_g = gain((L, D_MODEL))
    mem_w_kv = normal((L, D_MODEL, 2 * MEM_WIDTH), D_MODEL ** -0.5)
    mem_q_g = gain((L, HEAD_DIM))
    mem_k_g = gain((L, HEAD_DIM))
    mem_out_g = gain((L, MEM_WIDTH))
    w_out = normal((L, MIX_WIDTH, D_MODEL), MIX_WIDTH ** -0.5)
    ffn_norm_g = gain((L, D_MODEL))
    router_w = normal((L, D_MODEL, N_EXPERTS), D_MODEL ** -0.5)
    router_b = normal((L, N_EXPERTS), 0.01)
    exp_w_gate = normal((L, N_EXPERTS, D_MODEL, D_EXPERT), D_MODEL ** -0.5)
    exp_b_gate = normal((L, N_EXPERTS, D_EXPERT), 0.01)
    exp_w_up = normal((L, N_EXPERTS, D_MODEL, D_EXPERT), D_MODEL ** -0.5)
    exp_b_up = normal((L, N_EXPERTS, D_EXPERT), 0.01)
    exp_w_down = normal((L, N_EXPERTS, D_EXPERT, D_MODEL), D_EXPERT ** -0.5)
    exp_b_down = normal((L, N_EXPERTS, D_MODEL), 0.01)
    return {'x': x, 'mem': mem, 'mix_norm_g': mix_norm_g, 'w_in': w_in,
            'ssm_lambda_re': ssm_lambda_re, 'ssm_lambda_im': ssm_lambda_im, 'ssm_log_dt': ssm_log_dt,
            'ssm_b_re': ssm_b_re, 'ssm_b_im': ssm_b_im, 'ssm_c_re': ssm_c_re, 'ssm_c_im': ssm_c_im,
            'ssm_d': ssm_d, 'ssm_glu_w': ssm_glu_w, 'ssm_glu_b': ssm_glu_b, 'ssm_out_g': ssm_out_g,
            'pool_w': pool_w, 'pool_scale': pool_scale,
            'fox_forget_b': fox_forget_b, 'fox_q_g': fox_q_g, 'fox_k_g': fox_k_g, 'fox_out_g': fox_out_g,
            'mem_norm_g': mem_norm_g, 'mem_w_kv': mem_w_kv, 'mem_q_g': mem_q_g, 'mem_k_g': mem_k_g,
            'mem_out_g': mem_out_g, 'w_out': w_out, 'ffn_norm_g': ffn_norm_g,
            'router_w': router_w, 'router_b': router_b,
            'exp_w_gate': exp_w_gate, 'exp_b_gate': exp_b_gate, 'exp_w_up': exp_w_up, 'exp_b_up': exp_b_up,
            'exp_w_down': exp_w_down, 'exp_b_down': exp_b_down}


def reference(x, mem, mix_norm_g, w_in, ssm_lambda_re, ssm_lambda_im, ssm_log_dt,
              ssm_b_re, ssm_b_im, ssm_c_re, ssm_c_im, ssm_d, ssm_glu_w, ssm_glu_b, ssm_out_g,
              pool_w, pool_scale, fox_forget_b, fox_q_g, fox_k_g, fox_out_g,
              mem_norm_g, mem_w_kv, mem_q_g, mem_k_g, mem_out_g, w_out,
              ffn_norm_g, router_w, router_b, exp_w_gate, exp_b_gate, exp_w_up, exp_b_up,
              exp_w_down, exp_b_down):
    for l in range(DEPTH):
        x = hybrid_layer(x, mem, mix_norm_g[l], w_in[l], ssm_lambda_re[l], ssm_lambda_im[l],
                         ssm_log_dt[l], ssm_b_re[l], ssm_b_im[l], ssm_c_re[l], ssm_c_im[l], ssm_d[l],
                         ssm_glu_w[l], ssm_glu_b[l], ssm_out_g[l], pool_w[l], pool_scale[l],
                         fox_forget_b[l], fox_q_g[l], fox_k_g[l], fox_out_g[l],
                         mem_norm_g[l], mem_w_kv[l], mem_q_g[l], mem_k_g[l], mem_out_g[l], w_out[l],
                         ffn_norm_g[l], router_w[l], router_b[l], exp_w_gate[l], exp_b_gate[l],
                         exp_w_up[l], exp_b_up[l], exp_w_down[l], exp_b_down[l])
    return x
```

```python
import functools
import math

import jax
import jax.numpy as jnp
from jax import lax
from jax.experimental import pallas as pl
from jax.experimental.pallas import tpu as pltpu

F32 = jnp.float32
BF16 = jnp.bfloat16
I32 = jnp.int32

EPS = 1e-6
HEAD_DIM = 64
N_HEADS = 4
GROUP_WIDTH = 256
LANES = 128
SUBLANES = 8
SSM_GROUPS = 16
SSM_CH = 16
SSM_STATE = 64
SSM_LANES = SSM_GROUPS * SSM_STATE
POOL_WINDOWS = (2, 4, 8, 16)
POOL_HALO = 16
TOP_K = 4
SWIGLU_LIMIT = 7.0
SWIGLU_ALPHA = 1.702
VMEM_LIMIT = 56 * 1024 * 1024

TM_INPROJ = 512
T_S5 = 256
T_POOL = 512
T_ATT = 512
TM_OUT = 256
TM_EXPERT = 256
T_MOVE = 256
TM_COMBINE = 512

NEG_INF = float("-inf")


def _cparams(sem):
    return pltpu.CompilerParams(dimension_semantics=sem, vmem_limit_bytes=VMEM_LIMIT)


def _lane_iota(shape):
    return lax.broadcasted_iota(I32, shape, len(shape) - 1)


def _split3(x):
    hi = x.astype(BF16).astype(F32)
    r = x - hi
    mid = r.astype(BF16).astype(F32)
    lo = r - mid
    return hi, mid, lo


def _head_slab(slab, odd):
    lane = _lane_iota(slab.shape)
    if odd:
        slab = pltpu.roll(slab, HEAD_DIM, axis=1)
    return jnp.where(lane < HEAD_DIM, slab, 0.0)


def _head_norm(xh, gain):
    ss = jnp.sum(xh * xh, axis=-1, keepdims=True)
    return xh * lax.rsqrt(ss * (1.0 / HEAD_DIM) + EPS) * gain


def _join_heads(o_even, o_odd):
    lane = _lane_iota(o_even.shape)
    return jnp.where(lane < HEAD_DIM, o_even, pltpu.roll(o_odd, HEAD_DIM, axis=1))


def _group_norm_pairs(pairs, gain_ref, out_ref):
    ss = sum(jnp.sum(p * p, axis=-1, keepdims=True) for p in pairs)
    scale = lax.rsqrt(ss * (1.0 / GROUP_WIDTH) + EPS)
    for i, p in enumerate(pairs):
        sl = slice(i * LANES, (i + 1) * LANES)
        out_ref[:, sl] = (p * scale * gain_ref[:, sl]).astype(out_ref.dtype)


COL_A, COL_B, COL_Q, COL_K, COL_V, COL_QM, COL_F = 0, 256, 512, 768, 1024, 1280, 1536
IN_COLS_PADDED = COL_F + LANES


def _inproj_kernel(x_ref, g_ref, w_ref, qg_ref, kg_ref, mqg_ref, fb_ref, tri_ref,
                   ua_ref, ub_ref, qa_ref, ka_ref, va_ref, qm_ref, carry_ref, *, tiles_per_seq):
    i = pl.program_id(0)

    @pl.when(i % tiles_per_seq == 0)
    def _():
        carry_ref[...] = jnp.zeros_like(carry_ref)

    x = x_ref[...]
    h = x * lax.rsqrt(jnp.mean(x * x, axis=-1, keepdims=True) + EPS) * g_ref[...]
    proj = jnp.dot(h.astype(BF16), w_ref[...], preferred_element_type=F32)
    ua_ref[...] = proj[:, COL_A:COL_A + GROUP_WIDTH]
    ub_ref[...] = proj[:, COL_B:COL_B + GROUP_WIDTH]

    z = proj[:, COL_F:COL_F + LANES] + fb_ref[...]
    lane = _lane_iota(z.shape)
    logf = jnp.minimum(z, 0.0) - jnp.log(1.0 + jnp.exp(-jnp.abs(z)))
    hi, mid, lo = _split3(logf)
    packed = jnp.where(lane < 4, hi, jnp.where(lane < 8, mid, jnp.where(lane < 12, lo, 0.0)))
    cs = jnp.dot(tri_ref[...], packed.astype(BF16), preferred_element_type=F32)
    cum = cs + pltpu.roll(cs, LANES - 4, axis=1) + pltpu.roll(cs, LANES - 8, axis=1)
    cum = cum + carry_ref[...]
    carry_ref[...] = cum[cum.shape[0] - 1:, :]

    scale = HEAD_DIM ** -0.5
    for hd in range(N_HEADS):
        pair, odd = hd // 2, hd % 2
        c_hi, c_mid, c_lo = _split3(cum[:, hd:hd + 1])
        qs = _head_slab(proj[:, COL_Q + pair * LANES:COL_Q + (pair + 1) * LANES], odd)
        qn = _head_norm(qs, qg_ref[...]) * scale
        q_aug = jnp.where(lane == 64, c_hi, jnp.where(lane == 65, c_mid, jnp.where(
            lane == 66, c_lo, jnp.where((lane >= 67) & (lane < 70), 1.0, qn))))
        qa_ref[hd] = q_aug.astype(BF16)
        ks = _head_slab(proj[:, COL_K + pair * LANES:COL_K + (pair + 1) * LANES], odd)
        kn = _head_norm(ks, kg_ref[...])
        k_aug = jnp.where((lane >= 64) & (lane < 67), 1.0, jnp.where(lane == 67, -c_hi, jnp.where(
            lane == 68, -c_mid, jnp.where(lane == 69, -c_lo, kn))))
        ka_ref[hd] = k_aug.astype(BF16)
        vs = _head_slab(proj[:, COL_V + pair * LANES:COL_V + (pair + 1) * LANES], odd)
        va_ref[hd] = jnp.where(lane == 64, 1.0, vs).astype(BF16)
        ms = _head_slab(proj[:, COL_QM + pair * LANES:COL_QM + (pair + 1) * LANES], odd)
        qm_ref[hd] = (_head_norm(ms, mqg_ref[...]) * scale).astype(BF16)


def _inproj(x2, seq, mix_g, w_all, qg, kg, mqg, fb, tri):
    n, d = x2.shape
    tm = min(TM_INPROJ, seq)
    grid = (n // tm,)
    const = lambda i: (0, 0)
    row = lambda i: (i, 0)
    hrow = lambda i: (0, i, 0)
    out_shape = (
        jax.ShapeDtypeStruct((n, GROUP_WIDTH), F32),
        jax.ShapeDtypeStruct((n, GROUP_WIDTH), F32),
        jax.ShapeDtypeStruct((N_HEADS, n, LANES), BF16),
        jax.ShapeDtypeStruct((N_HEADS, n, LANES), BF16),
        jax.ShapeDtypeStruct((N_HEADS, n, LANES), BF16),
        jax.ShapeDtypeStruct((N_HEADS, n, LANES), BF16),
    )
    return pl.pallas_call(
        functools.partial(_inproj_kernel, tiles_per_seq=seq // tm),
        grid=grid,
        in_specs=[
            pl.BlockSpec((tm, d), row),
            pl.BlockSpec((1, d), const),
            pl.BlockSpec((d, IN_COLS_PADDED), const),
            pl.BlockSpec((1, LANES), const),
            pl.BlockSpec((1, LANES), const),
            pl.BlockSpec((1, LANES), const),
            pl.BlockSpec((1, LANES), const),
            pl.BlockSpec((tm, tm), const),
        ],
        out_specs=(
            pl.BlockSpec((tm, GROUP_WIDTH), row),
            pl.BlockSpec((tm, GROUP_WIDTH), row),
            pl.BlockSpec((N_HEADS, tm, LANES), hrow),
            pl.BlockSpec((N_HEADS, tm, LANES), hrow),
            pl.BlockSpec((N_HEADS, tm, LANES), hrow),
            pl.BlockSpec((N_HEADS, tm, LANES), hrow),
        ),
        out_shape=out_shape,
        scratch_shapes=[pltpu.VMEM((1, LANES), F32)],
        compiler_params=_cparams(("arbitrary",)),
        name="inproj",
    )(x2, mix_g, w_all, qg, kg, mqg, fb, tri)


def _s5_kernel(u_ref, bmat_ref, coef_ref, cmat_ref, d_ref, gluw_ref, glub_ref, og_ref,
               o_ref, x_scr, carry_ref):
    @pl.when(pl.program_id(1) == 0)
    def _():
        carry_ref[...] = jnp.zeros_like(carry_ref)

    u = u_ref[...]
    t = u.shape[0]
    x_scr[...] = jnp.dot(u.astype(BF16), bmat_ref[...], preferred_element_type=F32)
    n_lane_blocks = SSM_LANES // LANES

    def group(gi, carry):
        r0 = pl.multiple_of(gi * SUBLANES, SUBLANES)
        rows = pl.ds(r0, SUBLANES)
        new = []
        for lb in range(n_lane_blocks):
            re_sl = slice(lb * LANES, (lb + 1) * LANES)
            im_sl = slice(SSM_LANES + lb * LANES, SSM_LANES + (lb + 1) * LANES)
            xr = x_scr[rows, re_sl]
            xi = x_scr[rows, im_sl]
            for s, k in enumerate((1, 2, 4)):
                cr = coef_ref[2 * s, :, re_sl]
                ci = coef_ref[2 * s + 1, :, re_sl]
                sr = pltpu.roll(xr, k, axis=0)
                si = pltpu.roll(xi, k, axis=0)
                xr, xi = xr + cr * sr - ci * si, xi + cr * si + ci * sr
            pr = coef_ref[6, :, re_sl]
            pi_ = coef_ref[7, :, re_sl]
            cbr, cbi = carry[2 * lb], carry[2 * lb + 1]
            xr, xi = xr + pr * cbr - pi_ * cbi, xi + pr * cbi + pi_ * cbr
            x_scr[rows, re_sl] = xr
            x_scr[rows, im_sl] = xi
            new.append(jnp.broadcast_to(xr[SUBLANES - 1:, :], xr.shape))
            new.append(jnp.broadcast_to(xi[SUBLANES - 1:, :], xi.shape))
        return tuple(new)

    carry0 = []
    for lb in range(n_lane_blocks):
        carry0.append(carry_ref[:, lb * LANES:(lb + 1) * LANES])
        carry0.append(carry_ref[:, SSM_LANES + lb * LANES:SSM_LANES + (lb + 1) * LANES])
    carry = lax.fori_loop(0, t // SUBLANES, group, tuple(carry0))
    for lb in range(n_lane_blocks):
        carry_ref[:, lb * LANES:(lb + 1) * LANES] = carry[2 * lb]
        carry_ref[:, SSM_LANES + lb * LANES:SSM_LANES + (lb + 1) * LANES] = carry[2 * lb + 1]

    y = jnp.dot(x_scr[...].astype(BF16), cmat_ref[...], preferred_element_type=F32) + d_ref[...] * u
    z = jax.nn.gelu(y, approximate=True)
    gate = jnp.dot(z.astype(BF16), gluw_ref[...], preferred_element_type=F32) + glub_ref[...]
    out = z * (1.0 / (1.0 + jnp.exp(-gate)))
    _group_norm_pairs([out[:, :LANES], out[:, LANES:]], og_ref, o_ref)


def _s5(ua, bsz, seq, bmat, coef, cmat, dvec, gluw, glub, og):
    n = ua.shape[0]
    t = min(T_S5, seq)
    nt = seq // t
    row = lambda b, j: (b * nt + j, 0)
    c2 = lambda b, j: (0, 0)
    c3 = lambda b, j: (0, 0, 0)
    return pl.pallas_call(
        _s5_kernel,
        grid=(bsz, nt),
        in_specs=[
            pl.BlockSpec((t, GROUP_WIDTH), row),
            pl.BlockSpec((GROUP_WIDTH, 2 * SSM_LANES), c2),
            pl.BlockSpec((8, SUBLANES, SSM_LANES), c3),
            pl.BlockSpec((2 * SSM_LANES, GROUP_WIDTH), c2),
            pl.BlockSpec((1, GROUP_WIDTH), c2),
            pl.BlockSpec((GROUP_WIDTH, GROUP_WIDTH), c2),
            pl.BlockSpec((1, GROUP_WIDTH), c2),
            pl.BlockSpec((1, GROUP_WIDTH), c2),
        ],
        out_specs=pl.BlockSpec((t, GROUP_WIDTH), row),
        out_shape=jax.ShapeDtypeStruct((n, GROUP_WIDTH), BF16),
        scratch_shapes=[pltpu.VMEM((t, 2 * SSM_LANES), F32),
                        pltpu.VMEM((SUBLANES, 2 * SSM_LANES), F32)],
        compiler_params=_cparams(("arbitrary", "arbitrary")),
        name="s5",
    )(ua, bmat, coef, cmat, dvec, gluw, glub, og)


def _s5_constants(lam_re, lam_im, log_dt, b_re, b_im, c_re, c_im):
    lr = lam_re.astype(F32)
    li = lam_im.astype(F32)
    dt = jnp.exp(log_dt.astype(F32))[:, None]
    mag = jnp.exp(lr * dt)
    a_re = mag * jnp.cos(li * dt)
    a_im = mag * jnp.sin(li * dt)
    den = lr * lr + li * li
    n_re = a_re - 1.0
    n_im = a_im
    k_re = (n_re * lr + n_im * li) / den
    k_im = (n_im * lr - n_re * li) / den
    br = b_re.astype(F32)
    bi = b_im.astype(F32)
    bb_re = k_re[..., None] * br - k_im[..., None] * bi
    bb_im = k_re[..., None] * bi + k_im[..., None] * br
    eye = jnp.eye(SSM_GROUPS, dtype=F32)
    bm_re = jnp.einsum('gph,gk->ghkp', bb_re, eye).reshape(GROUP_WIDTH, SSM_LANES)
    bm_im = jnp.einsum('gph,gk->ghkp', bb_im, eye).reshape(GROUP_WIDTH, SSM_LANES)
    bmat = jnp.concatenate([bm_re, bm_im], axis=1).astype(BF16)
    cm_re = jnp.einsum('ghp,gk->gpkh', c_re.astype(F32), eye).reshape(SSM_LANES, GROUP_WIDTH)
    cm_im = jnp.einsum('ghp,gk->gpkh', c_im.astype(F32), eye).reshape(SSM_LANES, GROUP_WIDTH)
    cmat = jnp.concatenate([cm_re, -cm_im], axis=0).astype(BF16)

    ar = a_re.reshape(1, SSM_LANES)
    ai = a_im.reshape(1, SSM_LANES)

    def cmul(x, y):
        return x[0] * y[0] - x[1] * y[1], x[0] * y[1] + x[1] * y[0]

    pows = [(ar, ai)]
    for _ in range(SUBLANES - 1):
        pows.append(cmul(pows[-1], (ar, ai)))
    rows = jnp.arange(SUBLANES, dtype=I32)[:, None]
    planes = []
    for k in (1, 2, 4):
        m = (rows >= k).astype(F32)
        planes += [m * pows[k - 1][0], m * pows[k - 1][1]]
    planes += [jnp.concatenate([p[0] for p in pows], axis=0),
               jnp.concatenate([p[1] for p in pows], axis=0)]
    coef = jnp.stack(planes, axis=0)
    return bmat, coef, cmat


def _pool_kernel(v_ref, w_ref, g_ref, o_ref, ext_ref):
    j = pl.program_id(1)
    t = v_ref.shape[0]

    @pl.when(j == 0)
    def _():
        ext_ref[0:POOL_HALO, :] = jnp.zeros((POOL_HALO, GROUP_WIDTH), F32)

    v = v_ref[...]
    ext_ref[POOL_HALO:POOL_HALO + t, :] = v
    cur = ext_ref[...]
    width = 1
    wins = {}
    while width < POOL_WINDOWS[-1]:
        cur = cur + pltpu.roll(cur, width, axis=0)
        width *= 2
        wins[width] = cur[POOL_HALO:, :]
    lane = _lane_iota(v.shape)
    pos = (j * t + lax.broadcasted_iota(I32, v.shape, 0) + 1).astype(F32)
    mean = None
    for gi, w in enumerate(POOL_WINDOWS):
        m = wins[w] / jnp.minimum(pos, float(w))
        mean = m if mean is None else jnp.where(lane >= gi * HEAD_DIM, m, mean)
    mixed = jnp.dot((mean - v).astype(BF16), w_ref[...], preferred_element_type=F32)
    _group_norm_pairs([mixed[:, :LANES], mixed[:, LANES:]], g_ref, o_ref)
    ext_ref[0:POOL_HALO, :] = v[t - POOL_HALO:, :]


def _pool(ub, bsz, seq, w_blk, g):
    n = ub.shape[0]
    t = min(T_POOL, seq)
    nt = seq // t
    row = lambda b, j: (b * nt + j, 0)
    c2 = lambda b, j: (0, 0)
    return pl.pallas_call(
        _pool_kernel,
        grid=(bsz, nt),
        in_specs=[pl.BlockSpec((t, GROUP_WIDTH), row),
                  pl.BlockSpec((GROUP_WIDTH, GROUP_WIDTH), c2),
                  pl.BlockSpec((1, GROUP_WIDTH), c2)],
        out_specs=pl.BlockSpec((t, GROUP_WIDTH), row),
        out_shape=jax.ShapeDtypeStruct((n, GROUP_WIDTH), BF16),
        scratch_shapes=[pltpu.VMEM((t + POOL_HALO, GROUP_WIDTH), F32)],
        compiler_params=_cparams(("arbitrary", "arbitrary")),
        name="pool",
    )(ub, w_blk, g)


def _fox_kernel(qt_ref, kt_ref, qa_ref, ka_ref, va_ref, g_ref, o_ref, m_ref, acc_ref):
    p_id = pl.program_id(1)
    qi = qt_ref[p_id]
    ki = kt_ref[p_id]
    tq = qa_ref.shape[1]
    tk = ka_ref.shape[1]

    @pl.when(ki == 0)
    def _():
        m_ref[...] = jnp.full_like(m_ref, NEG_INF)
        acc_ref[...] = jnp.zeros_like(acc_ref)

    row = qi * tq + lax.broadcasted_iota(I32, (tq, tk), 0)
    col = ki * tk + lax.broadcasted_iota(I32, (tq, tk), 1)
    causal = row >= col
    for hd in range(N_HEADS):
        s = lax.dot_general(qa_ref[hd], ka_ref[hd], (((1,), (1,)), ((), ())),
                            preferred_element_type=F32)
        s = jnp.where(causal, s, NEG_INF)
        m_prev = m_ref[hd]
        m_new = jnp.maximum(m_prev, jnp.max(s, axis=-1, keepdims=True))
        alpha = jnp.exp(m_prev - m_new)
        p = jnp.exp(s - m_new)
        acc_ref[hd] = alpha * acc_ref[hd] + jnp.dot(p.astype(BF16), va_ref[hd],
                                                    preferred_element_type=F32)
        m_ref[hd] = m_new

    @pl.when(ki == qi)
    def _():
        heads = []
        for hd in range(N_HEADS):
            acc = acc_ref[hd]
            heads.append(acc / acc[:, HEAD_DIM:HEAD_DIM + 1])
        _group_norm_pairs([_join_heads(heads[0], heads[1]), _join_heads(heads[2], heads[3])],
                          g_ref, o_ref)


def _fox(qa, ka, va, bsz, seq, g):
    n = qa.shape[1]
    t = min(T_ATT, seq)
    nq = seq // t
    pairs = [(q, k) for q in range(nq) for k in range(q + 1)]
    qt = jnp.asarray([p[0] for p in pairs], I32)
    kt = jnp.asarray([p[1] for p in pairs], I32)
    qmap = lambda b, p, qt, kt: (0, b * nq + qt[p], 0)
    kmap = lambda b, p, qt, kt: (0, b * nq + kt[p], 0)
    grid_spec = pltpu.PrefetchScalarGridSpec(
        num_scalar_prefetch=2,
        grid=(bsz, len(pairs)),
        in_specs=[pl.BlockSpec((N_HEADS, t, LANES), qmap),
                  pl.BlockSpec((N_HEADS, t, LANES), kmap),
                  pl.BlockSpec((N_HEADS, t, LANES), kmap),
                  pl.BlockSpec((1, GROUP_WIDTH), lambda b, p, qt, kt: (0, 0))],
        out_specs=pl.BlockSpec((t, GROUP_WIDTH), lambda b, p, qt, kt: (b * nq + qt[p], 0)),
        scratch_shapes=[pltpu.VMEM((N_HEADS, t, 1), F32),
                        pltpu.VMEM((N_HEADS, t, LANES), F32)],
    )
    return pl.pallas_call(
        _fox_kernel,
        grid_spec=grid_spec,
        out_shape=jax.ShapeDtypeStruct((n, GROUP_WIDTH), BF16),
        compiler_params=_cparams(("arbitrary", "arbitrary")),
        name="fox",
    )(qt, kt, qa, ka, va, g)


def _memkv_kernel(mem_ref, g_ref, w_ref, kg_ref, mk_ref, mv_ref):
    x = mem_ref[0]
    h = x * lax.rsqrt(jnp.mean(x * x, axis=-1, keepdims=True) + EPS) * g_ref[...]
    kv = jnp.dot(h.astype(BF16), w_ref[...], preferred_element_type=F32)
    lane = _lane_iota((x.shape[0], LANES))
    for hd in range(N_HEADS):
        pair, odd = hd // 2, hd % 2
        ks = _head_slab(kv[:, pair * LANES:(pair + 1) * LANES], odd)
        mk_ref[0, hd] = _head_norm(ks, kg_ref[...]).astype(BF16)
        vs = _head_slab(kv[:, GROUP_WIDTH + pair * LANES:GROUP_WIDTH + (pair + 1) * LANES], odd)
        mv_ref[0, hd] = jnp.where(lane == HEAD_DIM, 1.0, vs).astype(BF16)


def _memkv(mem, g, w_kv, kg):
    bsz, m, d = mem.shape
    c2 = lambda b: (0, 0)
    out = jax.ShapeDtypeStruct((bsz, N_HEADS, m, LANES), BF16)
    return pl.pallas_call(
        _memkv_kernel,
        grid=(bsz,),
        in_specs=[pl.BlockSpec((1, m, d), lambda b: (b, 0, 0)),
                  pl.BlockSpec((1, d), c2),
                  pl.BlockSpec((d, 2 * GROUP_WIDTH), c2),
                  pl.BlockSpec((1, LANES), c2)],
        out_specs=(pl.BlockSpec((1, N_HEADS, m, LANES), lambda b: (b, 0, 0, 0)),
                   pl.BlockSpec((1, N_HEADS, m, LANES), lambda b: (b, 0, 0, 0))),
        out_shape=(out, out),
        compiler_params=_cparams(("arbitrary",)),
        name="memkv",
    )(mem, g, w_kv, kg)


def _memattn_kernel(qm_ref, mk_ref, mv_ref, g_ref, o_ref):
    heads = []
    for hd in range(N_HEADS):
        s = lax.dot_general(qm_ref[hd], mk_ref[0, hd], (((1,), (1,)), ((), ())),
                            preferred_element_type=F32)
        p = jnp.exp(s - jnp.max(s, axis=-1, keepdims=True))
        acc = jnp.dot(p.astype(BF16), mv_ref[0, hd], preferred_element_type=F32)
        heads.append(acc / acc[:, HEAD_DIM:HEAD_DIM + 1])
    _group_norm_pairs([_join_heads(heads[0], heads[1]), _join_heads(heads[2], heads[3])],
                      g_ref, o_ref)


def _memattn(qm, mk, mv, bsz, seq, g):
    n = qm.shape[1]
    m = mk.shape[2]
    t = min(T_ATT, seq)
    nt = seq // t
    return pl.pallas_call(
        _memattn_kernel,
        grid=(bsz, nt),
        in_specs=[pl.BlockSpec((N_HEADS, t, LANES), lambda b, j: (0, b * nt + j, 0)),
                  pl.BlockSpec((1, N_HEADS, m, LANES), lambda b, j: (b, 0, 0, 0)),
                  pl.BlockSpec((1, N_HEADS, m, LANES), lambda b, j: (b, 0, 0, 0)),
                  pl.BlockSpec((1, GROUP_WIDTH), lambda b, j: (0, 0))],
        out_specs=pl.BlockSpec((t, GROUP_WIDTH), lambda b, j: (b * nt + j, 0)),
        out_shape=jax.ShapeDtypeStruct((n, GROUP_WIDTH), BF16),
        compiler_params=_cparams(("arbitrary", "arbitrary")),
        name="memattn",
    )(qm, mk, mv, g)


def _outproj_kernel(x_ref, ya_ref, yb_ref, yc_ref, ym_ref, w_ref, fg_ref, rwt_ref, rb_ref, us_ref,
                    x1_ref, hn_ref, idx_ref, gate_ref, rank_ref, cnt_ref, carry_ref):
    @pl.when(pl.program_id(0) == 0)
    def _():
        carry_ref[...] = jnp.zeros_like(carry_ref)

    acc = x_ref[...]
    for i, y_ref in enumerate((ya_ref, yb_ref, yc_ref, ym_ref)):
        acc = acc + jnp.dot(y_ref[...], w_ref[i * GROUP_WIDTH:(i + 1) * GROUP_WIDTH, :],
                            preferred_element_type=F32)
    x1_ref[...] = acc
    hn = acc * lax.rsqrt(jnp.mean(acc * acc, axis=-1, keepdims=True) + EPS) * fg_ref[...]
    hn_ref[...] = hn
    logits = lax.dot_general(rwt_ref[...], hn, (((1,), (1,)), ((), ())),
                             precision=lax.Precision.HIGHEST,
                             preferred_element_type=F32) + rb_ref[...]
    n_exp, tm = logits.shape
    e_iota = lax.broadcasted_iota(I32, (n_exp, tm), 0).astype(F32)
    work = logits
    vals, onehots = [], []
    for k in range(TOP_K):
        m = jnp.max(work, axis=0, keepdims=True)
        sel = jnp.min(jnp.where(work == m, e_iota, float(n_exp)), axis=0, keepdims=True)
        hot = e_iota == sel
        idx_ref[k:k + 1, :] = sel.astype(I32)
        vals.append(m)
        onehots.append(hot.astype(F32))
        work = jnp.where(hot, NEG_INF, work)
    exps = [jnp.exp(v - vals[0]) for v in vals]
    denom = exps[0] + exps[1] + exps[2] + exps[3]
    for k in range(TOP_K):
        gate_ref[k:k + 1, :] = exps[k] / denom
    stacked = jnp.concatenate(onehots, axis=0).astype(BF16)
    prefix = jnp.dot(stacked, us_ref[...], preferred_element_type=F32)
    base = carry_ref[...]
    for k in range(TOP_K):
        hot = onehots[k]
        pk = prefix[k * n_exp:(k + 1) * n_exp, :]
        rank = jnp.sum(hot * (pk + base[:, 0:1]), axis=0, keepdims=True)
        rank_ref[k:k + 1, :] = rank.astype(I32)
        base = base + jnp.sum(hot, axis=1, keepdims=True)
    carry_ref[...] = base
    cnt_ref[...] = base


def _outproj(x2, ya, yb, yc, ym, w_out, fg, rwt, rb, ustrict):
    n, d = x2.shape
    n_exp = rwt.shape[0]
    tm = min(TM_OUT, n)
    row = lambda i: (i, 0)
    col = lambda i: (0, i)
    const = lambda i: (0, 0)
    return pl.pallas_call(
        _outproj_kernel,
        grid=(n // tm,),
        in_specs=[pl.BlockSpec((tm, d), row)]
        + [pl.BlockSpec((tm, GROUP_WIDTH), row)] * 4
        + [pl.BlockSpec((d, d), const),
           pl.BlockSpec((1, d), const),
           pl.BlockSpec((n_exp, d), const),
           pl.BlockSpec((n_exp, 1), const),
           pl.BlockSpec((tm, tm), const)],
        out_specs=(pl.BlockSpec((tm, d), row),
                   pl.BlockSpec((tm, d), row),
                   pl.BlockSpec((TOP_K, tm), col),
                   pl.BlockSpec((TOP_K, tm), col),
                   pl.BlockSpec((TOP_K, tm), col),
                   pl.BlockSpec((n_exp, LANES), const)),
        out_shape=(jax.ShapeDtypeStruct((n, d), F32),
                   jax.ShapeDtypeStruct((n, d), F32),
                   jax.ShapeDtypeStruct((TOP_K, n), I32),
                   jax.ShapeDtypeStruct((TOP_K, n), F32),
                   jax.ShapeDtypeStruct((TOP_K, n), I32),
                   jax.ShapeDtypeStruct((n_exp, LANES), F32)),
        scratch_shapes=[pltpu.VMEM((n_exp, LANES), F32)],
        compiler_params=_cparams(("arbitrary",)),
        name="outproj",
    )(x2, ya, yb, yc, ym, w_out, fg, rwt, rb, ustrict)


def _plan_kernel(cnt_ref, idx_ref, rank_ref, dest_ref, meta_ref, start_ref, *, n_exp, block_rows):
    def body(e, off):
        start_ref[e] = off
        return off + (cnt_ref[e] + block_rows - 1) // block_rows * block_rows

    total = lax.fori_loop(0, n_exp, body, jnp.int32(0))
    idx = idx_ref[...]
    dest = rank_ref[...]
    blk_start = lax.broadcasted_iota(I32, meta_ref.shape, 1) * block_rows
    blk_e = jnp.zeros(meta_ref.shape, I32)
    for e in range(n_exp):
        dest = dest + jnp.where(idx == e, start_ref[e], 0)
        if e > 0:
            blk_e = blk_e + (blk_start >= start_ref[e]).astype(I32)
    dest_ref[...] = dest
    sub = lax.broadcasted_iota(I32, meta_ref.shape, 0)
    meta_ref[...] = jnp.where(sub == 0, blk_e, total // block_rows)


def _plan(cnt, idx, rank, n_blocks):
    n_exp = cnt.shape[0]
    nb_pad = (n_blocks + LANES - 1) // LANES * LANES
    return pl.pallas_call(
        functools.partial(_plan_kernel, n_exp=n_exp, block_rows=TM_EXPERT),
        in_specs=[pl.BlockSpec(memory_space=pltpu.SMEM),
                  pl.BlockSpec(memory_space=pltpu.VMEM),
                  pl.BlockSpec(memory_space=pltpu.VMEM)],
        out_specs=(pl.BlockSpec(memory_space=pltpu.VMEM),
                   pl.BlockSpec(memory_space=pltpu.VMEM)),
        out_shape=(jax.ShapeDtypeStruct(idx.shape, I32),
                   jax.ShapeDtypeStruct((SUBLANES, nb_pad), I32)),
        scratch_shapes=[pltpu.SMEM((n_exp,), I32)],
        compiler_params=pltpu.CompilerParams(vmem_limit_bytes=VMEM_LIMIT),
        name="plan",
    )(cnt, idx, rank)


def _dispatch_kernel(dest_ref, hn_ref, xs_in_ref, xs_ref, sem):
    del xs_in_ref
    i = pl.program_id(0)
    tm = dest_ref.shape[1]

    def copy(k, t):
        return pltpu.make_async_copy(hn_ref.at[pl.ds(i * tm + t, 1)],
                                     xs_ref.at[pl.ds(dest_ref[k, t], 1)], sem)

    def start(t, c):
        for k in range(TOP_K):
            copy(k, t).start()
        return c

    def wait(t, c):
        for k in range(TOP_K):
            copy(k, t).wait()
        return c

    lax.fori_loop(0, tm, start, 0)
    lax.fori_loop(0, tm, wait, 0)


def _dispatch(dest, hn, cap):
    n, d = hn.shape
    tm = min(T_MOVE, n)
    zeros = jnp.zeros((cap, d), hn.dtype)
    return pl.pallas_call(
        _dispatch_kernel,
        grid=(n // tm,),
        in_specs=[pl.BlockSpec((TOP_K, tm), lambda i: (0, i), memory_space=pltpu.SMEM),
                  pl.BlockSpec(memory_space=pl.ANY),
                  pl.BlockSpec(memory_space=pl.ANY)],
        out_specs=pl.BlockSpec(memory_space=pl.ANY),
        out_shape=jax.ShapeDtypeStruct((cap, d), hn.dtype),
        scratch_shapes=[pltpu.SemaphoreType.DMA],
        input_output_aliases={2: 0},
        compiler_params=_cparams(("arbitrary",)),
        name="dispatch",
    )(dest, hn, zeros)


def _collect_kernel(dest_ref, ys_ref, yg_ref, sem):
    i = pl.program_id(0)
    tm = dest_ref.shape[1]

    def copy(k, t):
        return pltpu.make_async_copy(ys_ref.at[pl.ds(dest_ref[k, t], 1)],
                                     yg_ref.at[k, pl.ds(i * tm + t, 1)], sem)

    def start(t, c):
        for k in range(TOP_K):
            copy(k, t).start()
        return c

    def wait(t, c):
        for k in range(TOP_K):
            copy(k, t).wait()
        return c

    lax.fori_loop(0, tm, start, 0)
    lax.fori_loop(0, tm, wait, 0)


def _collect(dest, ys, n):
    d = ys.shape[1]
    tm = min(T_MOVE, n)
    return pl.pallas_call(
        _collect_kernel,
        grid=(n // tm,),
        in_specs=[pl.BlockSpec((TOP_K, tm), lambda i: (0, i), memory_space=pltpu.SMEM),
                  pl.BlockSpec(memory_space=pl.ANY)],
        out_specs=pl.BlockSpec(memory_space=pl.ANY),
        out_shape=jax.ShapeDtypeStruct((TOP_K, n, d), ys.dtype),
        scratch_shapes=[pltpu.SemaphoreType.DMA],
        compiler_params=_cparams(("arbitrary",)),
        name="collect",
    )(dest, ys)


def _experts_kernel(meta_ref, xs_ref, wg_ref, bg_ref, wu_ref, bu_ref, wd_ref, bd_ref, ys_ref):
    in_use = pl.program_id(0) < meta_ref[1, 0]

    @pl.when(jnp.logical_not(in_use))
    def _():
        ys_ref[...] = jnp.zeros_like(ys_ref)

    @pl.when(in_use)
    def _():
        x = xs_ref[...].astype(BF16)
        gl = jnp.dot(x, wg_ref[0], preferred_element_type=F32) + bg_ref[0]
        up = jnp.dot(x, wu_ref[0], preferred_element_type=F32) + bu_ref[0]
        gl = jnp.minimum(gl, SWIGLU_LIMIT)
        up = jnp.clip(up, -SWIGLU_LIMIT, SWIGLU_LIMIT)
        act = gl * (1.0 / (1.0 + jnp.exp(-SWIGLU_ALPHA * gl)))
        hmid = ((up + 1.0) * act).astype(BF16)
        ys_ref[...] = jnp.dot(hmid, wd_ref[0], preferred_element_type=F32) + bd_ref[0]


def _experts(meta, xs, wg, bg, wu, bu, wd, bd):
    cap, d = xs.shape
    de = wg.shape[2]
    tm = TM_EXPERT
    n_blocks = cap // tm

    def blk(i, meta):
        return jnp.minimum(i, meta[1, 0] - 1)

    rows = lambda i, meta: (blk(i, meta), 0)
    wmap = lambda i, meta: (meta[0, blk(i, meta)], 0, 0)
    grid_spec = pltpu.PrefetchScalarGridSpec(
        num_scalar_prefetch=1,
        grid=(n_blocks,),
        in_specs=[pl.BlockSpec((tm, d), rows),
                  pl.BlockSpec((1, d, de), wmap),
                  pl.BlockSpec((1, 1, de), wmap),
                  pl.BlockSpec((1, d, de), wmap),
                  pl.BlockSpec((1, 1, de), wmap),
                  pl.BlockSpec((1, de, d), wmap),
                  pl.BlockSpec((1, 1, d), wmap)],
        out_specs=pl.BlockSpec((tm, d), lambda i, meta: (i, 0)),
    )
    return pl.pallas_call(
        _experts_kernel,
        grid_spec=grid_spec,
        out_shape=jax.ShapeDtypeStruct((cap, d), F32),
        compiler_params=_cparams(("arbitrary",)),
        name="experts",
    )(meta, xs, wg, bg, wu, bu, wd, bd)


def _combine_kernel(x1_ref, gate_ref, yg_ref, o_ref):
    gates = jnp.transpose(gate_ref[...])
    acc = x1_ref[...]
    for k in range(TOP_K):
        acc = acc + gates[:, k:k + 1] * yg_ref[k]
    o_ref[...] = acc


def _combine(x1, gates8, yg):
    n, d = x1.shape
    tm = min(TM_COMBINE, n)
    return pl.pallas_call(
        _combine_kernel,
        grid=(n // tm,),
        in_specs=[pl.BlockSpec((tm, d), lambda i: (i, 0)),
                  pl.BlockSpec((LANES, tm), lambda i: (0, i)),
                  pl.BlockSpec((TOP_K, tm, d), lambda i: (0, i, 0))],
        out_specs=pl.BlockSpec((tm, d), lambda i: (i, 0)),
        out_shape=jax.ShapeDtypeStruct((n, d), F32),
        compiler_params=_cparams(("arbitrary",)),
        name="combine",
    )(x1, gates8, yg)


def _pad_lanes(v, width=LANES):
    v = v.astype(F32).reshape(1, -1)
    return jnp.pad(v, ((0, 0), (0, width - v.shape[1])))


def _layer(x2, mem, bsz, seq, p):
    n, d = x2.shape
    f32 = F32
    w_in = p['w_in']
    off_q = 2 * GROUP_WIDTH
    off_k, off_v, off_f = off_q + GROUP_WIDTH, off_q + 2 * GROUP_WIDTH, off_q + 3 * GROUP_WIDTH
    off_qm = off_f + N_HEADS
    w_f = w_in[:, off_f:off_qm]
    w_f_pad = jnp.pad(jnp.concatenate([w_f, w_f, w_f], axis=1), ((0, 0), (0, LANES - 3 * N_HEADS)))
    w_all = jnp.concatenate([w_in[:, :off_f], w_in[:, off_qm:], w_f_pad], axis=1).astype(BF16)
    fb = p['fox_forget_b'].astype(f32)
    fb_pad = _pad_lanes(jnp.concatenate([fb, fb, fb]))
    tm_in = min(TM_INPROJ, seq)
    tri = jnp.tril(jnp.ones((tm_in, tm_in), f32)).astype(BF16)

    ua, ub, qa, ka, va, qm = _inproj(
        x2, seq, p['mix_norm_g'].reshape(1, d).astype(f32), w_all,
        _pad_lanes(p['fox_q_g']), _pad_lanes(p['fox_k_g']), _pad_lanes(p['mem_q_g']), fb_pad, tri)

    bmat, coef, cmat = _s5_constants(p['ssm_lambda_re'], p['ssm_lambda_im'], p['ssm_log_dt'],
                                     p['ssm_b_re'], p['ssm_b_im'], p['ssm_c_re'], p['ssm_c_im'])
    ya = _s5(ua, bsz, seq, bmat, coef, cmat,
             p['ssm_d'].reshape(1, GROUP_WIDTH).astype(f32), p['ssm_glu_w'].astype(BF16),
             p['ssm_glu_b'].reshape(1, GROUP_WIDTH).astype(f32),
             p['ssm_out_g'].reshape(1, GROUP_WIDTH).astype(f32))

    pw = p['pool_w'].astype(f32)
    w_blk = jnp.zeros((GROUP_WIDTH, GROUP_WIDTH), f32)
    for gi in range(len(POOL_WINDOWS)):
        w_blk = w_blk.at[gi * HEAD_DIM:(gi + 1) * HEAD_DIM, gi * HEAD_DIM:(gi + 1) * HEAD_DIM].set(pw[gi])
    yb = _pool(ub, bsz, seq, w_blk.astype(BF16), p['pool_scale'].reshape(1, GROUP_WIDTH).astype(f32))

    yc = _fox(qa, ka, va, bsz, seq, p['fox_out_g'].reshape(1, GROUP_WIDTH).astype(f32))

    mk, mv = _memkv(mem, p['mem_norm_g'].reshape(1, d).astype(f32), p['mem_w_kv'].astype(BF16),
                    _pad_lanes(p['mem_k_g']))
    ym = _memattn(qm, mk, mv, bsz, seq, p['mem_out_g'].reshape(1, GROUP_WIDTH).astype(f32))

    n_exp = p['router_w'].shape[1]
    tm_out = min(TM_OUT, n)
    ustrict = jnp.triu(jnp.ones((tm_out, tm_out), f32), k=1).astype(BF16)
    x1, hn, idx, gates, rank, cnt = _outproj(
        x2, ya, yb, yc, ym, p['w_out'].astype(BF16), p['ffn_norm_g'].reshape(1, d).astype(f32),
        jnp.transpose(p['router_w']).astype(f32), p['router_b'].reshape(n_exp, 1).astype(f32), ustrict)

    cap = n * TOP_K + n_exp * TM_EXPERT
    dest, meta = _plan(cnt[:, 0].astype(I32), idx, rank, cap // TM_EXPERT)
    xs = _dispatch(dest, hn, cap)
    ys = _experts(meta, xs,
                  p['exp_w_gate'].astype(BF16), p['exp_b_gate'][:, None, :].astype(f32),
                  p['exp_w_up'].astype(BF16), p['exp_b_up'][:, None, :].astype(f32),
                  p['exp_w_down'].astype(BF16), p['exp_b_down'][:, None, :].astype(f32))
    yg = _collect(dest, ys, n)
    gates8 = jnp.pad(gates, ((0, LANES - TOP_K), (0, 0)))
    return _combine(x1, gates8, yg)


_PARAM_NAMES = ('mix_norm_g', 'w_in', 'ssm_lambda_re', 'ssm_lambda_im', 'ssm_log_dt',
                'ssm_b_re', 'ssm_b_im', 'ssm_c_re', 'ssm_c_im', 'ssm_d', 'ssm_glu_w', 'ssm_glu_b',
                'ssm_out_g', 'pool_w', 'pool_scale', 'fox_forget_b', 'fox_q_g', 'fox_k_g',
                'fox_out_g', 'mem_norm_g', 'mem_w_kv', 'mem_q_g', 'mem_k_g', 'mem_out_g', 'w_out',
                'ffn_norm_g', 'router_w', 'router_b', 'exp_w_gate', 'exp_b_gate', 'exp_w_up',
                'exp_b_up', 'exp_w_down', 'exp_b_down')


def kernel(x, mem, mix_norm_g, w_in, ssm_lambda_re, ssm_lambda_im, ssm_log_dt, ssm_b_re, ssm_b_im,
           ssm_c_re, ssm_c_im, ssm_d, ssm_glu_w, ssm_glu_b, ssm_out_g, pool_w, pool_scale,
           fox_forget_b, fox_q_g, fox_k_g, fox_out_g, mem_norm_g, mem_w_kv, mem_q_g, mem_k_g,
           mem_out_g, w_out, ffn_norm_g, router_w, router_b, exp_w_gate, exp_b_gate, exp_w_up,
           exp_b_up, exp_w_down, exp_b_down):
    stacked = dict(zip(_PARAM_NAMES, (
        mix_norm_g, w_in, ssm_lambda_re, ssm_lambda_im, ssm_log_dt, ssm_b_re, ssm_b_im, ssm_c_re,
        ssm_c_im, ssm_d, ssm_glu_w, ssm_glu_b, ssm_out_g, pool_w, pool_scale, fox_forget_b,
        fox_q_g, fox_k_g, fox_out_g, mem_norm_g, mem_w_kv, mem_q_g, mem_k_g, mem_out_g, w_out,
        ffn_norm_g, router_w, router_b, exp_w_gate, exp_b_gate, exp_w_up, exp_b_up, exp_w_down,
        exp_b_down)))
    bsz, seq, d = x.shape
    depth = w_in.shape[0]
    x2 = x.reshape(bsz * seq, d).astype(F32)
    mem = mem.astype(F32)
    for layer in range(depth):
        x2 = _layer(x2, mem, bsz, seq, {k: v[layer] for k, v in stacked.items()})
    return x2.reshape(bsz, seq, d).astype(x.dtype)
```

```python
import functools
import math

import jax
import jax.numpy as jnp
from jax import lax
from jax.experimental import pallas as pl
from jax.experimental.pallas import tpu as pltpu

F32 = jnp.float32
BF16 = jnp.bfloat16
I32 = jnp.int32

EPS = 1e-6
HEAD_DIM = 64
N_HEADS = 4
GROUP_WIDTH = 256
LANES = 128
SUBLANES = 8
ROW_CHUNKS = 8
SSM_GROUPS = 16
SSM_CH = 16
SSM_STATE = 64
SSM_LANES = SSM_GROUPS * SSM_STATE
POOL_WINDOWS = (2, 4, 8, 16)
POOL_HALO = 16
TOP_K = 4
SWIGLU_LIMIT = 7.0
SWIGLU_ALPHA = 1.702
VMEM_LIMIT = 56 * 1024 * 1024

TM_INPROJ = 512
T_S5 = 256
T_POOL = 512
T_ATT = 512
TM_OUT = 256
TM_EXPERT = 256
T_MOVE = 256

NEG_INF = float("-inf")


def _cparams(sem):
    return pltpu.CompilerParams(dimension_semantics=sem, vmem_limit_bytes=VMEM_LIMIT)


def _lane_iota(shape):
    return lax.broadcasted_iota(I32, shape, len(shape) - 1)


def _split3(x):
    hi = x.astype(BF16).astype(F32)
    r = x - hi
    mid = r.astype(BF16).astype(F32)
    lo = r - mid
    return hi, mid, lo


def _head_slab(slab, odd):
    lane = _lane_iota(slab.shape)
    if odd:
        slab = pltpu.roll(slab, HEAD_DIM, axis=1)
    return jnp.where(lane < HEAD_DIM, slab, 0.0)


def _head_norm(xh, gain):
    ss = jnp.sum(xh * xh, axis=-1, keepdims=True)
    return xh * lax.rsqrt(ss * (1.0 / HEAD_DIM) + EPS) * gain


def _join_heads(o_even, o_odd):
    lane = _lane_iota(o_even.shape)
    return jnp.where(lane < HEAD_DIM, o_even, pltpu.roll(o_odd, HEAD_DIM, axis=1))


def _store_token_tiles(ref, val):
    t = val.shape[0]
    for s in range(ROW_CHUNKS):
        ref[pl.ds(s, t, stride=ROW_CHUNKS), :] = val[:, s * LANES:(s + 1) * LANES]


def _load_token_tiles(ref, t):
    return jnp.concatenate([ref[pl.ds(s, t, stride=ROW_CHUNKS), :] for s in range(ROW_CHUNKS)],
                           axis=1)


def _group_norm_pairs(pairs, gain_ref, out_ref):
    ss = sum(jnp.sum(p * p, axis=-1, keepdims=True) for p in pairs)
    scale = lax.rsqrt(ss * (1.0 / GROUP_WIDTH) + EPS)
    for i, p in enumerate(pairs):
        sl = slice(i * LANES, (i + 1) * LANES)
        out_ref[:, sl] = (p * scale * gain_ref[:, sl]).astype(out_ref.dtype)


COL_A, COL_B, COL_Q, COL_K, COL_V, COL_QM, COL_F = 0, 256, 512, 768, 1024, 1280, 1536
IN_COLS_PADDED = COL_F + LANES


def _inproj_kernel(x_ref, g_ref, w_ref, qg_ref, kg_ref, mqg_ref, fb_ref, tri_ref,
                   ua_ref, ub_ref, qa_ref, ka_ref, va_ref, qm_ref, carry_ref, *, tiles_per_seq):
    i = pl.program_id(0)

    @pl.when(i % tiles_per_seq == 0)
    def _():
        carry_ref[...] = jnp.zeros_like(carry_ref)

    x = x_ref[...]
    h = x * lax.rsqrt(jnp.mean(x * x, axis=-1, keepdims=True) + EPS) * g_ref[...]
    proj = jnp.dot(h.astype(BF16), w_ref[...], preferred_element_type=F32)
    ua_ref[...] = proj[:, COL_A:COL_A + GROUP_WIDTH]
    ub_ref[...] = proj[:, COL_B:COL_B + GROUP_WIDTH]

    z = proj[:, COL_F:COL_F + LANES] + fb_ref[...]
    lane = _lane_iota(z.shape)
    logf = jnp.minimum(z, 0.0) - jnp.log(1.0 + jnp.exp(-jnp.abs(z)))
    hi, mid, lo = _split3(logf)
    packed = jnp.where(lane < 4, hi, jnp.where(lane < 8, mid, jnp.where(lane < 12, lo, 0.0)))
    cs = jnp.dot(tri_ref[...], packed.astype(BF16), preferred_element_type=F32)
    cum = cs + pltpu.roll(cs, LANES - 4, axis=1) + pltpu.roll(cs, LANES - 8, axis=1)
    cum = cum + carry_ref[...]
    carry_ref[...] = cum[cum.shape[0] - 1:, :]

    scale = HEAD_DIM ** -0.5
    for hd in range(N_HEADS):
        pair, odd = hd // 2, hd % 2
        c_hi, c_mid, c_lo = _split3(cum[:, hd:hd + 1])
        qs = _head_slab(proj[:, COL_Q + pair * LANES:COL_Q + (pair + 1) * LANES], odd)
        qn = _head_norm(qs, qg_ref[...]) * scale
        q_aug = jnp.where(lane == 64, c_hi, jnp.where(lane == 65, c_mid, jnp.where(
            lane == 66, c_lo, jnp.where((lane >= 67) & (lane < 70), 1.0, qn))))
        qa_ref[hd] = q_aug.astype(BF16)
        ks = _head_slab(proj[:, COL_K + pair * LANES:COL_K + (pair + 1) * LANES], odd)
        kn = _head_norm(ks, kg_ref[...])
        k_aug = jnp.where((lane >= 64) & (lane < 67), 1.0, jnp.where(lane == 67, -c_hi, jnp.where(
            lane == 68, -c_mid, jnp.where(lane == 69, -c_lo, kn))))
        ka_ref[hd] = k_aug.astype(BF16)
        vs = _head_slab(proj[:, COL_V + pair * LANES:COL_V + (pair + 1) * LANES], odd)
        va_ref[hd] = jnp.where(lane == 64, 1.0, vs).astype(BF16)
        ms = _head_slab(proj[:, COL_QM + pair * LANES:COL_QM + (pair + 1) * LANES], odd)
        qm_ref[hd] = (_head_norm(ms, mqg_ref[...]) * scale).astype(BF16)


def _inproj(x2, seq, mix_g, w_all, qg, kg, mqg, fb, tri):
    n, d = x2.shape
    tm = min(TM_INPROJ, seq)
    grid = (n // tm,)
    const = lambda i: (0, 0)
    row = lambda i: (i, 0)
    hrow = lambda i: (0, i, 0)
    out_shape = (
        jax.ShapeDtypeStruct((n, GROUP_WIDTH), F32),
        jax.ShapeDtypeStruct((n, GROUP_WIDTH), F32),
        jax.ShapeDtypeStruct((N_HEADS, n, LANES), BF16),
        jax.ShapeDtypeStruct((N_HEADS, n, LANES), BF16),
        jax.ShapeDtypeStruct((N_HEADS, n, LANES), BF16),
        jax.ShapeDtypeStruct((N_HEADS, n, LANES), BF16),
    )
    return pl.pallas_call(
        functools.partial(_inproj_kernel, tiles_per_seq=seq // tm),
        grid=grid,
        in_specs=[
            pl.BlockSpec((tm, d), row),
            pl.BlockSpec((1, d), const),
            pl.BlockSpec((d, IN_COLS_PADDED), const),
            pl.BlockSpec((1, LANES), const),
            pl.BlockSpec((1, LANES), const),
            pl.BlockSpec((1, LANES), const),
            pl.BlockSpec((1, LANES), const),
            pl.BlockSpec((tm, tm), const),
        ],
        out_specs=(
            pl.BlockSpec((tm, GROUP_WIDTH), row),
            pl.BlockSpec((tm, GROUP_WIDTH), row),
            pl.BlockSpec((N_HEADS, tm, LANES), hrow),
            pl.BlockSpec((N_HEADS, tm, LANES), hrow),
            pl.BlockSpec((N_HEADS, tm, LANES), hrow),
            pl.BlockSpec((N_HEADS, tm, LANES), hrow),
        ),
        out_shape=out_shape,
        scratch_shapes=[pltpu.VMEM((1, LANES), F32)],
        compiler_params=_cparams(("arbitrary",)),
        name="inproj",
    )(x2, mix_g, w_all, qg, kg, mqg, fb, tri)


def _s5_kernel(u_ref, bmat_ref, coef_ref, cmat_ref, d_ref, gluw_ref, glub_ref, og_ref,
               o_ref, x_scr, carry_ref):
    @pl.when(pl.program_id(1) == 0)
    def _():
        carry_ref[...] = jnp.zeros_like(carry_ref)

    u = u_ref[...]
    t = u.shape[0]
    x_scr[...] = jnp.dot(u.astype(BF16), bmat_ref[...], preferred_element_type=F32)
    n_lane_blocks = SSM_LANES // LANES

    def group(gi, carry):
        r0 = pl.multiple_of(gi * SUBLANES, SUBLANES)
        rows = pl.ds(r0, SUBLANES)
        new = []
        for lb in range(n_lane_blocks):
            re_sl = slice(lb * LANES, (lb + 1) * LANES)
            im_sl = slice(SSM_LANES + lb * LANES, SSM_LANES + (lb + 1) * LANES)
            xr = x_scr[rows, re_sl]
            xi = x_scr[rows, im_sl]
            for s, k in enumerate((1, 2, 4)):
                cr = coef_ref[2 * s, :, re_sl]
                ci = coef_ref[2 * s + 1, :, re_sl]
                sr = pltpu.roll(xr, k, axis=0)
                si = pltpu.roll(xi, k, axis=0)
                xr, xi = xr + cr * sr - ci * si, xi + cr * si + ci * sr
            pr = coef_ref[6, :, re_sl]
            pi_ = coef_ref[7, :, re_sl]
            cbr, cbi = carry[2 * lb], carry[2 * lb + 1]
            xr, xi = xr + pr * cbr - pi_ * cbi, xi + pr * cbi + pi_ * cbr
            x_scr[rows, re_sl] = xr
            x_scr[rows, im_sl] = xi
            new.append(jnp.broadcast_to(xr[SUBLANES - 1:, :], xr.shape))
            new.append(jnp.broadcast_to(xi[SUBLANES - 1:, :], xi.shape))
        return tuple(new)

    carry0 = []
    for lb in range(n_lane_blocks):
        carry0.append(carry_ref[:, lb * LANES:(lb + 1) * LANES])
        carry0.append(carry_ref[:, SSM_LANES + lb * LANES:SSM_LANES + (lb + 1) * LANES])
    carry = lax.fori_loop(0, t // SUBLANES, group, tuple(carry0))
    for lb in range(n_lane_blocks):
        carry_ref[:, lb * LANES:(lb + 1) * LANES] = carry[2 * lb]
        carry_ref[:, SSM_LANES + lb * LANES:SSM_LANES + (lb + 1) * LANES] = carry[2 * lb + 1]

    y = jnp.dot(x_scr[...].astype(BF16), cmat_ref[...], preferred_element_type=F32) + d_ref[...] * u
    z = jax.nn.gelu(y, approximate=True)
    gate = jnp.dot(z.astype(BF16), gluw_ref[...], preferred_element_type=F32) + glub_ref[...]
    out = z * (1.0 / (1.0 + jnp.exp(-gate)))
    _group_norm_pairs([out[:, :LANES], out[:, LANES:]], og_ref, o_ref)


def _s5(ua, bsz, seq, bmat, coef, cmat, dvec, gluw, glub, og):
    n = ua.shape[0]
    t = min(T_S5, seq)
    nt = seq // t
    row = lambda b, j: (b * nt + j, 0)
    c2 = lambda b, j: (0, 0)
    c3 = lambda b, j: (0, 0, 0)
    return pl.pallas_call(
        _s5_kernel,
        grid=(bsz, nt),
        in_specs=[
            pl.BlockSpec((t, GROUP_WIDTH), row),
            pl.BlockSpec((GROUP_WIDTH, 2 * SSM_LANES), c2),
            pl.BlockSpec((8, SUBLANES, SSM_LANES), c3),
            pl.BlockSpec((2 * SSM_LANES, GROUP_WIDTH), c2),
            pl.BlockSpec((1, GROUP_WIDTH), c2),
            pl.BlockSpec((GROUP_WIDTH, GROUP_WIDTH), c2),
            pl.BlockSpec((1, GROUP_WIDTH), c2),
            pl.BlockSpec((1, GROUP_WIDTH), c2),
        ],
        out_specs=pl.BlockSpec((t, GROUP_WIDTH), row),
        out_shape=jax.ShapeDtypeStruct((n, GROUP_WIDTH), BF16),
        scratch_shapes=[pltpu.VMEM((t, 2 * SSM_LANES), F32),
                        pltpu.VMEM((SUBLANES, 2 * SSM_LANES), F32)],
        compiler_params=_cparams(("arbitrary", "arbitrary")),
        name="s5",
    )(ua, bmat, coef, cmat, dvec, gluw, glub, og)


def _s5_constants(lam_re, lam_im, log_dt, b_re, b_im, c_re, c_im):
    lr = lam_re.astype(F32)
    li = lam_im.astype(F32)
    dt = jnp.exp(log_dt.astype(F32))[:, None]
    mag = jnp.exp(lr * dt)
    a_re = mag * jnp.cos(li * dt)
    a_im = mag * jnp.sin(li * dt)
    den = lr * lr + li * li
    n_re = a_re - 1.0
    n_im = a_im
    k_re = (n_re * lr + n_im * li) / den
    k_im = (n_im * lr - n_re * li) / den
    br = b_re.astype(F32)
    bi = b_im.astype(F32)
    bb_re = k_re[..., None] * br - k_im[..., None] * bi
    bb_im = k_re[..., None] * bi + k_im[..., None] * br
    eye = jnp.eye(SSM_GROUPS, dtype=F32)
    bm_re = jnp.einsum('gph,gk->ghkp', bb_re, eye).reshape(GROUP_WIDTH, SSM_LANES)
    bm_im = jnp.einsum('gph,gk->ghkp', bb_im, eye).reshape(GROUP_WIDTH, SSM_LANES)
    bmat = jnp.concatenate([bm_re, bm_im], axis=1).astype(BF16)
    cm_re = jnp.einsum('ghp,gk->gpkh', c_re.astype(F32), eye).reshape(SSM_LANES, GROUP_WIDTH)
    cm_im = jnp.einsum('ghp,gk->gpkh', c_im.astype(F32), eye).reshape(SSM_LANES, GROUP_WIDTH)
    cmat = jnp.concatenate([cm_re, -cm_im], axis=0).astype(BF16)

    ar = a_re.reshape(1, SSM_LANES)
    ai = a_im.reshape(1, SSM_LANES)

    def cmul(x, y):
        return x[0] * y[0] - x[1] * y[1], x[0] * y[1] + x[1] * y[0]

    pows = [(ar, ai)]
    for _ in range(SUBLANES - 1):
        pows.append(cmul(pows[-1], (ar, ai)))
    rows = jnp.arange(SUBLANES, dtype=I32)[:, None]
    planes = []
    for k in (1, 2, 4):
        m = (rows >= k).astype(F32)
        planes += [m * pows[k - 1][0], m * pows[k - 1][1]]
    planes += [jnp.concatenate([p[0] for p in pows], axis=0),
               jnp.concatenate([p[1] for p in pows], axis=0)]
    coef = jnp.stack(planes, axis=0)
    return bmat, coef, cmat


def _pool_kernel(v_ref, w_ref, g_ref, o_ref, ext_ref):
    j = pl.program_id(1)
    t = v_ref.shape[0]

    @pl.when(j == 0)
    def _():
        ext_ref[0:POOL_HALO, :] = jnp.zeros((POOL_HALO, GROUP_WIDTH), F32)

    v = v_ref[...]
    ext_ref[POOL_HALO:POOL_HALO + t, :] = v
    cur = ext_ref[...]
    width = 1
    wins = {}
    while width < POOL_WINDOWS[-1]:
        cur = cur + pltpu.roll(cur, width, axis=0)
        width *= 2
        wins[width] = cur[POOL_HALO:, :]
    lane = _lane_iota(v.shape)
    pos = (j * t + lax.broadcasted_iota(I32, v.shape, 0) + 1).astype(F32)
    mean = None
    for gi, w in enumerate(POOL_WINDOWS):
        m = wins[w] / jnp.minimum(pos, float(w))
        mean = m if mean is None else jnp.where(lane >= gi * HEAD_DIM, m, mean)
    mixed = jnp.dot((mean - v).astype(BF16), w_ref[...], preferred_element_type=F32)
    _group_norm_pairs([mixed[:, :LANES], mixed[:, LANES:]], g_ref, o_ref)
    ext_ref[0:POOL_HALO, :] = v[t - POOL_HALO:, :]


def _pool(ub, bsz, seq, w_blk, g):
    n = ub.shape[0]
    t = min(T_POOL, seq)
    nt = seq // t
    row = lambda b, j: (b * nt + j, 0)
    c2 = lambda b, j: (0, 0)
    return pl.pallas_call(
        _pool_kernel,
        grid=(bsz, nt),
        in_specs=[pl.BlockSpec((t, GROUP_WIDTH), row),
                  pl.BlockSpec((GROUP_WIDTH, GROUP_WIDTH), c2),
                  pl.BlockSpec((1, GROUP_WIDTH), c2)],
        out_specs=pl.BlockSpec((t, GROUP_WIDTH), row),
        out_shape=jax.ShapeDtypeStruct((n, GROUP_WIDTH), BF16),
        scratch_shapes=[pltpu.VMEM((t + POOL_HALO, GROUP_WIDTH), F32)],
        compiler_params=_cparams(("arbitrary", "arbitrary")),
        name="pool",
    )(ub, w_blk, g)


def _fox_kernel(qt_ref, kt_ref, qa_ref, ka_ref, va_ref, g_ref, o_ref, m_ref, acc_ref):
    p_id = pl.program_id(1)
    qi = qt_ref[p_id]
    ki = kt_ref[p_id]
    tq = qa_ref.shape[1]
    tk = ka_ref.shape[1]

    @pl.when(ki == 0)
    def _():
        m_ref[...] = jnp.full_like(m_ref, NEG_INF)
        acc_ref[...] = jnp.zeros_like(acc_ref)

    row = qi * tq + lax.broadcasted_iota(I32, (tq, tk), 0)
    col = ki * tk + lax.broadcasted_iota(I32, (tq, tk), 1)
    causal = row >= col
    for hd in range(N_HEADS):
        s = lax.dot_general(qa_ref[hd], ka_ref[hd], (((1,), (1,)), ((), ())),
                            preferred_element_type=F32)
        s = jnp.where(causal, s, NEG_INF)
        m_prev = m_ref[hd]
        m_new = jnp.maximum(m_prev, jnp.max(s, axis=-1, keepdims=True))
        alpha = jnp.exp(m_prev - m_new)
        p = jnp.exp(s - m_new)
        acc_ref[hd] = alpha * acc_ref[hd] + jnp.dot(p.astype(BF16), va_ref[hd],
                                                    preferred_element_type=F32)
        m_ref[hd] = m_new

    @pl.when(ki == qi)
    def _():
        heads = []
        for hd in range(N_HEADS):
            acc = acc_ref[hd]
            heads.append(acc / acc[:, HEAD_DIM:HEAD_DIM + 1])
        _group_norm_pairs([_join_heads(heads[0], heads[1]), _join_heads(heads[2], heads[3])],
                          g_ref, o_ref)


def _fox(qa, ka, va, bsz, seq, g):
    n = qa.shape[1]
    t = min(T_ATT, seq)
    nq = seq // t
    pairs = [(q, k) for q in range(nq) for k in range(q + 1)]
    qt = jnp.asarray([p[0] for p in pairs], I32)
    kt = jnp.asarray([p[1] for p in pairs], I32)
    qmap = lambda b, p, qt, kt: (0, b * nq + qt[p], 0)
    kmap = lambda b, p, qt, kt: (0, b * nq + kt[p], 0)
    grid_spec = pltpu.PrefetchScalarGridSpec(
        num_scalar_prefetch=2,
        grid=(bsz, len(pairs)),
        in_specs=[pl.BlockSpec((N_HEADS, t, LANES), qmap),
                  pl.BlockSpec((N_HEADS, t, LANES), kmap),
                  pl.BlockSpec((N_HEADS, t, LANES), kmap),
                  pl.BlockSpec((1, GROUP_WIDTH), lambda b, p, qt, kt: (0, 0))],
        out_specs=pl.BlockSpec((t, GROUP_WIDTH), lambda b, p, qt, kt: (b * nq + qt[p], 0)),
        scratch_shapes=[pltpu.VMEM((N_HEADS, t, 1), F32),
                        pltpu.VMEM((N_HEADS, t, LANES), F32)],
    )
    return pl.pallas_call(
        _fox_kernel,
        grid_spec=grid_spec,
        out_shape=jax.ShapeDtypeStruct((n, GROUP_WIDTH), BF16),
        compiler_params=_cparams(("arbitrary", "arbitrary")),
        name="fox",
    )(qt, kt, qa, ka, va, g)


def _memkv_kernel(mem_ref, g_ref, w_ref, kg_ref, mk_ref, mv_ref):
    x = mem_ref[0]
    h = x * lax.rsqrt(jnp.mean(x * x, axis=-1, keepdims=True) + EPS) * g_ref[...]
    kv = jnp.dot(h.astype(BF16), w_ref[...], preferred_element_type=F32)
    lane = _lane_iota((x.shape[0], LANES))
    for hd in range(N_HEADS):
        pair, odd = hd // 2, hd % 2
        ks = _head_slab(kv[:, pair * LANES:(pair + 1) * LANES], odd)
        mk_ref[0, hd] = _head_norm(ks, kg_ref[...]).astype(BF16)
        vs = _head_slab(kv[:, GROUP_WIDTH + pair * LANES:GROUP_WIDTH + (pair + 1) * LANES], odd)
        mv_ref[0, hd] = jnp.where(lane == HEAD_DIM, 1.0, vs).astype(BF16)


def _memkv(mem, g, w_kv, kg):
    bsz, m, d = mem.shape
    c2 = lambda b: (0, 0)
    out = jax.ShapeDtypeStruct((bsz, N_HEADS, m, LANES), BF16)
    return pl.pallas_call(
        _memkv_kernel,
        grid=(bsz,),
        in_specs=[pl.BlockSpec((1, m, d), lambda b: (b, 0, 0)),
                  pl.BlockSpec((1, d), c2),
                  pl.BlockSpec((d, 2 * GROUP_WIDTH), c2),
                  pl.BlockSpec((1, LANES), c2)],
        out_specs=(pl.BlockSpec((1, N_HEADS, m, LANES), lambda b: (b, 0, 0, 0)),
                   pl.BlockSpec((1, N_HEADS, m, LANES), lambda b: (b, 0, 0, 0))),
        out_shape=(out, out),
        compiler_params=_cparams(("arbitrary",)),
        name="memkv",
    )(mem, g, w_kv, kg)


def _memattn_kernel(qm_ref, mk_ref, mv_ref, g_ref, o_ref):
    heads = []
    for hd in range(N_HEADS):
        s = lax.dot_general(qm_ref[hd], mk_ref[0, hd], (((1,), (1,)), ((), ())),
                            preferred_element_type=F32)
        p = jnp.exp(s - jnp.max(s, axis=-1, keepdims=True))
        acc = jnp.dot(p.astype(BF16), mv_ref[0, hd], preferred_element_type=F32)
        heads.append(acc / acc[:, HEAD_DIM:HEAD_DIM + 1])
    _group_norm_pairs([_join_heads(heads[0], heads[1]), _join_heads(heads[2], heads[3])],
                      g_ref, o_ref)


def _memattn(qm, mk, mv, bsz, seq, g):
    n = qm.shape[1]
    m = mk.shape[2]
    t = min(T_ATT, seq)
    nt = seq // t
    return pl.pallas_call(
        _memattn_kernel,
        grid=(bsz, nt),
        in_specs=[pl.BlockSpec((N_HEADS, t, LANES), lambda b, j: (0, b * nt + j, 0)),
                  pl.BlockSpec((1, N_HEADS, m, LANES), lambda b, j: (b, 0, 0, 0)),
                  pl.BlockSpec((1, N_HEADS, m, LANES), lambda b, j: (b, 0, 0, 0)),
                  pl.BlockSpec((1, GROUP_WIDTH), lambda b, j: (0, 0))],
        out_specs=pl.BlockSpec((t, GROUP_WIDTH), lambda b, j: (b * nt + j, 0)),
        out_shape=jax.ShapeDtypeStruct((n, GROUP_WIDTH), BF16),
        compiler_params=_cparams(("arbitrary", "arbitrary")),
        name="memattn",
    )(qm, mk, mv, g)


def _outproj_kernel(x_ref, ya_ref, yb_ref, yc_ref, ym_ref, w_ref, fg_ref, rwt_ref, rb_ref, us_ref,
                    x1_ref, hn_ref, idx_ref, gate_ref, rank_ref, cnt_ref, carry_ref):
    @pl.when(pl.program_id(0) == 0)
    def _():
        carry_ref[...] = jnp.zeros_like(carry_ref)

    acc = x_ref[...]
    for i, y_ref in enumerate((ya_ref, yb_ref, yc_ref, ym_ref)):
        acc = acc + jnp.dot(y_ref[...], w_ref[i * GROUP_WIDTH:(i + 1) * GROUP_WIDTH, :],
                            preferred_element_type=F32)
    x1_ref[...] = acc
    hn = acc * lax.rsqrt(jnp.mean(acc * acc, axis=-1, keepdims=True) + EPS) * fg_ref[...]
    _store_token_tiles(hn_ref, hn)
    logits = lax.dot_general(rwt_ref[...], hn, (((1,), (1,)), ((), ())),
                             precision=lax.Precision.HIGHEST,
                             preferred_element_type=F32) + rb_ref[...]
    n_exp, tm = logits.shape
    e_iota = lax.broadcasted_iota(I32, (n_exp, tm), 0).astype(F32)
    work = logits
    vals, onehots = [], []
    for k in range(TOP_K):
        m = jnp.max(work, axis=0, keepdims=True)
        sel = jnp.min(jnp.where(work == m, e_iota, float(n_exp)), axis=0, keepdims=True)
        hot = e_iota == sel
        idx_ref[k:k + 1, :] = sel.astype(I32)
        vals.append(m)
        onehots.append(hot.astype(F32))
        work = jnp.where(hot, NEG_INF, work)
    exps = [jnp.exp(v - vals[0]) for v in vals]
    denom = exps[0] + exps[1] + exps[2] + exps[3]
    for k in range(TOP_K):
        gate_ref[k:k + 1, :] = exps[k] / denom
    stacked = jnp.concatenate(onehots, axis=0).astype(BF16)
    prefix = jnp.dot(stacked, us_ref[...], preferred_element_type=F32)
    base = carry_ref[...]
    for k in range(TOP_K):
        hot = onehots[k]
        pk = prefix[k * n_exp:(k + 1) * n_exp, :]
        rank = jnp.sum(hot * (pk + base[:, 0:1]), axis=0, keepdims=True)
        rank_ref[k:k + 1, :] = rank.astype(I32)
        base = base + jnp.sum(hot, axis=1, keepdims=True)
    carry_ref[...] = base
    cnt_ref[...] = base


def _outproj(x2, ya, yb, yc, ym, w_out, fg, rwt, rb, ustrict):
    n, d = x2.shape
    n_exp = rwt.shape[0]
    tm = min(TM_OUT, n)
    row = lambda i: (i, 0)
    col = lambda i: (0, i)
    const = lambda i: (0, 0)
    return pl.pallas_call(
        _outproj_kernel,
        grid=(n // tm,),
        in_specs=[pl.BlockSpec((tm, d), row)]
        + [pl.BlockSpec((tm, GROUP_WIDTH), row)] * 4
        + [pl.BlockSpec((d, d), const),
           pl.BlockSpec((1, d), const),
           pl.BlockSpec((n_exp, d), const),
           pl.BlockSpec((n_exp, 1), const),
           pl.BlockSpec((tm, tm), const)],
        out_specs=(pl.BlockSpec((tm, d), row),
                   pl.BlockSpec((tm * ROW_CHUNKS, LANES), row),
                   pl.BlockSpec((TOP_K, tm), col),
                   pl.BlockSpec((TOP_K, tm), col),
                   pl.BlockSpec((TOP_K, tm), col),
                   pl.BlockSpec((n_exp, LANES), const)),
        out_shape=(jax.ShapeDtypeStruct((n, d), F32),
                   jax.ShapeDtypeStruct((n * ROW_CHUNKS, LANES), F32),
                   jax.ShapeDtypeStruct((TOP_K, n), I32),
                   jax.ShapeDtypeStruct((TOP_K, n), F32),
                   jax.ShapeDtypeStruct((TOP_K, n), I32),
                   jax.ShapeDtypeStruct((n_exp, LANES), F32)),
        scratch_shapes=[pltpu.VMEM((n_exp, LANES), F32)],
        compiler_params=_cparams(("arbitrary",)),
        name="outproj",
    )(x2, ya, yb, yc, ym, w_out, fg, rwt, rb, ustrict)


def _plan_kernel(cnt_ref, idx_ref, rank_ref, dest_ref, meta_ref, start_ref, *, n_exp, block_rows):
    def body(e, off):
        start_ref[e] = off
        return off + (cnt_ref[e] + block_rows - 1) // block_rows * block_rows

    total = lax.fori_loop(0, n_exp, body, jnp.int32(0))
    idx = idx_ref[...]
    dest = rank_ref[...]
    blk_start = lax.broadcasted_iota(I32, meta_ref.shape, 1) * block_rows
    blk_e = jnp.zeros(meta_ref.shape, I32)
    for e in range(n_exp):
        dest = dest + jnp.where(idx == e, start_ref[e], 0)
        if e > 0:
            blk_e = blk_e + (blk_start >= start_ref[e]).astype(I32)
    dest_ref[...] = dest
    sub = lax.broadcasted_iota(I32, meta_ref.shape, 0)
    meta_ref[...] = jnp.where(sub == 0, blk_e, total // block_rows)


def _plan(cnt, idx, rank, n_blocks):
    n_exp = cnt.shape[0]
    nb_pad = (n_blocks + LANES - 1) // LANES * LANES
    return pl.pallas_call(
        functools.partial(_plan_kernel, n_exp=n_exp, block_rows=TM_EXPERT),
        in_specs=[pl.BlockSpec(memory_space=pltpu.SMEM),
                  pl.BlockSpec(memory_space=pltpu.VMEM),
                  pl.BlockSpec(memory_space=pltpu.VMEM)],
        out_specs=(pl.BlockSpec(memory_space=pltpu.VMEM),
                   pl.BlockSpec(memory_space=pltpu.VMEM),
                   pl.BlockSpec(memory_space=pltpu.SMEM)),
        out_shape=(jax.ShapeDtypeStruct(idx.shape, I32),
                   jax.ShapeDtypeStruct((SUBLANES, nb_pad), I32),
                   jax.ShapeDtypeStruct((n_exp,), I32)),
        compiler_params=pltpu.CompilerParams(vmem_limit_bytes=VMEM_LIMIT),
        name="plan",
    )(cnt, idx, rank)


def _tile_rows(row):
    return pl.ds(pl.multiple_of(row * ROW_CHUNKS, ROW_CHUNKS), ROW_CHUNKS)


def _dispatch_kernel(dest_ref, cnt_ref, start_ref, hn_ref, xs_ref, zero_ref, sem, zsem,
                     *, n_exp, block_rows):
    tm = dest_ref.shape[1]

    def pad_copy(e, r):
        return pltpu.make_async_copy(zero_ref, xs_ref.at[_tile_rows(start_ref[e] + r)], zsem)

    def pad_bounds(e):
        cnt = cnt_ref[e]
        return cnt, (cnt + block_rows - 1) // block_rows * block_rows

    @pl.when(pl.program_id(0) == 0)
    def _():
        zero_ref[...] = jnp.zeros_like(zero_ref)

        def pad_start(e, c):
            lo, hi = pad_bounds(e)
            return lax.fori_loop(lo, hi, lambda r, c: (pad_copy(e, r).start(), c)[1], c)

        def pad_wait(e, c):
            lo, hi = pad_bounds(e)
            return lax.fori_loop(lo, hi, lambda r, c: (pad_copy(e, r).wait(), c)[1], c)

        lax.fori_loop(0, n_exp, pad_start, 0)
        lax.fori_loop(0, n_exp, pad_wait, 0)

        last = n_exp - 1
        used = start_ref[last] + pad_bounds(last)[1]
        cap = xs_ref.shape[0] // ROW_CHUNKS

        def tail_copy(r):
            return pltpu.make_async_copy(zero_ref, xs_ref.at[_tile_rows(r)], zsem)

        lax.fori_loop(used, cap, lambda r, c: (tail_copy(r).start(), c)[1], 0)
        lax.fori_loop(used, cap, lambda r, c: (tail_copy(r).wait(), c)[1], 0)

    def copy(k, t):
        return pltpu.make_async_copy(hn_ref.at[_tile_rows(t)],
                                     xs_ref.at[_tile_rows(dest_ref[k, t])], sem)

    def start(t, c):
        for k in range(TOP_K):
            copy(k, t).start()
        return c

    def wait(t, c):
        for k in range(TOP_K):
            copy(k, t).wait()
        return c

    lax.fori_loop(0, tm, start, 0)
    lax.fori_loop(0, tm, wait, 0)


def _dispatch(dest, cnt, starts, hn, cap):
    n = hn.shape[0] // ROW_CHUNKS
    tm = min(T_MOVE, n)
    return pl.pallas_call(
        functools.partial(_dispatch_kernel, n_exp=cnt.shape[0], block_rows=TM_EXPERT),
        grid=(n // tm,),
        in_specs=[pl.BlockSpec((TOP_K, tm), lambda i: (0, i), memory_space=pltpu.SMEM),
                  pl.BlockSpec(memory_space=pltpu.SMEM),
                  pl.BlockSpec(memory_space=pltpu.SMEM),
                  pl.BlockSpec((tm * ROW_CHUNKS, LANES), lambda i: (i, 0))],
        out_specs=pl.BlockSpec(memory_space=pl.ANY),
        out_shape=jax.ShapeDtypeStruct((cap * ROW_CHUNKS, LANES), hn.dtype),
        scratch_shapes=[pltpu.VMEM((ROW_CHUNKS, LANES), hn.dtype),
                        pltpu.SemaphoreType.DMA, pltpu.SemaphoreType.DMA],
        compiler_params=_cparams(("arbitrary",)),
        name="dispatch",
    )(dest, cnt, starts, hn)


def _experts_kernel(meta_ref, xs_ref, wg_ref, bg_ref, wu_ref, bu_ref, wd_ref, bd_ref, ys_ref,
                    wg_s, wu_s, wd_s):
    i = pl.program_id(0)
    tm = xs_ref.shape[0] // ROW_CHUNKS
    in_use = i < meta_ref[1, 0]
    new_expert = jnp.logical_or(i == 0, meta_ref[0, i] != meta_ref[0, jnp.maximum(i - 1, 0)])

    @pl.when(jnp.logical_not(in_use))
    def _():
        ys_ref[...] = jnp.zeros_like(ys_ref)

    @pl.when(jnp.logical_and(in_use, new_expert))
    def _():
        wg_s[...] = wg_ref[0, 0].astype(BF16)
        wu_s[...] = wu_ref[0, 0].astype(BF16)
        wd_s[...] = wd_ref[0, 0].astype(BF16)

    @pl.when(in_use)
    def _():
        x = _load_token_tiles(xs_ref, tm).astype(BF16)
        gl = jnp.dot(x, wg_s[...], preferred_element_type=F32) + bg_ref[0, 0]
        up = jnp.dot(x, wu_s[...], preferred_element_type=F32) + bu_ref[0, 0]
        gl = jnp.minimum(gl, SWIGLU_LIMIT)
        up = jnp.clip(up, -SWIGLU_LIMIT, SWIGLU_LIMIT)
        act = gl * (1.0 / (1.0 + jnp.exp(-SWIGLU_ALPHA * gl)))
        hmid = ((up + 1.0) * act).astype(BF16)
        y = jnp.dot(hmid, wd_s[...], preferred_element_type=F32) + bd_ref[0, 0]
        _store_token_tiles(ys_ref, y)


def _experts(meta, xs, layer, wg, bg, wu, bu, wd, bd):
    cap = xs.shape[0] // ROW_CHUNKS
    _, _, d, de = wg.shape
    tm = TM_EXPERT
    n_blocks = cap // tm

    def blk(i, meta):
        return jnp.minimum(i, meta[1, 0] - 1)

    rows = lambda i, meta: (blk(i, meta), 0)
    wmap = lambda i, meta: (layer, meta[0, blk(i, meta)], 0, 0)
    grid_spec = pltpu.PrefetchScalarGridSpec(
        num_scalar_prefetch=1,
        grid=(n_blocks,),
        in_specs=[pl.BlockSpec((tm * ROW_CHUNKS, LANES), rows),
                  pl.BlockSpec((1, 1, d, de), wmap),
                  pl.BlockSpec((1, 1, 1, de), wmap),
                  pl.BlockSpec((1, 1, d, de), wmap),
                  pl.BlockSpec((1, 1, 1, de), wmap),
                  pl.BlockSpec((1, 1, de, d), wmap),
                  pl.BlockSpec((1, 1, 1, d), wmap)],
        out_specs=pl.BlockSpec((tm * ROW_CHUNKS, LANES), lambda i, meta: (i, 0)),
        scratch_shapes=[pltpu.VMEM((d, de), BF16), pltpu.VMEM((d, de), BF16),
                        pltpu.VMEM((de, d), BF16)],
    )
    return pl.pallas_call(
        _experts_kernel,
        grid_spec=grid_spec,
        out_shape=jax.ShapeDtypeStruct((cap * ROW_CHUNKS, LANES), F32),
        compiler_params=_cparams(("arbitrary",)),
        name="experts",
    )(meta, xs, wg, bg, wu, bu, wd, bd)


def _combine_kernel(dest_ref, x1_ref, gate_ref, ys_ref, o_ref, buf_ref, sem):
    tm = x1_ref.shape[0]

    def copy(k, t):
        return pltpu.make_async_copy(ys_ref.at[_tile_rows(dest_ref[k, t])],
                                     buf_ref.at[k, _tile_rows(t)], sem)

    def start(t, c):
        for k in range(TOP_K):
            copy(k, t).start()
        return c

    def wait(t, c):
        for k in range(TOP_K):
            copy(k, t).wait()
        return c

    lax.fori_loop(0, tm, start, 0)
    gates = jnp.transpose(gate_ref[...])
    acc = x1_ref[...]
    lax.fori_loop(0, tm, wait, 0)
    for k in range(TOP_K):
        acc = acc + gates[:, k:k + 1] * _load_token_tiles(buf_ref.at[k], tm)
    o_ref[...] = acc


def _combine(dest, x1, gates_pad, ys):
    n, d = x1.shape
    tm = min(T_MOVE, n)
    return pl.pallas_call(
        _combine_kernel,
        grid=(n // tm,),
        in_specs=[pl.BlockSpec((TOP_K, tm), lambda i: (0, i), memory_space=pltpu.SMEM),
                  pl.BlockSpec((tm, d), lambda i: (i, 0)),
                  pl.BlockSpec((LANES, tm), lambda i: (0, i)),
                  pl.BlockSpec(memory_space=pl.ANY)],
        out_specs=pl.BlockSpec((tm, d), lambda i: (i, 0)),
        out_shape=jax.ShapeDtypeStruct((n, d), F32),
        scratch_shapes=[pltpu.VMEM((TOP_K, tm * ROW_CHUNKS, LANES), F32),
                        pltpu.SemaphoreType.DMA],
        compiler_params=_cparams(("arbitrary",)),
        name="combine",
    )(dest, x1, gates_pad, ys)


def _pad_lanes(v, width=LANES):
    v = v.astype(F32).reshape(1, -1)
    return jnp.pad(v, ((0, 0), (0, width - v.shape[1])))


def _layer(x2, mem, bsz, seq, p, layer, experts):
    n, d = x2.shape
    f32 = F32
    w_in = p['w_in']
    off_q = 2 * GROUP_WIDTH
    off_k, off_v, off_f = off_q + GROUP_WIDTH, off_q + 2 * GROUP_WIDTH, off_q + 3 * GROUP_WIDTH
    off_qm = off_f + N_HEADS
    w_f = w_in[:, off_f:off_qm]
    w_f_pad = jnp.pad(jnp.concatenate([w_f, w_f, w_f], axis=1), ((0, 0), (0, LANES - 3 * N_HEADS)))
    w_all = jnp.concatenate([w_in[:, :off_f], w_in[:, off_qm:], w_f_pad], axis=1).astype(BF16)
    fb = p['fox_forget_b'].astype(f32)
    fb_pad = _pad_lanes(jnp.concatenate([fb, fb, fb]))
    tm_in = min(TM_INPROJ, seq)
    tri = jnp.tril(jnp.ones((tm_in, tm_in), f32)).astype(BF16)

    ua, ub, qa, ka, va, qm = _inproj(
        x2, seq, p['mix_norm_g'].reshape(1, d).astype(f32), w_all,
        _pad_lanes(p['fox_q_g']), _pad_lanes(p['fox_k_g']), _pad_lanes(p['mem_q_g']), fb_pad, tri)

    bmat, coef, cmat = _s5_constants(p['ssm_lambda_re'], p['ssm_lambda_im'], p['ssm_log_dt'],
                                     p['ssm_b_re'], p['ssm_b_im'], p['ssm_c_re'], p['ssm_c_im'])
    ya = _s5(ua, bsz, seq, bmat, coef, cmat,
             p['ssm_d'].reshape(1, GROUP_WIDTH).astype(f32), p['ssm_glu_w'].astype(BF16),
             p['ssm_glu_b'].reshape(1, GROUP_WIDTH).astype(f32),
             p['ssm_out_g'].reshape(1, GROUP_WIDTH).astype(f32))

    pw = p['pool_w'].astype(f32)
    w_blk = jnp.zeros((GROUP_WIDTH, GROUP_WIDTH), f32)
    for gi in range(len(POOL_WINDOWS)):
        w_blk = w_blk.at[gi * HEAD_DIM:(gi + 1) * HEAD_DIM, gi * HEAD_DIM:(gi + 1) * HEAD_DIM].set(pw[gi])
    yb = _pool(ub, bsz, seq, w_blk.astype(BF16), p['pool_scale'].reshape(1, GROUP_WIDTH).astype(f32))

    yc = _fox(qa, ka, va, bsz, seq, p['fox_out_g'].reshape(1, GROUP_WIDTH).astype(f32))

    mk, mv = _memkv(mem, p['mem_norm_g'].reshape(1, d).astype(f32), p['mem_w_kv'].astype(BF16),
                    _pad_lanes(p['mem_k_g']))
    ym = _memattn(qm, mk, mv, bsz, seq, p['mem_out_g'].reshape(1, GROUP_WIDTH).astype(f32))

    n_exp = p['router_w'].shape[1]
    tm_out = min(TM_OUT, n)
    ustrict = jnp.triu(jnp.ones((tm_out, tm_out), f32), k=1).astype(BF16)
    x1, hn, idx, gates, rank, cnt = _outproj(
        x2, ya, yb, yc, ym, p['w_out'].astype(BF16), p['ffn_norm_g'].reshape(1, d).astype(f32),
        jnp.transpose(p['router_w']).astype(f32), p['router_b'].reshape(n_exp, 1).astype(f32), ustrict)

    cap = n * TOP_K + n_exp * TM_EXPERT
    cnt_i = cnt[:, 0].astype(I32)
    dest, meta, starts = _plan(cnt_i, idx, rank, cap // TM_EXPERT)
    xs = _dispatch(dest, cnt_i, starts, hn, cap)
    ys = _experts(meta, xs, layer, *experts)
    gates_pad = jnp.pad(gates, ((0, LANES - TOP_K), (0, 0)))
    return _combine(dest, x1, gates_pad, ys)


_PARAM_NAMES = ('mix_norm_g', 'w_in', 'ssm_lambda_re', 'ssm_lambda_im', 'ssm_log_dt',
                'ssm_b_re', 'ssm_b_im', 'ssm_c_re', 'ssm_c_im', 'ssm_d', 'ssm_glu_w', 'ssm_glu_b',
                'ssm_out_g', 'pool_w', 'pool_scale', 'fox_forget_b', 'fox_q_g', 'fox_k_g',
                'fox_out_g', 'mem_norm_g', 'mem_w_kv', 'mem_q_g', 'mem_k_g', 'mem_out_g', 'w_out',
                'ffn_norm_g', 'router_w', 'router_b', 'exp_w_gate', 'exp_b_gate', 'exp_w_up',
                'exp_b_up', 'exp_w_down', 'exp_b_down')


def kernel(x, mem, mix_norm_g, w_in, ssm_lambda_re, ssm_lambda_im, ssm_log_dt, ssm_b_re, ssm_b_im,
           ssm_c_re, ssm_c_im, ssm_d, ssm_glu_w, ssm_glu_b, ssm_out_g, pool_w, pool_scale,
           fox_forget_b, fox_q_g, fox_k_g, fox_out_g, mem_norm_g, mem_w_kv, mem_q_g, mem_k_g,
           mem_out_g, w_out, ffn_norm_g, router_w, router_b, exp_w_gate, exp_b_gate, exp_w_up,
           exp_b_up, exp_w_down, exp_b_down):
    stacked = dict(zip(_PARAM_NAMES, (
        mix_norm_g, w_in, ssm_lambda_re, ssm_lambda_im, ssm_log_dt, ssm_b_re, ssm_b_im, ssm_c_re,
        ssm_c_im, ssm_d, ssm_glu_w, ssm_glu_b, ssm_out_g, pool_w, pool_scale, fox_forget_b,
        fox_q_g, fox_k_g, fox_out_g, mem_norm_g, mem_w_kv, mem_q_g, mem_k_g, mem_out_g, w_out,
        ffn_norm_g, router_w, router_b, exp_w_gate, exp_b_gate, exp_w_up, exp_b_up, exp_w_down,
        exp_b_down)))
    bsz, seq, d = x.shape
    depth = w_in.shape[0]
    x2 = x.reshape(bsz * seq, d).astype(F32)
    mem = mem.astype(F32)
    expert_names = ('exp_w_gate', 'exp_b_gate', 'exp_w_up', 'exp_b_up', 'exp_w_down', 'exp_b_down')
    experts = tuple(stacked[k].astype(F32) if stacked[k].ndim == 4
                    else stacked[k].astype(F32)[:, :, None, :] for k in expert_names)
    for layer in range(depth):
        x2 = _layer(x2, mem, bsz, seq,
                    {k: v[layer] for k, v in stacked.items() if k not in expert_names},
                    layer, experts)
    return x2.reshape(bsz, seq, d).astype(x.dtype)
```

```python
import functools
import math

import jax
import jax.numpy as jnp
from jax import lax
from jax.experimental import pallas as pl
from jax.experimental.pallas import tpu as pltpu

F32 = jnp.float32
BF16 = jnp.bfloat16
I32 = jnp.int32

EPS = 1e-6
HEAD_DIM = 64
N_HEADS = 4
GROUP_WIDTH = 256
LANES = 128
SUBLANES = 8
ROW_CHUNKS = 8
SSM_GROUPS = 16
SSM_CH = 16
SSM_STATE = 64
SSM_LANES = SSM_GROUPS * SSM_STATE
POOL_WINDOWS = (2, 4, 8, 16)
POOL_HALO = 16
TOP_K = 4
SWIGLU_LIMIT = 7.0
SWIGLU_ALPHA = 1.702
VMEM_LIMIT = 56 * 1024 * 1024

TM_INPROJ = 512
T_S5 = 256
T_POOL = 512
T_ATT = 512
TM_OUT = 512
TM_EXPERT = 512
T_MOVE = 256

NEG_INF = float("-inf")
LOG2E = 1.4426950408889634


def _cparams(sem):
    return pltpu.CompilerParams(dimension_semantics=sem, vmem_limit_bytes=VMEM_LIMIT)


def _lane_iota(shape):
    return lax.broadcasted_iota(I32, shape, len(shape) - 1)


def _split3(x):
    hi = x.astype(BF16).astype(F32)
    r = x - hi
    mid = r.astype(BF16).astype(F32)
    lo = r - mid
    return hi, mid, lo


def _head_slab(slab, odd):
    lane = _lane_iota(slab.shape)
    if odd:
        slab = pltpu.roll(slab, HEAD_DIM, axis=1)
    return jnp.where(lane < HEAD_DIM, slab, 0.0)


def _head_norm(xh, gain):
    ss = jnp.sum(xh * xh, axis=-1, keepdims=True)
    return xh * jnp.broadcast_to(lax.rsqrt(ss * (1.0 / HEAD_DIM) + EPS), xh.shape) * gain


def _divide_by_denominator(acc):
    inv = 1.0 / acc[:, HEAD_DIM:HEAD_DIM + 1]
    return acc * jnp.broadcast_to(inv, acc.shape)


def _join_heads(o_even, o_odd):
    lane = _lane_iota(o_even.shape)
    return jnp.where(lane < HEAD_DIM, o_even, pltpu.roll(o_odd, HEAD_DIM, axis=1))


def _store_token_tiles(ref, val):
    t = val.shape[0]
    for s in range(ROW_CHUNKS):
        ref[pl.ds(s, t, stride=ROW_CHUNKS), :] = val[:, s * LANES:(s + 1) * LANES]


def _load_token_tiles(ref, t):
    return jnp.concatenate([ref[pl.ds(s, t, stride=ROW_CHUNKS), :] for s in range(ROW_CHUNKS)],
                           axis=1)


def _group_norm_pairs(pairs, gain_ref, out_ref):
    ss = jnp.sum(sum(p * p for p in pairs), axis=-1, keepdims=True)
    scale = jnp.broadcast_to(lax.rsqrt(ss * (1.0 / GROUP_WIDTH) + EPS), pairs[0].shape)
    for i, p in enumerate(pairs):
        sl = slice(i * LANES, (i + 1) * LANES)
        out_ref[:, sl] = (p * scale * gain_ref[:, sl]).astype(out_ref.dtype)


COL_A, COL_B, COL_Q, COL_K, COL_V, COL_QM, COL_F = 0, 256, 512, 768, 1024, 1280, 1536
IN_COLS_PADDED = COL_F + LANES


def _inproj_kernel(x_ref, g_ref, w_ref, qg_ref, kg_ref, mqg_ref, fb_ref, tri_ref, sel_ref, ones_ref,
                   ua_ref, ub_ref, qa_ref, ka_ref, va_ref, qm_ref, carry_ref, *, tiles_per_seq):
    i = pl.program_id(0)

    @pl.when(i % tiles_per_seq == 0)
    def _():
        carry_ref[...] = jnp.zeros_like(carry_ref)

    x = x_ref[...]
    h = x * lax.rsqrt(jnp.mean(x * x, axis=-1, keepdims=True) + EPS) * g_ref[...]
    proj = jnp.dot(h.astype(BF16), w_ref[...], preferred_element_type=F32)
    ua_ref[...] = proj[:, COL_A:COL_A + GROUP_WIDTH]
    ub_ref[...] = proj[:, COL_B:COL_B + GROUP_WIDTH]

    z = proj[:, COL_F:COL_F + LANES] + fb_ref[...]
    lane = _lane_iota(z.shape)
    logf = jnp.minimum(z, 0.0) - jnp.log(1.0 + jnp.exp(-jnp.abs(z)))
    hi, mid, lo = _split3(logf)
    packed = jnp.where(lane < 4, hi, jnp.where(lane < 8, mid, jnp.where(lane < 12, lo, 0.0)))
    cs = jnp.dot(tri_ref[...], packed.astype(BF16), preferred_element_type=F32)
    cum = cs + pltpu.roll(cs, LANES - 4, axis=1) + pltpu.roll(cs, LANES - 8, axis=1)
    cum = cum + carry_ref[...]
    carry_ref[...] = cum[cum.shape[0] - 1:, :]

    scale = HEAD_DIM ** -0.5 * LOG2E
    c_parts = jnp.concatenate(_split3(cum * LOG2E), axis=1).astype(BF16)
    bias = jnp.dot(c_parts, sel_ref[...], preferred_element_type=F32)

    def heads_normed(col, gain_ref):
        t = proj[:, col:col + GROUP_WIDTH]
        sq = t * t
        sq_hi = sq.astype(BF16)
        sq_lo = (sq - sq_hi.astype(F32)).astype(BF16)
        ss = jnp.dot(jnp.concatenate([sq_hi, sq_lo], axis=1), ones_ref[...],
                     preferred_element_type=F32)
        return t * lax.rsqrt(ss * (1.0 / HEAD_DIM) + EPS) * gain_ref[...]

    qn_all = heads_normed(COL_Q, qg_ref) * scale
    kn_all = heads_normed(COL_K, kg_ref)
    mn_all = heads_normed(COL_QM, mqg_ref) * scale
    q_bias = (lane >= 64) & (lane < 67)
    k_bias = (lane >= 67) & (lane < 70)
    for hd in range(N_HEADS):
        pair, odd = hd // 2, hd % 2
        pair_sl = slice(pair * LANES, (pair + 1) * LANES)
        b = bias[:, hd * LANES:(hd + 1) * LANES]
        qn = _head_slab(qn_all[:, pair_sl], odd)
        qa_ref[hd] = jnp.where(q_bias, b, jnp.where(k_bias, 1.0, qn)).astype(BF16)
        kn = _head_slab(kn_all[:, pair_sl], odd)
        ka_ref[hd] = jnp.where(k_bias, b, jnp.where(q_bias, 1.0, kn)).astype(BF16)
        vs = _head_slab(proj[:, COL_V + pair * LANES:COL_V + (pair + 1) * LANES], odd)
        va_ref[hd] = jnp.where(lane == 64, 1.0, vs).astype(BF16)
        qm_ref[hd] = _head_slab(mn_all[:, pair_sl], odd).astype(BF16)


def _inproj(x2, seq, mix_g, w_all, qg, kg, mqg, fb, tri, sel, ones):
    n, d = x2.shape
    tm = min(TM_INPROJ, seq)
    grid = (n // tm,)
    const = lambda i: (0, 0)
    row = lambda i: (i, 0)
    hrow = lambda i: (0, i, 0)
    out_shape = (
        jax.ShapeDtypeStruct((n, GROUP_WIDTH), F32),
        jax.ShapeDtypeStruct((n, GROUP_WIDTH), F32),
        jax.ShapeDtypeStruct((N_HEADS, n, LANES), BF16),
        jax.ShapeDtypeStruct((N_HEADS, n, LANES), BF16),
        jax.ShapeDtypeStruct((N_HEADS, n, LANES), BF16),
        jax.ShapeDtypeStruct((N_HEADS, n, LANES), BF16),
    )
    return pl.pallas_call(
        functools.partial(_inproj_kernel, tiles_per_seq=seq // tm),
        grid=grid,
        in_specs=[
            pl.BlockSpec((tm, d), row),
            pl.BlockSpec((1, d), const),
            pl.BlockSpec((d, IN_COLS_PADDED), const),
            pl.BlockSpec((1, GROUP_WIDTH), const),
            pl.BlockSpec((1, GROUP_WIDTH), const),
            pl.BlockSpec((1, GROUP_WIDTH), const),
            pl.BlockSpec((1, LANES), const),
            pl.BlockSpec((tm, tm), const),
            pl.BlockSpec((3 * LANES, N_HEADS * LANES), const),
            pl.BlockSpec((2 * GROUP_WIDTH, GROUP_WIDTH), const),
        ],
        out_specs=(
            pl.BlockSpec((tm, GROUP_WIDTH), row),
            pl.BlockSpec((tm, GROUP_WIDTH), row),
            pl.BlockSpec((N_HEADS, tm, LANES), hrow),
            pl.BlockSpec((N_HEADS, tm, LANES), hrow),
            pl.BlockSpec((N_HEADS, tm, LANES), hrow),
            pl.BlockSpec((N_HEADS, tm, LANES), hrow),
        ),
        out_shape=out_shape,
        scratch_shapes=[pltpu.VMEM((1, LANES), F32)],
        compiler_params=_cparams(("arbitrary",)),
        name="inproj",
    )(x2, mix_g, w_all, qg, kg, mqg, fb, tri, sel, ones)


def _s5_kernel(u_ref, bmat_ref, coef_ref, cmat_ref, d_ref, gluw_ref, glub_ref, og_ref,
               o_ref, x_scr, carry_ref):
    @pl.when(pl.program_id(1) == 0)
    def _():
        carry_ref[...] = jnp.zeros_like(carry_ref)

    u = u_ref[...]
    t = u.shape[0]
    x_scr[...] = jnp.dot(u.astype(BF16), bmat_ref[...], preferred_element_type=F32)
    n_lane_blocks = SSM_LANES // LANES

    def group(gi, carry):
        r0 = pl.multiple_of(gi * SUBLANES, SUBLANES)
        rows = pl.ds(r0, SUBLANES)
        new = []
        for lb in range(n_lane_blocks):
            re_sl = slice(lb * LANES, (lb + 1) * LANES)
            im_sl = slice(SSM_LANES + lb * LANES, SSM_LANES + (lb + 1) * LANES)
            xr = x_scr[rows, re_sl]
            xi = x_scr[rows, im_sl]
            for s, k in enumerate((1, 2, 4)):
                cr = coef_ref[2 * s, :, re_sl]
                ci = coef_ref[2 * s + 1, :, re_sl]
                sr = pltpu.roll(xr, k, axis=0)
                si = pltpu.roll(xi, k, axis=0)
                xr, xi = xr + cr * sr - ci * si, xi + cr * si + ci * sr
            pr = coef_ref[6, :, re_sl]
            pi_ = coef_ref[7, :, re_sl]
            cbr, cbi = carry[2 * lb], carry[2 * lb + 1]
            xr, xi = xr + pr * cbr - pi_ * cbi, xi + pr * cbi + pi_ * cbr
            x_scr[rows, re_sl] = xr
            x_scr[rows, im_sl] = xi
            new.append(jnp.broadcast_to(xr[SUBLANES - 1:, :], xr.shape))
            new.append(jnp.broadcast_to(xi[SUBLANES - 1:, :], xi.shape))
        return tuple(new)

    carry0 = []
    for lb in range(n_lane_blocks):
        carry0.append(carry_ref[:, lb * LANES:(lb + 1) * LANES])
        carry0.append(carry_ref[:, SSM_LANES + lb * LANES:SSM_LANES + (lb + 1) * LANES])
    carry = lax.fori_loop(0, t // SUBLANES, group, tuple(carry0))
    for lb in range(n_lane_blocks):
        carry_ref[:, lb * LANES:(lb + 1) * LANES] = carry[2 * lb]
        carry_ref[:, SSM_LANES + lb * LANES:SSM_LANES + (lb + 1) * LANES] = carry[2 * lb + 1]

    y = jnp.dot(x_scr[...].astype(BF16), cmat_ref[...], preferred_element_type=F32) + d_ref[...] * u
    z = jax.nn.gelu(y, approximate=True)
    gate = jnp.dot(z.astype(BF16), gluw_ref[...], preferred_element_type=F32) + glub_ref[...]
    out = z * (1.0 / (1.0 + jnp.exp(-gate)))
    _group_norm_pairs([out[:, :LANES], out[:, LANES:]], og_ref, o_ref)


def _s5(ua, bsz, seq, bmat, coef, cmat, dvec, gluw, glub, og):
    n = ua.shape[0]
    t = min(T_S5, seq)
    nt = seq // t
    row = lambda b, j: (b * nt + j, 0)
    c2 = lambda b, j: (0, 0)
    c3 = lambda b, j: (0, 0, 0)
    return pl.pallas_call(
        _s5_kernel,
        grid=(bsz, nt),
        in_specs=[
            pl.BlockSpec((t, GROUP_WIDTH), row),
            pl.BlockSpec((GROUP_WIDTH, 2 * SSM_LANES), c2),
            pl.BlockSpec((8, SUBLANES, SSM_LANES), c3),
            pl.BlockSpec((2 * SSM_LANES, GROUP_WIDTH), c2),
            pl.BlockSpec((1, GROUP_WIDTH), c2),
            pl.BlockSpec((GROUP_WIDTH, GROUP_WIDTH), c2),
            pl.BlockSpec((1, GROUP_WIDTH), c2),
            pl.BlockSpec((1, GROUP_WIDTH), c2),
        ],
        out_specs=pl.BlockSpec((t, GROUP_WIDTH), row),
        out_shape=jax.ShapeDtypeStruct((n, GROUP_WIDTH), BF16),
        scratch_shapes=[pltpu.VMEM((t, 2 * SSM_LANES), F32),
                        pltpu.VMEM((SUBLANES, 2 * SSM_LANES), F32)],
        compiler_params=_cparams(("arbitrary", "arbitrary")),
        name="s5",
    )(ua, bmat, coef, cmat, dvec, gluw, glub, og)


def _s5_constants(lam_re, lam_im, log_dt, b_re, b_im, c_re, c_im):
    lr = lam_re.astype(F32)
    li = lam_im.astype(F32)
    dt = jnp.exp(log_dt.astype(F32))[:, None]
    mag = jnp.exp(lr * dt)
    a_re = mag * jnp.cos(li * dt)
    a_im = mag * jnp.sin(li * dt)
    den = lr * lr + li * li
    n_re = a_re - 1.0
    n_im = a_im
    k_re = (n_re * lr + n_im * li) / den
    k_im = (n_im * lr - n_re * li) / den
    br = b_re.astype(F32)
    bi = b_im.astype(F32)
    bb_re = k_re[..., None] * br - k_im[..., None] * bi
    bb_im = k_re[..., None] * bi + k_im[..., None] * br
    eye = jnp.eye(SSM_GROUPS, dtype=F32)
    bm_re = jnp.einsum('gph,gk->ghkp', bb_re, eye).reshape(GROUP_WIDTH, SSM_LANES)
    bm_im = jnp.einsum('gph,gk->ghkp', bb_im, eye).reshape(GROUP_WIDTH, SSM_LANES)
    bmat = jnp.concatenate([bm_re, bm_im], axis=1).astype(BF16)
    cm_re = jnp.einsum('ghp,gk->gpkh', c_re.astype(F32), eye).reshape(SSM_LANES, GROUP_WIDTH)
    cm_im = jnp.einsum('ghp,gk->gpkh', c_im.astype(F32), eye).reshape(SSM_LANES, GROUP_WIDTH)
    cmat = jnp.concatenate([cm_re, -cm_im], axis=0).astype(BF16)

    ar = a_re.reshape(1, SSM_LANES)
    ai = a_im.reshape(1, SSM_LANES)

    def cmul(x, y):
        return x[0] * y[0] - x[1] * y[1], x[0] * y[1] + x[1] * y[0]

    pows = [(ar, ai)]
    for _ in range(SUBLANES - 1):
        pows.append(cmul(pows[-1], (ar, ai)))
    rows = jnp.arange(SUBLANES, dtype=I32)[:, None]
    planes = []
    for k in (1, 2, 4):
        m = (rows >= k).astype(F32)
        planes += [m * pows[k - 1][0], m * pows[k - 1][1]]
    planes += [jnp.concatenate([p[0] for p in pows], axis=0),
               jnp.concatenate([p[1] for p in pows], axis=0)]
    coef = jnp.stack(planes, axis=0)
    return bmat, coef, cmat


def _pool_kernel(v_ref, w_ref, g_ref, o_ref, ext_ref):
    j = pl.program_id(1)
    t = v_ref.shape[0]

    @pl.when(j == 0)
    def _():
        ext_ref[0:POOL_HALO, :] = jnp.zeros((POOL_HALO, GROUP_WIDTH), F32)

    v = v_ref[...]
    ext_ref[POOL_HALO:POOL_HALO + t, :] = v
    cur = ext_ref[...]
    width = 1
    wins = {}
    while width < POOL_WINDOWS[-1]:
        cur = cur + pltpu.roll(cur, width, axis=0)
        width *= 2
        wins[width] = cur[POOL_HALO:, :]
    lane = _lane_iota(v.shape)
    pos = (j * t + lax.broadcasted_iota(I32, v.shape, 0) + 1).astype(F32)
    mean = None
    for gi, w in enumerate(POOL_WINDOWS):
        m = wins[w] / jnp.minimum(pos, float(w))
        mean = m if mean is None else jnp.where(lane >= gi * HEAD_DIM, m, mean)
    mixed = jnp.dot((mean - v).astype(BF16), w_ref[...], preferred_element_type=F32)
    _group_norm_pairs([mixed[:, :LANES], mixed[:, LANES:]], g_ref, o_ref)
    ext_ref[0:POOL_HALO, :] = v[t - POOL_HALO:, :]


def _pool(ub, bsz, seq, w_blk, g):
    n = ub.shape[0]
    t = min(T_POOL, seq)
    nt = seq // t
    row = lambda b, j: (b * nt + j, 0)
    c2 = lambda b, j: (0, 0)
    return pl.pallas_call(
        _pool_kernel,
        grid=(bsz, nt),
        in_specs=[pl.BlockSpec((t, GROUP_WIDTH), row),
                  pl.BlockSpec((GROUP_WIDTH, GROUP_WIDTH), c2),
                  pl.BlockSpec((1, GROUP_WIDTH), c2)],
        out_specs=pl.BlockSpec((t, GROUP_WIDTH), row),
        out_shape=jax.ShapeDtypeStruct((n, GROUP_WIDTH), BF16),
        scratch_shapes=[pltpu.VMEM((t + POOL_HALO, GROUP_WIDTH), F32)],
        compiler_params=_cparams(("arbitrary", "arbitrary")),
        name="pool",
    )(ub, w_blk, g)


def _fox_kernel(qt_ref, kt_ref, qa_ref, ka_ref, va_ref, g_ref, o_ref, m_ref, acc_ref):
    p_id = pl.program_id(1)
    qi = qt_ref[p_id]
    ki = kt_ref[p_id]
    tq = qa_ref.shape[1]
    tk = ka_ref.shape[1]

    @pl.when(ki == 0)
    def _():
        m_ref[...] = jnp.full_like(m_ref, NEG_INF)
        acc_ref[...] = jnp.zeros_like(acc_ref)

    def step(on_diagonal):
        if on_diagonal:
            causal = (lax.broadcasted_iota(I32, (tq, tk), 0)
                      >= lax.broadcasted_iota(I32, (tq, tk), 1))
        def scores(hd):
            return lax.dot_general(qa_ref[hd], ka_ref[hd], (((1,), (1,)), ((), ())),
                                   preferred_element_type=F32)

        s_next = scores(0)
        for hd in range(N_HEADS):
            s = s_next
            if hd + 1 < N_HEADS:
                s_next = scores(hd + 1)
            if on_diagonal:
                s = jnp.where(causal, s, NEG_INF)
            m_prev = m_ref[hd]
            m_new = jnp.maximum(m_prev, jnp.broadcast_to(jnp.max(s, axis=-1, keepdims=True),
                                                         m_prev.shape))
            alpha = jnp.exp2(m_prev - m_new)
            p = jnp.exp2(s - jnp.concatenate([m_new] * (tk // LANES), axis=1))
            acc_ref[hd] = alpha * acc_ref[hd] + jnp.dot(p.astype(BF16), va_ref[hd],
                                                        preferred_element_type=F32)
            m_ref[hd] = m_new

    @pl.when(ki < qi)
    def _():
        step(False)

    @pl.when(ki == qi)
    def _():
        step(True)
        heads = []
        for hd in range(N_HEADS):
            acc = acc_ref[hd]
            heads.append(_divide_by_denominator(acc))
        _group_norm_pairs([_join_heads(heads[0], heads[1]), _join_heads(heads[2], heads[3])],
                          g_ref, o_ref)


def _fox(qa, ka, va, bsz, seq, g):
    n = qa.shape[1]
    t = min(T_ATT, seq)
    nq = seq // t
    pairs = [(q, k) for q in range(nq) for k in range(q + 1)]
    qt = jnp.asarray([p[0] for p in pairs], I32)
    kt = jnp.asarray([p[1] for p in pairs], I32)
    qmap = lambda b, p, qt, kt: (0, b * nq + qt[p], 0)
    kmap = lambda b, p, qt, kt: (0, b * nq + kt[p], 0)
    grid_spec = pltpu.PrefetchScalarGridSpec(
        num_scalar_prefetch=2,
        grid=(bsz, len(pairs)),
        in_specs=[pl.BlockSpec((N_HEADS, t, LANES), qmap),
                  pl.BlockSpec((N_HEADS, t, LANES), kmap),
                  pl.BlockSpec((N_HEADS, t, LANES), kmap),
                  pl.BlockSpec((1, GROUP_WIDTH), lambda b, p, qt, kt: (0, 0))],
        out_specs=pl.BlockSpec((t, GROUP_WIDTH), lambda b, p, qt, kt: (b * nq + qt[p], 0)),
        scratch_shapes=[pltpu.VMEM((N_HEADS, t, LANES), F32),
                        pltpu.VMEM((N_HEADS, t, LANES), F32)],
    )
    return pl.pallas_call(
        _fox_kernel,
        grid_spec=grid_spec,
        out_shape=jax.ShapeDtypeStruct((n, GROUP_WIDTH), BF16),
        compiler_params=_cparams(("arbitrary", "arbitrary")),
        name="fox",
    )(qt, kt, qa, ka, va, g)


def _memkv_kernel(mem_ref, g_ref, w_ref, kg_ref, mk_ref, mv_ref):
    x = mem_ref[0]
    h = x * lax.rsqrt(jnp.mean(x * x, axis=-1, keepdims=True) + EPS) * g_ref[...]
    kv = jnp.dot(h.astype(BF16), w_ref[...], preferred_element_type=F32)
    lane = _lane_iota((x.shape[0], LANES))
    for hd in range(N_HEADS):
        pair, odd = hd // 2, hd % 2
        ks = _head_slab(kv[:, pair * LANES:(pair + 1) * LANES], odd)
        mk_ref[0, hd] = _head_norm(ks, kg_ref[...]).astype(BF16)
        vs = _head_slab(kv[:, GROUP_WIDTH + pair * LANES:GROUP_WIDTH + (pair + 1) * LANES], odd)
        mv_ref[0, hd] = jnp.where(lane == HEAD_DIM, 1.0, vs).astype(BF16)


def _memkv(mem, g, w_kv, kg):
    bsz, m, d = mem.shape
    c2 = lambda b: (0, 0)
    out = jax.ShapeDtypeStruct((bsz, N_HEADS, m, LANES), BF16)
    return pl.pallas_call(
        _memkv_kernel,
        grid=(bsz,),
        in_specs=[pl.BlockSpec((1, m, d), lambda b: (b, 0, 0)),
                  pl.BlockSpec((1, d), c2),
                  pl.BlockSpec((d, 2 * GROUP_WIDTH), c2),
                  pl.BlockSpec((1, LANES), c2)],
        out_specs=(pl.BlockSpec((1, N_HEADS, m, LANES), lambda b: (b, 0, 0, 0)),
                   pl.BlockSpec((1, N_HEADS, m, LANES), lambda b: (b, 0, 0, 0))),
        out_shape=(out, out),
        compiler_params=_cparams(("arbitrary",)),
        name="memkv",
    )(mem, g, w_kv, kg)


def _memattn_kernel(qm_ref, mk_ref, mv_ref, g_ref, o_ref):
    heads = []
    for hd in range(N_HEADS):
        s = lax.dot_general(qm_ref[hd], mk_ref[0, hd], (((1,), (1,)), ((), ())),
                            preferred_element_type=F32)
        m = jnp.broadcast_to(jnp.max(s, axis=-1, keepdims=True), (s.shape[0], LANES))
        p = jnp.exp2(s - jnp.concatenate([m] * (s.shape[1] // LANES), axis=1))
        acc = jnp.dot(p.astype(BF16), mv_ref[0, hd], preferred_element_type=F32)
        heads.append(_divide_by_denominator(acc))
    _group_norm_pairs([_join_heads(heads[0], heads[1]), _join_heads(heads[2], heads[3])],
                      g_ref, o_ref)


def _memattn(qm, mk, mv, bsz, seq, g):
    n = qm.shape[1]
    m = mk.shape[2]
    t = min(T_ATT, seq)
    nt = seq // t
    return pl.pallas_call(
        _memattn_kernel,
        grid=(bsz, nt),
        in_specs=[pl.BlockSpec((N_HEADS, t, LANES), lambda b, j: (0, b * nt + j, 0)),
                  pl.BlockSpec((1, N_HEADS, m, LANES), lambda b, j: (b, 0, 0, 0)),
                  pl.BlockSpec((1, N_HEADS, m, LANES), lambda b, j: (b, 0, 0, 0)),
                  pl.BlockSpec((1, GROUP_WIDTH), lambda b, j: (0, 0))],
        out_specs=pl.BlockSpec((t, GROUP_WIDTH), lambda b, j: (b * nt + j, 0)),
        out_shape=jax.ShapeDtypeStruct((n, GROUP_WIDTH), BF16),
        compiler_params=_cparams(("arbitrary", "arbitrary")),
        name="memattn",
    )(qm, mk, mv, g)


def _outproj_kernel(x_ref, ya_ref, yb_ref, yc_ref, ym_ref, w_ref, fg_ref, rw_ref, rb_ref, us_ref,
                    x1_ref, hn_ref, idx_ref, gate_ref, rank_ref, cnt_ref, carry_ref):
    @pl.when(pl.program_id(0) == 0)
    def _():
        carry_ref[...] = jnp.zeros_like(carry_ref)

    merged = jnp.concatenate([ya_ref[...], yb_ref[...], yc_ref[...], ym_ref[...]], axis=1)
    acc = x_ref[...] + jnp.dot(merged, w_ref[...], preferred_element_type=F32)
    x1_ref[...] = acc
    hn = acc * lax.rsqrt(jnp.mean(acc * acc, axis=-1, keepdims=True) + EPS) * fg_ref[...]
    _store_token_tiles(hn_ref, hn)
    n_exp = rb_ref.shape[0]
    tm = hn.shape[0]
    hn_hi = hn.astype(BF16)
    hn_lo = (hn - hn_hi.astype(F32)).astype(BF16)
    parts = jnp.dot(jnp.concatenate([hn_hi, hn_lo], axis=0), rw_ref[...],
                    preferred_element_type=F32)
    top = jnp.transpose(parts[:tm])
    bot = jnp.transpose(parts[tm:])
    logits = (top[:n_exp] + top[n_exp:2 * n_exp] + bot[:n_exp] + bot[n_exp:2 * n_exp]
              + rb_ref[...])
    e_iota = lax.broadcasted_iota(I32, (n_exp, tm), 0).astype(F32)
    work = logits
    vals, onehots = [], []
    for k in range(TOP_K):
        m = jnp.max(work, axis=0, keepdims=True)
        sel = jnp.min(jnp.where(work == m, e_iota, float(n_exp)), axis=0, keepdims=True)
        hot = e_iota == sel
        idx_ref[k:k + 1, :] = sel.astype(I32)
        vals.append(m)
        onehots.append(hot.astype(F32))
        work = jnp.where(hot, NEG_INF, work)
    exps = [jnp.exp(v - vals[0]) for v in vals]
    denom = exps[0] + exps[1] + exps[2] + exps[3]
    for k in range(TOP_K):
        gate_ref[k:k + 1, :] = exps[k] / denom
    stacked = jnp.concatenate(onehots, axis=0).astype(BF16)
    prefix = jnp.dot(stacked, us_ref[...], preferred_element_type=F32)
    base = carry_ref[...]
    for k in range(TOP_K):
        hot = onehots[k]
        pk = prefix[k * n_exp:(k + 1) * n_exp, :]
        rank = jnp.sum(hot * (pk + base[:, 0:1]), axis=0, keepdims=True)
        rank_ref[k:k + 1, :] = rank.astype(I32)
        base = base + jnp.sum(hot, axis=1, keepdims=True)
    carry_ref[...] = base
    cnt_ref[...] = base


def _outproj(x2, ya, yb, yc, ym, w_out, fg, rw, rb, ustrict):
    n, d = x2.shape
    n_exp = rb.shape[0]
    tm = min(TM_OUT, n)
    row = lambda i: (i, 0)
    col = lambda i: (0, i)
    const = lambda i: (0, 0)
    return pl.pallas_call(
        _outproj_kernel,
        grid=(n // tm,),
        in_specs=[pl.BlockSpec((tm, d), row)]
        + [pl.BlockSpec((tm, GROUP_WIDTH), row)] * 4
        + [pl.BlockSpec((d, d), const),
           pl.BlockSpec((1, d), const),
           pl.BlockSpec((d, LANES), const),
           pl.BlockSpec((n_exp, 1), const),
           pl.BlockSpec((tm, tm), const)],
        out_specs=(pl.BlockSpec((tm, d), row),
                   pl.BlockSpec((tm * ROW_CHUNKS, LANES), row),
                   pl.BlockSpec((TOP_K, tm), col),
                   pl.BlockSpec((TOP_K, tm), col),
                   pl.BlockSpec((TOP_K, tm), col),
                   pl.BlockSpec((n_exp, LANES), const)),
        out_shape=(jax.ShapeDtypeStruct((n, d), F32),
                   jax.ShapeDtypeStruct((n * ROW_CHUNKS, LANES), F32),
                   jax.ShapeDtypeStruct((TOP_K, n), I32),
                   jax.ShapeDtypeStruct((TOP_K, n), F32),
                   jax.ShapeDtypeStruct((TOP_K, n), I32),
                   jax.ShapeDtypeStruct((n_exp, LANES), F32)),
        scratch_shapes=[pltpu.VMEM((n_exp, LANES), F32)],
        compiler_params=_cparams(("arbitrary",)),
        name="outproj",
    )(x2, ya, yb, yc, ym, w_out, fg, rw, rb, ustrict)


def _plan_kernel(cnt_ref, idx_ref, rank_ref, dest_ref, meta_ref, start_ref, *, n_exp, block_rows):
    def body(e, off):
        start_ref[e] = off
        return off + (cnt_ref[e] + block_rows - 1) // block_rows * block_rows

    total = lax.fori_loop(0, n_exp, body, jnp.int32(0))
    idx = idx_ref[...]
    dest = rank_ref[...]
    blk_start = lax.broadcasted_iota(I32, meta_ref.shape, 1) * block_rows
    blk_e = jnp.zeros(meta_ref.shape, I32)
    for e in range(n_exp):
        dest = dest + jnp.where(idx == e, start_ref[e], 0)
        if e > 0:
            blk_e = blk_e + (blk_start >= start_ref[e]).astype(I32)
    dest_ref[...] = dest
    sub = lax.broadcasted_iota(I32, meta_ref.shape, 0)
    meta_ref[...] = jnp.where(sub == 0, blk_e, total // block_rows)


def _plan(cnt, idx, rank, n_blocks):
    n_exp = cnt.shape[0]
    nb_pad = (n_blocks + LANES - 1) // LANES * LANES
    return pl.pallas_call(
        functools.partial(_plan_kernel, n_exp=n_exp, block_rows=TM_EXPERT),
        in_specs=[pl.BlockSpec(memory_space=pltpu.SMEM),
                  pl.BlockSpec(memory_space=pltpu.VMEM),
                  pl.BlockSpec(memory_space=pltpu.VMEM)],
        out_specs=(pl.BlockSpec(memory_space=pltpu.VMEM),
                   pl.BlockSpec(memory_space=pltpu.VMEM),
                   pl.BlockSpec(memory_space=pltpu.SMEM)),
        out_shape=(jax.ShapeDtypeStruct(idx.shape, I32),
                   jax.ShapeDtypeStruct((SUBLANES, nb_pad), I32),
                   jax.ShapeDtypeStruct((n_exp,), I32)),
        compiler_params=pltpu.CompilerParams(vmem_limit_bytes=VMEM_LIMIT),
        name="plan",
    )(cnt, idx, rank)


def _tile_rows(row):
    return pl.ds(pl.multiple_of(row * ROW_CHUNKS, ROW_CHUNKS), ROW_CHUNKS)


def _dispatch_kernel(dest_ref, cnt_ref, start_ref, hn_ref, xs_ref, zero_ref, sem, zsem,
                     *, n_exp, block_rows):
    tm = dest_ref.shape[1]

    def pad_copy(e, r):
        return pltpu.make_async_copy(zero_ref.at[pl.ds(0, ROW_CHUNKS)],
                                     xs_ref.at[_tile_rows(start_ref[e] + r)], zsem)

    def pad_bounds(e):
        cnt = cnt_ref[e]
        return cnt, (cnt + block_rows - 1) // block_rows * block_rows

    @pl.when(pl.program_id(0) == 0)
    def _():
        zero_ref[...] = jnp.zeros_like(zero_ref)

        def pad_start(e, c):
            lo, hi = pad_bounds(e)
            return lax.fori_loop(lo, hi, lambda r, c: (pad_copy(e, r).start(), c)[1], c)

        def pad_wait(e, c):
            lo, hi = pad_bounds(e)
            return lax.fori_loop(lo, hi, lambda r, c: (pad_copy(e, r).wait(), c)[1], c)

        lax.fori_loop(0, n_exp, pad_start, 0)
        lax.fori_loop(0, n_exp, pad_wait, 0)

        last = n_exp - 1
        used = start_ref[last] + pad_bounds(last)[1]
        n_blocks = xs_ref.shape[0] // (ROW_CHUNKS * block_rows)

        def tail_copy(b):
            rows = pl.ds(pl.multiple_of(b * (block_rows * ROW_CHUNKS), block_rows * ROW_CHUNKS),
                         block_rows * ROW_CHUNKS)
            return pltpu.make_async_copy(zero_ref, xs_ref.at[rows], zsem)

        lax.fori_loop(used // block_rows, n_blocks, lambda b, c: (tail_copy(b).start(), c)[1], 0)
        lax.fori_loop(used // block_rows, n_blocks, lambda b, c: (tail_copy(b).wait(), c)[1], 0)

    def copy(k, t):
        return pltpu.make_async_copy(hn_ref.at[_tile_rows(t)],
                                     xs_ref.at[_tile_rows(dest_ref[k, t])], sem)

    def start(t, c):
        for k in range(TOP_K):
            copy(k, t).start(priority=k % 2)
        return c

    def wait(t, c):
        for k in range(TOP_K):
            copy(k, t).wait()
        return c

    lax.fori_loop(0, tm, start, 0)
    lax.fori_loop(0, tm, wait, 0)


def _dispatch(dest, cnt, starts, hn, cap):
    n = hn.shape[0] // ROW_CHUNKS
    tm = min(T_MOVE, n)
    return pl.pallas_call(
        functools.partial(_dispatch_kernel, n_exp=cnt.shape[0], block_rows=TM_EXPERT),
        grid=(n // tm,),
        in_specs=[pl.BlockSpec((TOP_K, tm), lambda i: (0, i), memory_space=pltpu.SMEM),
                  pl.BlockSpec(memory_space=pltpu.SMEM),
                  pl.BlockSpec(memory_space=pltpu.SMEM),
                  pl.BlockSpec((tm * ROW_CHUNKS, LANES), lambda i: (i, 0))],
        out_specs=pl.BlockSpec(memory_space=pl.ANY),
        out_shape=jax.ShapeDtypeStruct((cap * ROW_CHUNKS, LANES), hn.dtype),
        scratch_shapes=[pltpu.VMEM((TM_EXPERT * ROW_CHUNKS, LANES), hn.dtype),
                        pltpu.SemaphoreType.DMA, pltpu.SemaphoreType.DMA],
        compiler_params=_cparams(("arbitrary",)),
        name="dispatch",
    )(dest, cnt, starts, hn)


def _experts_kernel(meta_ref, xs_ref, wg_ref, bg_ref, wu_ref, bu_ref, wd_ref, bd_ref, ys_ref,
                    wg_s, wu_s, wd_s):
    i = pl.program_id(0)
    tm = xs_ref.shape[0] // ROW_CHUNKS
    in_use = i < meta_ref[1, 0]
    new_expert = jnp.logical_or(i == 0, meta_ref[0, i] != meta_ref[0, jnp.maximum(i - 1, 0)])

    @pl.when(jnp.logical_not(in_use))
    def _():
        ys_ref[...] = jnp.zeros_like(ys_ref)

    @pl.when(jnp.logical_and(in_use, new_expert))
    def _():
        wg_s[...] = wg_ref[0, 0].astype(BF16)
        wu_s[...] = wu_ref[0, 0].astype(BF16)
        wd_s[...] = wd_ref[0, 0].astype(BF16)

    @pl.when(in_use)
    def _():
        x = _load_token_tiles(xs_ref, tm).astype(BF16)
        gl = jnp.dot(x, wg_s[...], preferred_element_type=F32) + bg_ref[0, 0]
        up = jnp.dot(x, wu_s[...], preferred_element_type=F32) + bu_ref[0, 0]
        gl = jnp.minimum(gl, SWIGLU_LIMIT)
        up = jnp.clip(up, -SWIGLU_LIMIT, SWIGLU_LIMIT)
        act = gl * (1.0 / (1.0 + jnp.exp(-SWIGLU_ALPHA * gl)))
        hmid = ((up + 1.0) * act).astype(BF16)
        y = jnp.dot(hmid, wd_s[...], preferred_element_type=F32) + bd_ref[0, 0]
        _store_token_tiles(ys_ref, y)


def _experts(meta, xs, layer, wg, bg, wu, bu, wd, bd):
    cap = xs.shape[0] // ROW_CHUNKS
    _, _, d, de = wg.shape
    tm = TM_EXPERT
    n_blocks = cap // tm

    def blk(i, meta):
        return jnp.minimum(i, meta[1, 0] - 1)

    rows = lambda i, meta: (blk(i, meta), 0)
    wmap = lambda i, meta: (layer, meta[0, blk(i, meta)], 0, 0)
    grid_spec = pltpu.PrefetchScalarGridSpec(
        num_scalar_prefetch=1,
        grid=(n_blocks,),
        in_specs=[pl.BlockSpec((tm * ROW_CHUNKS, LANES), rows),
                  pl.BlockSpec((1, 1, d, de), wmap),
                  pl.BlockSpec((1, 1, 1, de), wmap),
                  pl.BlockSpec((1, 1, d, de), wmap),
                  pl.BlockSpec((1, 1, 1, de), wmap),
                  pl.BlockSpec((1, 1, de, d), wmap),
                  pl.BlockSpec((1, 1, 1, d), wmap)],
        out_specs=pl.BlockSpec((tm * ROW_CHUNKS, LANES), lambda i, meta: (i, 0)),
        scratch_shapes=[pltpu.VMEM((d, de), BF16), pltpu.VMEM((d, de), BF16),
                        pltpu.VMEM((de, d), BF16)],
    )
    return pl.pallas_call(
        _experts_kernel,
        grid_spec=grid_spec,
        out_shape=jax.ShapeDtypeStruct((cap * ROW_CHUNKS, LANES), F32),
        compiler_params=_cparams(("arbitrary",)),
        name="experts",
    )(meta, xs, wg, bg, wu, bu, wd, bd)


def _combine_kernel(dest_ref, x1_ref, gate_ref, ys_ref, o_ref, buf_ref, sem):
    tm = x1_ref.shape[0]

    def copy(k, t):
        return pltpu.make_async_copy(ys_ref.at[_tile_rows(dest_ref[k, t])],
                                     buf_ref.at[k, _tile_rows(t)], sem)

    def start(t, c):
        for k in range(TOP_K):
            copy(k, t).start(priority=k % 2)
        return c

    def wait(t, c):
        for k in range(TOP_K):
            copy(k, t).wait()
        return c

    lax.fori_loop(0, tm, start, 0)
    gates = jnp.transpose(gate_ref[...])
    acc = x1_ref[...]
    lax.fori_loop(0, tm, wait, 0)
    for k in range(TOP_K):
        acc = acc + gates[:, k:k + 1] * _load_token_tiles(buf_ref.at[k], tm)
    o_ref[...] = acc


def _combine(dest, x1, gates_pad, ys):
    n, d = x1.shape
    tm = min(T_MOVE, n)
    return pl.pallas_call(
        _combine_kernel,
        grid=(n // tm,),
        in_specs=[pl.BlockSpec((TOP_K, tm), lambda i: (0, i), memory_space=pltpu.SMEM),
                  pl.BlockSpec((tm, d), lambda i: (i, 0)),
                  pl.BlockSpec((LANES, tm), lambda i: (0, i)),
                  pl.BlockSpec(memory_space=pl.ANY)],
        out_specs=pl.BlockSpec((tm, d), lambda i: (i, 0)),
        out_shape=jax.ShapeDtypeStruct((n, d), F32),
        scratch_shapes=[pltpu.VMEM((TOP_K, tm * ROW_CHUNKS, LANES), F32),
                        pltpu.SemaphoreType.DMA],
        compiler_params=_cparams(("arbitrary",)),
        name="combine",
    )(dest, x1, gates_pad, ys)


def _pad_lanes(v, width=LANES):
    v = v.astype(F32).reshape(1, -1)
    return jnp.pad(v, ((0, 0), (0, width - v.shape[1])))


def _layer(x2, mem, bsz, seq, p, layer, experts):
    n, d = x2.shape
    f32 = F32
    w_in = p['w_in']
    off_q = 2 * GROUP_WIDTH
    off_k, off_v, off_f = off_q + GROUP_WIDTH, off_q + 2 * GROUP_WIDTH, off_q + 3 * GROUP_WIDTH
    off_qm = off_f + N_HEADS
    w_f = w_in[:, off_f:off_qm]
    w_f_pad = jnp.pad(jnp.concatenate([w_f, w_f, w_f], axis=1), ((0, 0), (0, LANES - 3 * N_HEADS)))
    w_all = jnp.concatenate([w_in[:, :off_f], w_in[:, off_qm:], w_f_pad], axis=1).astype(BF16)
    fb = p['fox_forget_b'].astype(f32)
    fb_pad = _pad_lanes(jnp.concatenate([fb, fb, fb]))
    tm_in = min(TM_INPROJ, seq)
    tri = jnp.tril(jnp.ones((tm_in, tm_in), f32)).astype(BF16)

    sel = jnp.zeros((3, LANES, N_HEADS, LANES), f32)
    for part in range(3):
        for hd in range(N_HEADS):
            sel = sel.at[part, hd, hd, HEAD_DIM + part].set(1.0)
            sel = sel.at[part, hd, hd, HEAD_DIM + 3 + part].set(-1.0)
    sel = sel.reshape(3 * LANES, N_HEADS * LANES).astype(BF16)
    head_of = jnp.arange(GROUP_WIDTH) // HEAD_DIM
    ones_blk = (head_of[:, None] == head_of[None, :]).astype(BF16)
    ones2 = jnp.concatenate([ones_blk, ones_blk], axis=0)

    def head_gain(g):
        return jnp.tile(g.astype(f32).reshape(1, HEAD_DIM), (1, N_HEADS))

    ua, ub, qa, ka, va, qm = _inproj(
        x2, seq, p['mix_norm_g'].reshape(1, d).astype(f32), w_all,
        head_gain(p['fox_q_g']), head_gain(p['fox_k_g']), head_gain(p['mem_q_g']), fb_pad, tri,
        sel, ones2)

    bmat, coef, cmat = _s5_constants(p['ssm_lambda_re'], p['ssm_lambda_im'], p['ssm_log_dt'],
                                     p['ssm_b_re'], p['ssm_b_im'], p['ssm_c_re'], p['ssm_c_im'])
    ya = _s5(ua, bsz, seq, bmat, coef, cmat,
             p['ssm_d'].reshape(1, GROUP_WIDTH).astype(f32), p['ssm_glu_w'].astype(BF16),
             p['ssm_glu_b'].reshape(1, GROUP_WIDTH).astype(f32),
             p['ssm_out_g'].reshape(1, GROUP_WIDTH).astype(f32))

    pw = p['pool_w'].astype(f32)
    w_blk = jnp.zeros((GROUP_WIDTH, GROUP_WIDTH), f32)
    for gi in range(len(POOL_WINDOWS)):
        w_blk = w_blk.at[gi * HEAD_DIM:(gi + 1) * HEAD_DIM, gi * HEAD_DIM:(gi + 1) * HEAD_DIM].set(pw[gi])
    yb = _pool(ub, bsz, seq, w_blk.astype(BF16), p['pool_scale'].reshape(1, GROUP_WIDTH).astype(f32))

    yc = _fox(qa, ka, va, bsz, seq, p['fox_out_g'].reshape(1, GROUP_WIDTH).astype(f32))

    mk, mv = _memkv(mem, p['mem_norm_g'].reshape(1, d).astype(f32), p['mem_w_kv'].astype(BF16),
                    _pad_lanes(p['mem_k_g']))
    ym = _memattn(qm, mk, mv, bsz, seq, p['mem_out_g'].reshape(1, GROUP_WIDTH).astype(f32))

    n_exp = p['router_w'].shape[1]
    tm_out = min(TM_OUT, n)
    ustrict = jnp.triu(jnp.ones((tm_out, tm_out), f32), k=1).astype(BF16)
    rw = p['router_w'].astype(f32)
    rw_hi = rw.astype(BF16)
    rw_lo = (rw - rw_hi.astype(f32)).astype(BF16)
    rw_parts = jnp.pad(jnp.concatenate([rw_hi, rw_lo], axis=1), ((0, 0), (0, LANES - 2 * n_exp)))
    x1, hn, idx, gates, rank, cnt = _outproj(
        x2, ya, yb, yc, ym, p['w_out'].astype(BF16), p['ffn_norm_g'].reshape(1, d).astype(f32),
        rw_parts, p['router_b'].reshape(n_exp, 1).astype(f32), ustrict)

    cap = n * TOP_K + n_exp * TM_EXPERT
    cnt_i = cnt[:, 0].astype(I32)
    dest, meta, starts = _plan(cnt_i, idx, rank, cap // TM_EXPERT)
    xs = _dispatch(dest, cnt_i, starts, hn, cap)
    ys = _experts(meta, xs, layer, *experts)
    gates_pad = jnp.pad(gates, ((0, LANES - TOP_K), (0, 0)))
    return _combine(dest, x1, gates_pad, ys)


_PARAM_NAMES = ('mix_norm_g', 'w_in', 'ssm_lambda_re', 'ssm_lambda_im', 'ssm_log_dt',
                'ssm_b_re', 'ssm_b_im', 'ssm_c_re', 'ssm_c_im', 'ssm_d', 'ssm_glu_w', 'ssm_glu_b',
                'ssm_out_g', 'pool_w', 'pool_scale', 'fox_forget_b', 'fox_q_g', 'fox_k_g',
                'fox_out_g', 'mem_norm_g', 'mem_w_kv', 'mem_q_g', 'mem_k_g', 'mem_out_g', 'w_out',
                'ffn_norm_g', 'router_w', 'router_b', 'exp_w_gate', 'exp_b_gate', 'exp_w_up',
                'exp_b_up', 'exp_w_down', 'exp_b_down')


def kernel(x, mem, mix_norm_g, w_in, ssm_lambda_re, ssm_lambda_im, ssm_log_dt, ssm_b_re, ssm_b_im,
           ssm_c_re, ssm_c_im, ssm_d, ssm_glu_w, ssm_glu_b, ssm_out_g, pool_w, pool_scale,
           fox_forget_b, fox_q_g, fox_k_g, fox_out_g, mem_norm_g, mem_w_kv, mem_q_g, mem_k_g,
           mem_out_g, w_out, ffn_norm_g, router_w, router_b, exp_w_gate, exp_b_gate, exp_w_up,
           exp_b_up, exp_w_down, exp_b_down):
    stacked = dict(zip(_PARAM_NAMES, (
        mix_norm_g, w_in, ssm_lambda_re, ssm_lambda_im, ssm_log_dt, ssm_b_re, ssm_b_im, ssm_c_re,
        ssm_c_im, ssm_d, ssm_glu_w, ssm_glu_b, ssm_out_g, pool_w, pool_scale, fox_forget_b,
        fox_q_g, fox_k_g, fox_out_g, mem_norm_g, mem_w_kv, mem_q_g, mem_k_g, mem_out_g, w_out,
        ffn_norm_g, router_w, router_b, exp_w_gate, exp_b_gate, exp_w_up, exp_b_up, exp_w_down,
        exp_b_down)))
    bsz, seq, d = x.shape
    depth = w_in.shape[0]
    x2 = x.reshape(bsz * seq, d).astype(F32)
    mem = mem.astype(F32)
    expert_names = ('exp_w_gate', 'exp_b_gate', 'exp_w_up', 'exp_b_up', 'exp_w_down', 'exp_b_down')
    experts = tuple(stacked[k].astype(F32) if stacked[k].ndim == 4
                    else stacked[k].astype(F32)[:, :, None, :] for k in expert_names)
    for layer in range(depth):
        x2 = _layer(x2, mem, bsz, seq,
                    {k: v[layer] for k, v in stacked.items() if k not in expert_names},
                    layer, experts)
    return x2.reshape(bsz, seq, d).astype(x.dtype)
```

```python
import functools
import math

import jax
import jax.numpy as jnp
from jax import lax
from jax.experimental import pallas as pl
from jax.experimental.pallas import tpu as pltpu

F32 = jnp.float32
BF16 = jnp.bfloat16
I32 = jnp.int32

EPS = 1e-6
HEAD_DIM = 64
N_HEADS = 4
GROUP_WIDTH = 256
LANES = 128
SUBLANES = 8
ROW_CHUNKS = 8
SSM_GROUPS = 16
SSM_CH = 16
SSM_STATE = 64
SSM_LANES = SSM_GROUPS * SSM_STATE
POOL_WINDOWS = (2, 4, 8, 16)
POOL_HALO = 16
TOP_K = 4
SWIGLU_LIMIT = 7.0
SWIGLU_ALPHA = 1.702
VMEM_LIMIT = 56 * 1024 * 1024

TM_INPROJ = 512
T_S5 = 512
T_POOL = 512
T_ATT = 512
TM_OUT = 512
TM_EXPERT = 512
T_MOVE = 256
BF16_ROWS = 16
WINDOW_SHIFT = 7
WINDOW_ROWS = 1 << WINDOW_SHIFT
SEL_CHUNK_WINDOWS = 8

NEG_INF = float("-inf")
LOG2E = 1.4426950408889634


def _cparams(sem):
    return pltpu.CompilerParams(dimension_semantics=sem, vmem_limit_bytes=VMEM_LIMIT)


def _lane_iota(shape):
    return lax.broadcasted_iota(I32, shape, len(shape) - 1)


def _split3(x):
    hi = x.astype(BF16).astype(F32)
    r = x - hi
    mid = r.astype(BF16).astype(F32)
    lo = r - mid
    return hi, mid, lo


def _head_slab(slab, odd):
    lane = _lane_iota(slab.shape)
    if odd:
        slab = pltpu.roll(slab, HEAD_DIM, axis=1)
    return jnp.where(lane < HEAD_DIM, slab, 0.0)


def _head_norm(xh, gain):
    ss = jnp.sum(xh * xh, axis=-1, keepdims=True)
    return xh * jnp.broadcast_to(lax.rsqrt(ss * (1.0 / HEAD_DIM) + EPS), xh.shape) * gain


def _divide_by_denominator(acc):
    inv = 1.0 / acc[:, HEAD_DIM:HEAD_DIM + 1]
    return acc * jnp.broadcast_to(inv, acc.shape)


def _join_heads(o_even, o_odd):
    lane = _lane_iota(o_even.shape)
    return jnp.where(lane < HEAD_DIM, o_even, pltpu.roll(o_odd, HEAD_DIM, axis=1))


def _store_token_tiles(ref, val):
    t = val.shape[0]
    for s in range(ROW_CHUNKS):
        ref[pl.ds(s, t, stride=ROW_CHUNKS), :] = val[:, s * LANES:(s + 1) * LANES]


def _load_token_tiles(ref, t):
    return jnp.concatenate([ref[pl.ds(s, t, stride=ROW_CHUNKS), :] for s in range(ROW_CHUNKS)],
                           axis=1)


def _group_norm_pairs(pairs, gain_ref, out_ref):
    ss = jnp.sum(sum(p * p for p in pairs), axis=-1, keepdims=True)
    scale = jnp.broadcast_to(lax.rsqrt(ss * (1.0 / GROUP_WIDTH) + EPS), pairs[0].shape)
    for i, p in enumerate(pairs):
        sl = slice(i * LANES, (i + 1) * LANES)
        out_ref[:, sl] = (p * scale * gain_ref[:, sl]).astype(out_ref.dtype)


COL_A, COL_B, COL_Q, COL_K, COL_V, COL_QM, COL_F = 0, 256, 512, 768, 1024, 1280, 1536
IN_COLS_PADDED = COL_F + LANES


def _inproj_kernel(x_ref, g_ref, w_ref, qg_ref, kg_ref, mqg_ref, fb_ref, tri_ref, sel_ref, ones_ref,
                   ua_ref, ub_ref, qa_ref, ka_ref, va_ref, qm_ref, carry_ref, *, tiles_per_seq):
    i = pl.program_id(0)

    @pl.when(i % tiles_per_seq == 0)
    def _():
        carry_ref[...] = jnp.zeros_like(carry_ref)

    x = x_ref[...]
    h = x * lax.rsqrt(jnp.mean(x * x, axis=-1, keepdims=True) + EPS) * g_ref[...]
    proj = jnp.dot(h.astype(BF16), w_ref[...], preferred_element_type=F32)
    ua_ref[...] = proj[:, COL_A:COL_A + GROUP_WIDTH]
    ub_ref[...] = proj[:, COL_B:COL_B + GROUP_WIDTH]

    z = proj[:, COL_F:COL_F + LANES] + fb_ref[...]
    lane = _lane_iota(z.shape)
    logf = jnp.minimum(z, 0.0) - jnp.log(1.0 + jnp.exp(-jnp.abs(z)))
    hi, mid, lo = _split3(logf)
    packed = jnp.where(lane < 4, hi, jnp.where(lane < 8, mid, jnp.where(lane < 12, lo, 0.0)))
    cs = jnp.dot(tri_ref[...], packed.astype(BF16), preferred_element_type=F32)
    cum = cs + pltpu.roll(cs, LANES - 4, axis=1) + pltpu.roll(cs, LANES - 8, axis=1)
    cum = cum + carry_ref[...]
    carry_ref[...] = cum[cum.shape[0] - 1:, :]

    scale = HEAD_DIM ** -0.5 * LOG2E
    c_parts = jnp.concatenate(_split3(cum * LOG2E), axis=1).astype(BF16)
    bias = jnp.dot(c_parts, sel_ref[...], preferred_element_type=F32)

    def heads_normed(col, gain_ref):
        t = proj[:, col:col + GROUP_WIDTH]
        sq = t * t
        sq_hi = sq.astype(BF16)
        sq_lo = (sq - sq_hi.astype(F32)).astype(BF16)
        ss = jnp.dot(jnp.concatenate([sq_hi, sq_lo], axis=1), ones_ref[...],
                     preferred_element_type=F32)
        return t * lax.rsqrt(ss * (1.0 / HEAD_DIM) + EPS) * gain_ref[...]

    qn_all = heads_normed(COL_Q, qg_ref) * scale
    kn_all = heads_normed(COL_K, kg_ref)
    mn_all = heads_normed(COL_QM, mqg_ref) * scale
    q_bias = (lane >= 64) & (lane < 67)
    k_bias = (lane >= 67) & (lane < 70)
    for hd in range(N_HEADS):
        pair, odd = hd // 2, hd % 2
        pair_sl = slice(pair * LANES, (pair + 1) * LANES)
        b = bias[:, hd * LANES:(hd + 1) * LANES]
        qn = _head_slab(qn_all[:, pair_sl], odd)
        qa_ref[hd] = jnp.where(q_bias, b, jnp.where(k_bias, 1.0, qn)).astype(BF16)
        kn = _head_slab(kn_all[:, pair_sl], odd)
        ka_ref[hd] = jnp.where(k_bias, b, jnp.where(q_bias, 1.0, kn)).astype(BF16)
        vs = _head_slab(proj[:, COL_V + pair * LANES:COL_V + (pair + 1) * LANES], odd)
        va_ref[hd] = jnp.where(lane == 64, 1.0, vs).astype(BF16)
        qm_ref[hd] = _head_slab(mn_all[:, pair_sl], odd).astype(BF16)


def _inproj(x2, seq, mix_g, w_all, qg, kg, mqg, fb, tri, sel, ones):
    n, d = x2.shape
    tm = min(TM_INPROJ, seq)
    grid = (n // tm,)
    const = lambda i: (0, 0)
    row = lambda i: (i, 0)
    hrow = lambda i: (0, i, 0)
    out_shape = (
        jax.ShapeDtypeStruct((n, GROUP_WIDTH), F32),
        jax.ShapeDtypeStruct((n, GROUP_WIDTH), F32),
        jax.ShapeDtypeStruct((N_HEADS, n, LANES), BF16),
        jax.ShapeDtypeStruct((N_HEADS, n, LANES), BF16),
        jax.ShapeDtypeStruct((N_HEADS, n, LANES), BF16),
        jax.ShapeDtypeStruct((N_HEADS, n, LANES), BF16),
    )
    return pl.pallas_call(
        functools.partial(_inproj_kernel, tiles_per_seq=seq // tm),
        grid=grid,
        in_specs=[
            pl.BlockSpec((tm, d), row),
            pl.BlockSpec((1, d), const),
            pl.BlockSpec((d, IN_COLS_PADDED), const),
            pl.BlockSpec((1, GROUP_WIDTH), const),
            pl.BlockSpec((1, GROUP_WIDTH), const),
            pl.BlockSpec((1, GROUP_WIDTH), const),
            pl.BlockSpec((1, LANES), const),
            pl.BlockSpec((tm, tm), const),
            pl.BlockSpec((3 * LANES, N_HEADS * LANES), const),
            pl.BlockSpec((2 * GROUP_WIDTH, GROUP_WIDTH), const),
        ],
        out_specs=(
            pl.BlockSpec((tm, GROUP_WIDTH), row),
            pl.BlockSpec((tm, GROUP_WIDTH), row),
            pl.BlockSpec((N_HEADS, tm, LANES), hrow),
            pl.BlockSpec((N_HEADS, tm, LANES), hrow),
            pl.BlockSpec((N_HEADS, tm, LANES), hrow),
            pl.BlockSpec((N_HEADS, tm, LANES), hrow),
        ),
        out_shape=out_shape,
        scratch_shapes=[pltpu.VMEM((1, LANES), F32)],
        compiler_params=_cparams(("arbitrary",)),
        name="inproj",
    )(x2, mix_g, w_all, qg, kg, mqg, fb, tri, sel, ones)


def _s5_kernel(u_ref, bmat_ref, coef_ref, cmat_ref, d_ref, gluw_ref, glub_ref, og_ref,
               o_ref, x_scr, carry_ref):
    @pl.when(pl.program_id(1) == 0)
    def _():
        carry_ref[...] = jnp.zeros_like(carry_ref)

    u = u_ref[...]
    t = u.shape[0]
    x_scr[...] = jnp.dot(u.astype(BF16), bmat_ref[...], preferred_element_type=F32)
    n_lane_blocks = SSM_LANES // LANES

    def group(gi, carry):
        r0 = pl.multiple_of(gi * SUBLANES, SUBLANES)
        rows = pl.ds(r0, SUBLANES)
        new = []
        for lb in range(n_lane_blocks):
            re_sl = slice(lb * LANES, (lb + 1) * LANES)
            im_sl = slice(SSM_LANES + lb * LANES, SSM_LANES + (lb + 1) * LANES)
            xr = x_scr[rows, re_sl]
            xi = x_scr[rows, im_sl]
            for s, k in enumerate((1, 2, 4)):
                cr = coef_ref[2 * s, :, re_sl]
                ci = coef_ref[2 * s + 1, :, re_sl]
                sr = pltpu.roll(xr, k, axis=0)
                si = pltpu.roll(xi, k, axis=0)
                xr, xi = xr + cr * sr - ci * si, xi + cr * si + ci * sr
            pr = coef_ref[6, :, re_sl]
            pi_ = coef_ref[7, :, re_sl]
            cbr, cbi = carry[2 * lb], carry[2 * lb + 1]
            xr, xi = xr + pr * cbr - pi_ * cbi, xi + pr * cbi + pi_ * cbr
            x_scr[rows, re_sl] = xr
            x_scr[rows, im_sl] = xi
            new.append(jnp.broadcast_to(xr[SUBLANES - 1:, :], xr.shape))
            new.append(jnp.broadcast_to(xi[SUBLANES - 1:, :], xi.shape))
        return tuple(new)

    carry0 = []
    for lb in range(n_lane_blocks):
        carry0.append(carry_ref[:, lb * LANES:(lb + 1) * LANES])
        carry0.append(carry_ref[:, SSM_LANES + lb * LANES:SSM_LANES + (lb + 1) * LANES])
    carry = lax.fori_loop(0, t // SUBLANES, group, tuple(carry0))
    for lb in range(n_lane_blocks):
        carry_ref[:, lb * LANES:(lb + 1) * LANES] = carry[2 * lb]
        carry_ref[:, SSM_LANES + lb * LANES:SSM_LANES + (lb + 1) * LANES] = carry[2 * lb + 1]

    y = jnp.dot(x_scr[...].astype(BF16), cmat_ref[...], preferred_element_type=F32) + d_ref[...] * u
    z = jax.nn.gelu(y, approximate=True)
    gate = jnp.dot(z.astype(BF16), gluw_ref[...], preferred_element_type=F32) + glub_ref[...]
    out = z * (1.0 / (1.0 + jnp.exp(-gate)))
    _group_norm_pairs([out[:, :LANES], out[:, LANES:]], og_ref, o_ref)


def _s5(ua, bsz, seq, bmat, coef, cmat, dvec, gluw, glub, og):
    n = ua.shape[0]
    t = min(T_S5, seq)
    nt = seq // t
    row = lambda b, j: (b * nt + j, 0)
    c2 = lambda b, j: (0, 0)
    c3 = lambda b, j: (0, 0, 0)
    return pl.pallas_call(
        _s5_kernel,
        grid=(bsz, nt),
        in_specs=[
            pl.BlockSpec((t, GROUP_WIDTH), row),
            pl.BlockSpec((GROUP_WIDTH, 2 * SSM_LANES), c2),
            pl.BlockSpec((8, SUBLANES, SSM_LANES), c3),
            pl.BlockSpec((2 * SSM_LANES, GROUP_WIDTH), c2),
            pl.BlockSpec((1, GROUP_WIDTH), c2),
            pl.BlockSpec((GROUP_WIDTH, GROUP_WIDTH), c2),
            pl.BlockSpec((1, GROUP_WIDTH), c2),
            pl.BlockSpec((1, GROUP_WIDTH), c2),
        ],
        out_specs=pl.BlockSpec((t, GROUP_WIDTH), row),
        out_shape=jax.ShapeDtypeStruct((n, GROUP_WIDTH), BF16),
        scratch_shapes=[pltpu.VMEM((t, 2 * SSM_LANES), F32),
                        pltpu.VMEM((SUBLANES, 2 * SSM_LANES), F32)],
        compiler_params=_cparams(("arbitrary", "arbitrary")),
        name="s5",
    )(ua, bmat, coef, cmat, dvec, gluw, glub, og)


def _s5_constants(lam_re, lam_im, log_dt, b_re, b_im, c_re, c_im):
    lr = lam_re.astype(F32)
    li = lam_im.astype(F32)
    dt = jnp.exp(log_dt.astype(F32))[:, None]
    mag = jnp.exp(lr * dt)
    a_re = mag * jnp.cos(li * dt)
    a_im = mag * jnp.sin(li * dt)
    den = lr * lr + li * li
    n_re = a_re - 1.0
    n_im = a_im
    k_re = (n_re * lr + n_im * li) / den
    k_im = (n_im * lr - n_re * li) / den
    br = b_re.astype(F32)
    bi = b_im.astype(F32)
    bb_re = k_re[..., None] * br - k_im[..., None] * bi
    bb_im = k_re[..., None] * bi + k_im[..., None] * br
    eye = jnp.eye(SSM_GROUPS, dtype=F32)
    bm_re = jnp.einsum('gph,gk->ghkp', bb_re, eye).reshape(GROUP_WIDTH, SSM_LANES)
    bm_im = jnp.einsum('gph,gk->ghkp', bb_im, eye).reshape(GROUP_WIDTH, SSM_LANES)
    bmat = jnp.concatenate([bm_re, bm_im], axis=1).astype(BF16)
    cm_re = jnp.einsum('ghp,gk->gpkh', c_re.astype(F32), eye).reshape(SSM_LANES, GROUP_WIDTH)
    cm_im = jnp.einsum('ghp,gk->gpkh', c_im.astype(F32), eye).reshape(SSM_LANES, GROUP_WIDTH)
    cmat = jnp.concatenate([cm_re, -cm_im], axis=0).astype(BF16)

    ar = a_re.reshape(1, SSM_LANES)
    ai = a_im.reshape(1, SSM_LANES)

    def cmul(x, y):
        return x[0] * y[0] - x[1] * y[1], x[0] * y[1] + x[1] * y[0]

    pows = [(ar, ai)]
    for _ in range(SUBLANES - 1):
        pows.append(cmul(pows[-1], (ar, ai)))
    rows = jnp.arange(SUBLANES, dtype=I32)[:, None]
    planes = []
    for k in (1, 2, 4):
        m = (rows >= k).astype(F32)
        planes += [m * pows[k - 1][0], m * pows[k - 1][1]]
    planes += [jnp.concatenate([p[0] for p in pows], axis=0),
               jnp.concatenate([p[1] for p in pows], axis=0)]
    coef = jnp.stack(planes, axis=0)
    return bmat, coef, cmat


def _pool_kernel(v_ref, w_ref, g_ref, o_ref, ext_ref):
    j = pl.program_id(1)
    t = v_ref.shape[0]

    @pl.when(j == 0)
    def _():
        ext_ref[0:POOL_HALO, :] = jnp.zeros((POOL_HALO, GROUP_WIDTH), F32)

    v = v_ref[...]
    ext_ref[POOL_HALO:POOL_HALO + t, :] = v
    cur = ext_ref[...]
    width = 1
    wins = {}
    while width < POOL_WINDOWS[-1]:
        cur = cur + pltpu.roll(cur, width, axis=0)
        width *= 2
        wins[width] = cur[POOL_HALO:, :]
    lane = _lane_iota(v.shape)
    pos = (j * t + lax.broadcasted_iota(I32, v.shape, 0) + 1).astype(F32)
    mean = None
    for gi, w in enumerate(POOL_WINDOWS):
        m = wins[w] / jnp.minimum(pos, float(w))
        mean = m if mean is None else jnp.where(lane >= gi * HEAD_DIM, m, mean)
    mixed = jnp.dot((mean - v).astype(BF16), w_ref[...], preferred_element_type=F32)
    _group_norm_pairs([mixed[:, :LANES], mixed[:, LANES:]], g_ref, o_ref)
    ext_ref[0:POOL_HALO, :] = v[t - POOL_HALO:, :]


def _pool(ub, bsz, seq, w_blk, g):
    n = ub.shape[0]
    t = min(T_POOL, seq)
    nt = seq // t
    row = lambda b, j: (b * nt + j, 0)
    c2 = lambda b, j: (0, 0)
    return pl.pallas_call(
        _pool_kernel,
        grid=(bsz, nt),
        in_specs=[pl.BlockSpec((t, GROUP_WIDTH), row),
                  pl.BlockSpec((GROUP_WIDTH, GROUP_WIDTH), c2),
                  pl.BlockSpec((1, GROUP_WIDTH), c2)],
        out_specs=pl.BlockSpec((t, GROUP_WIDTH), row),
        out_shape=jax.ShapeDtypeStruct((n, GROUP_WIDTH), BF16),
        scratch_shapes=[pltpu.VMEM((t + POOL_HALO, GROUP_WIDTH), F32)],
        compiler_params=_cparams(("arbitrary", "arbitrary")),
        name="pool",
    )(ub, w_blk, g)


def _fox_kernel(qt_ref, kt_ref, qa_ref, ka_ref, va_ref, g_ref, o_ref, m_ref, acc_ref):
    p_id = pl.program_id(1)
    qi = qt_ref[p_id]
    ki = kt_ref[p_id]
    tq = qa_ref.shape[1]
    tk = ka_ref.shape[1]

    @pl.when(ki == 0)
    def _():
        m_ref[...] = jnp.full_like(m_ref, NEG_INF)
        acc_ref[...] = jnp.zeros_like(acc_ref)

    def step(on_diagonal):
        if on_diagonal:
            causal = (lax.broadcasted_iota(I32, (tq, tk), 0)
                      >= lax.broadcasted_iota(I32, (tq, tk), 1))
        def scores(hd):
            return lax.dot_general(qa_ref[hd], ka_ref[hd], (((1,), (1,)), ((), ())),
                                   preferred_element_type=F32)

        s_next = scores(0)
        for hd in range(N_HEADS):
            s = s_next
            if hd + 1 < N_HEADS:
                s_next = scores(hd + 1)
            if on_diagonal:
                s = jnp.where(causal, s, NEG_INF)
            m_prev = m_ref[hd]
            m_new = jnp.maximum(m_prev, jnp.broadcast_to(jnp.max(s, axis=-1, keepdims=True),
                                                         m_prev.shape))
            alpha = jnp.exp2(m_prev - m_new)
            p = jnp.exp2(s - jnp.concatenate([m_new] * (tk // LANES), axis=1))
            acc_ref[hd] = alpha * acc_ref[hd] + jnp.dot(p.astype(BF16), va_ref[hd],
                                                        preferred_element_type=F32)
            m_ref[hd] = m_new

    @pl.when(ki < qi)
    def _():
        step(False)

    @pl.when(ki == qi)
    def _():
        step(True)
        heads = []
        for hd in range(N_HEADS):
            acc = acc_ref[hd]
            heads.append(_divide_by_denominator(acc))
        _group_norm_pairs([_join_heads(heads[0], heads[1]), _join_heads(heads[2], heads[3])],
                          g_ref, o_ref)


def _fox(qa, ka, va, bsz, seq, g):
    n = qa.shape[1]
    t = min(T_ATT, seq)
    nq = seq // t
    pairs = [(q, k) for q in range(nq) for k in range(q + 1)]
    qt = jnp.asarray([p[0] for p in pairs], I32)
    kt = jnp.asarray([p[1] for p in pairs], I32)
    qmap = lambda b, p, qt, kt: (0, b * nq + qt[p], 0)
    kmap = lambda b, p, qt, kt: (0, b * nq + kt[p], 0)
    grid_spec = pltpu.PrefetchScalarGridSpec(
        num_scalar_prefetch=2,
        grid=(bsz, len(pairs)),
        in_specs=[pl.BlockSpec((N_HEADS, t, LANES), qmap),
                  pl.BlockSpec((N_HEADS, t, LANES), kmap),
                  pl.BlockSpec((N_HEADS, t, LANES), kmap),
                  pl.BlockSpec((1, GROUP_WIDTH), lambda b, p, qt, kt: (0, 0))],
        out_specs=pl.BlockSpec((t, GROUP_WIDTH), lambda b, p, qt, kt: (b * nq + qt[p], 0)),
        scratch_shapes=[pltpu.VMEM((N_HEADS, t, LANES), F32),
                        pltpu.VMEM((N_HEADS, t, LANES), F32)],
    )
    return pl.pallas_call(
        _fox_kernel,
        grid_spec=grid_spec,
        out_shape=jax.ShapeDtypeStruct((n, GROUP_WIDTH), BF16),
        compiler_params=_cparams(("arbitrary", "arbitrary")),
        name="fox",
    )(qt, kt, qa, ka, va, g)


def _memkv_kernel(mem_ref, g_ref, w_ref, kg_ref, mk_ref, mv_ref):
    x = mem_ref[0]
    h = x * lax.rsqrt(jnp.mean(x * x, axis=-1, keepdims=True) + EPS) * g_ref[...]
    kv = jnp.dot(h.astype(BF16), w_ref[...], preferred_element_type=F32)
    lane = _lane_iota((x.shape[0], LANES))
    for hd in range(N_HEADS):
        pair, odd = hd // 2, hd % 2
        ks = _head_slab(kv[:, pair * LANES:(pair + 1) * LANES], odd)
        mk_ref[0, hd] = _head_norm(ks, kg_ref[...]).astype(BF16)
        vs = _head_slab(kv[:, GROUP_WIDTH + pair * LANES:GROUP_WIDTH + (pair + 1) * LANES], odd)
        mv_ref[0, hd] = jnp.where(lane == HEAD_DIM, 1.0, vs).astype(BF16)


def _memkv(mem, g, w_kv, kg):
    bsz, m, d = mem.shape
    c2 = lambda b: (0, 0)
    out = jax.ShapeDtypeStruct((bsz, N_HEADS, m, LANES), BF16)
    return pl.pallas_call(
        _memkv_kernel,
        grid=(bsz,),
        in_specs=[pl.BlockSpec((1, m, d), lambda b: (b, 0, 0)),
                  pl.BlockSpec((1, d), c2),
                  pl.BlockSpec((d, 2 * GROUP_WIDTH), c2),
                  pl.BlockSpec((1, LANES), c2)],
        out_specs=(pl.BlockSpec((1, N_HEADS, m, LANES), lambda b: (b, 0, 0, 0)),
                   pl.BlockSpec((1, N_HEADS, m, LANES), lambda b: (b, 0, 0, 0))),
        out_shape=(out, out),
        compiler_params=_cparams(("arbitrary",)),
        name="memkv",
    )(mem, g, w_kv, kg)


def _memattn_kernel(qm_ref, mk_ref, mv_ref, g_ref, o_ref):
    heads = []
    for hd in range(N_HEADS):
        s = lax.dot_general(qm_ref[hd], mk_ref[0, hd], (((1,), (1,)), ((), ())),
                            preferred_element_type=F32)
        m = jnp.broadcast_to(jnp.max(s, axis=-1, keepdims=True), (s.shape[0], LANES))
        p = jnp.exp2(s - jnp.concatenate([m] * (s.shape[1] // LANES), axis=1))
        acc = jnp.dot(p.astype(BF16), mv_ref[0, hd], preferred_element_type=F32)
        heads.append(_divide_by_denominator(acc))
    _group_norm_pairs([_join_heads(heads[0], heads[1]), _join_heads(heads[2], heads[3])],
                      g_ref, o_ref)


def _memattn(qm, mk, mv, bsz, seq, g):
    n = qm.shape[1]
    m = mk.shape[2]
    t = min(T_ATT, seq)
    nt = seq // t
    return pl.pallas_call(
        _memattn_kernel,
        grid=(bsz, nt),
        in_specs=[pl.BlockSpec((N_HEADS, t, LANES), lambda b, j: (0, b * nt + j, 0)),
                  pl.BlockSpec((1, N_HEADS, m, LANES), lambda b, j: (b, 0, 0, 0)),
                  pl.BlockSpec((1, N_HEADS, m, LANES), lambda b, j: (b, 0, 0, 0)),
                  pl.BlockSpec((1, GROUP_WIDTH), lambda b, j: (0, 0))],
        out_specs=pl.BlockSpec((t, GROUP_WIDTH), lambda b, j: (b * nt + j, 0)),
        out_shape=jax.ShapeDtypeStruct((n, GROUP_WIDTH), BF16),
        compiler_params=_cparams(("arbitrary", "arbitrary")),
        name="memattn",
    )(qm, mk, mv, g)


def _outproj_kernel(x_ref, ya_ref, yb_ref, yc_ref, ym_ref, w_ref, fg_ref, rw_ref, rb_ref, us_ref,
                    x1_ref, hn_ref, idx_ref, gate_ref, rank_ref, cnt_ref, toff_ref, carry_ref):
    @pl.when(pl.program_id(0) == 0)
    def _():
        carry_ref[...] = jnp.zeros_like(carry_ref)

    merged = jnp.concatenate([ya_ref[...], yb_ref[...], yc_ref[...], ym_ref[...]], axis=1)
    acc = x_ref[...] + jnp.dot(merged, w_ref[...], preferred_element_type=F32)
    x1_ref[...] = acc
    hn = acc * lax.rsqrt(jnp.mean(acc * acc, axis=-1, keepdims=True) + EPS) * fg_ref[...]
    _store_token_tiles(hn_ref, hn)
    n_exp = rb_ref.shape[0]
    tm = hn.shape[0]
    hn_hi = hn.astype(BF16)
    hn_lo = (hn - hn_hi.astype(F32)).astype(BF16)
    parts = jnp.dot(jnp.concatenate([hn_hi, hn_lo], axis=0), rw_ref[...],
                    preferred_element_type=F32)
    top = jnp.transpose(parts[:tm])
    bot = jnp.transpose(parts[tm:])
    logits = (top[:n_exp] + top[n_exp:2 * n_exp] + bot[:n_exp] + bot[n_exp:2 * n_exp]
              + rb_ref[...])
    e_iota = lax.broadcasted_iota(I32, (n_exp, tm), 0).astype(F32)
    work = logits
    vals, onehots = [], []
    for k in range(TOP_K):
        m = jnp.max(work, axis=0, keepdims=True)
        sel = jnp.min(jnp.where(work == m, e_iota, float(n_exp)), axis=0, keepdims=True)
        hot = e_iota == sel
        idx_ref[k:k + 1, :] = sel.astype(I32)
        vals.append(m)
        onehots.append(hot.astype(F32))
        work = jnp.where(hot, NEG_INF, work)
    exps = [jnp.exp(v - vals[0]) for v in vals]
    denom = exps[0] + exps[1] + exps[2] + exps[3]
    for k in range(TOP_K):
        gate_ref[k:k + 1, :] = exps[k] / denom
    stacked = jnp.concatenate(onehots, axis=0).astype(BF16)
    prefix = jnp.dot(stacked, us_ref[...], preferred_element_type=F32)
    base = carry_ref[...]
    base_sq = jnp.concatenate([base, jnp.zeros((LANES - n_exp, LANES), F32)], axis=0)
    toff_ref[...] = jnp.transpose(base_sq)[:SUBLANES, :]
    for k in range(TOP_K):
        hot = onehots[k]
        pk = prefix[k * n_exp:(k + 1) * n_exp, :]
        rank = jnp.sum(hot * (pk + base[:, 0:1]), axis=0, keepdims=True)
        rank_ref[k:k + 1, :] = rank.astype(I32)
        base = base + jnp.sum(hot, axis=1, keepdims=True)
    carry_ref[...] = base
    cnt_ref[...] = base


def _outproj(x2, ya, yb, yc, ym, w_out, fg, rw, rb, ustrict):
    n, d = x2.shape
    n_exp = rb.shape[0]
    tm = min(TM_OUT, n)
    row = lambda i: (i, 0)
    col = lambda i: (0, i)
    const = lambda i: (0, 0)
    return pl.pallas_call(
        _outproj_kernel,
        grid=(n // tm,),
        in_specs=[pl.BlockSpec((tm, d), row)]
        + [pl.BlockSpec((tm, GROUP_WIDTH), row)] * 4
        + [pl.BlockSpec((d, d), const),
           pl.BlockSpec((1, d), const),
           pl.BlockSpec((d, LANES), const),
           pl.BlockSpec((n_exp, 1), const),
           pl.BlockSpec((tm, tm), const)],
        out_specs=(pl.BlockSpec((tm, d), row),
                   pl.BlockSpec((tm * ROW_CHUNKS, LANES), row),
                   pl.BlockSpec((TOP_K, tm), col),
                   pl.BlockSpec((TOP_K, tm), col),
                   pl.BlockSpec((TOP_K, tm), col),
                   pl.BlockSpec((n_exp, LANES), const),
                   pl.BlockSpec((SUBLANES, LANES), row)),
        out_shape=(jax.ShapeDtypeStruct((n, d), F32),
                   jax.ShapeDtypeStruct((n * ROW_CHUNKS, LANES), F32),
                   jax.ShapeDtypeStruct((TOP_K, n), I32),
                   jax.ShapeDtypeStruct((TOP_K, n), F32),
                   jax.ShapeDtypeStruct((TOP_K, n), I32),
                   jax.ShapeDtypeStruct((n_exp, LANES), F32),
                   jax.ShapeDtypeStruct((n // tm * SUBLANES, LANES), F32)),
        scratch_shapes=[pltpu.VMEM((n_exp, LANES), F32)],
        compiler_params=_cparams(("arbitrary",)),
        name="outproj",
    )(x2, ya, yb, yc, ym, w_out, fg, rw, rb, ustrict)


def _plan_kernel(cnt_ref, idx_ref, rank_ref, toff_ref, dest_ref, meta_ref, start_ref, win_ref,
                 *, n_exp, block_rows):
    def body(e, off):
        start_ref[e] = off
        return off + (cnt_ref[e] + block_rows - 1) // block_rows * block_rows

    total = lax.fori_loop(0, n_exp, body, jnp.int32(0))
    idx = idx_ref[...]
    dest = rank_ref[...]
    blk_start = lax.broadcasted_iota(I32, meta_ref.shape, 1) * block_rows
    blk_e = jnp.zeros(meta_ref.shape, I32)
    lane = _lane_iota((SUBLANES, LANES))
    start_l = jnp.zeros((SUBLANES, LANES), I32)
    cnt_l = jnp.zeros((SUBLANES, LANES), I32)
    for e in range(n_exp):
        dest = dest + jnp.where(idx == e, start_ref[e], 0)
        start_l = jnp.where(lane == e, start_ref[e], start_l)
        cnt_l = jnp.where(lane == e, cnt_ref[e], cnt_l)
        if e > 0:
            blk_e = blk_e + (blk_start >= start_ref[e]).astype(I32)
    dest_ref[...] = dest
    sub = lax.broadcasted_iota(I32, meta_ref.shape, 0)
    meta_ref[...] = jnp.where(sub == 0, blk_e, total // block_rows)

    toff = toff_ref[...].astype(I32)
    n_rows = toff.shape[0]
    nxt = jnp.concatenate([toff[SUBLANES:], cnt_l], axis=0) if n_rows > SUBLANES else cnt_l
    first = jnp.concatenate([start_l] * (n_rows // SUBLANES), axis=0) + toff
    aligned = jnp.bitwise_and(first, -BF16_ROWS)
    span = jnp.where(_lane_iota(toff.shape) < n_exp, first - aligned + (nxt - toff), 0)
    longest = jnp.max(span.astype(F32), axis=-1, keepdims=True).astype(I32)
    rounds = jnp.right_shift(longest + (WINDOW_ROWS - 1), WINDOW_SHIFT)
    win_ref[...] = jnp.where(_lane_iota(toff.shape) == n_exp, jnp.maximum(rounds, 1), aligned)


def _plan(cnt, idx, rank, toff, n_blocks):
    n_exp = cnt.shape[0]
    nb_pad = (n_blocks + LANES - 1) // LANES * LANES
    return pl.pallas_call(
        functools.partial(_plan_kernel, n_exp=n_exp, block_rows=TM_EXPERT),
        in_specs=[pl.BlockSpec(memory_space=pltpu.SMEM),
                  pl.BlockSpec(memory_space=pltpu.VMEM),
                  pl.BlockSpec(memory_space=pltpu.VMEM),
                  pl.BlockSpec(memory_space=pltpu.VMEM)],
        out_specs=(pl.BlockSpec(memory_space=pltpu.VMEM),
                   pl.BlockSpec(memory_space=pltpu.VMEM),
                   pl.BlockSpec(memory_space=pltpu.SMEM),
                   pl.BlockSpec(memory_space=pltpu.VMEM)),
        out_shape=(jax.ShapeDtypeStruct(idx.shape, I32),
                   jax.ShapeDtypeStruct((SUBLANES, nb_pad), I32),
                   jax.ShapeDtypeStruct((n_exp,), I32),
                   jax.ShapeDtypeStruct(toff.shape, I32)),
        compiler_params=pltpu.CompilerParams(vmem_limit_bytes=VMEM_LIMIT),
        name="plan",
    )(cnt, idx, rank, toff)


def _tile_rows(row):
    return pl.ds(pl.multiple_of(row * ROW_CHUNKS, ROW_CHUNKS), ROW_CHUNKS)


def _dispatch_kernel(dest_ref, cnt_ref, start_ref, hn_ref, xs_ref, zero_ref, sem, zsem,
                     *, n_exp, block_rows):
    tm = dest_ref.shape[1]

    def pad_copy(e, r):
        return pltpu.make_async_copy(zero_ref.at[pl.ds(0, ROW_CHUNKS)],
                                     xs_ref.at[_tile_rows(start_ref[e] + r)], zsem)

    def pad_bounds(e):
        cnt = cnt_ref[e]
        return cnt, (cnt + block_rows - 1) // block_rows * block_rows

    @pl.when(pl.program_id(0) == 0)
    def _():
        zero_ref[...] = jnp.zeros_like(zero_ref)

        def pad_start(e, c):
            lo, hi = pad_bounds(e)
            return lax.fori_loop(lo, hi, lambda r, c: (pad_copy(e, r).start(), c)[1], c)

        def pad_wait(e, c):
            lo, hi = pad_bounds(e)
            return lax.fori_loop(lo, hi, lambda r, c: (pad_copy(e, r).wait(), c)[1], c)

        lax.fori_loop(0, n_exp, pad_start, 0)
        lax.fori_loop(0, n_exp, pad_wait, 0)

        last = n_exp - 1
        used = start_ref[last] + pad_bounds(last)[1]
        n_blocks = xs_ref.shape[0] // (ROW_CHUNKS * block_rows)

        def tail_copy(b):
            rows = pl.ds(pl.multiple_of(b * (block_rows * ROW_CHUNKS), block_rows * ROW_CHUNKS),
                         block_rows * ROW_CHUNKS)
            return pltpu.make_async_copy(zero_ref, xs_ref.at[rows], zsem)

        lax.fori_loop(used // block_rows, n_blocks, lambda b, c: (tail_copy(b).start(), c)[1], 0)
        lax.fori_loop(used // block_rows, n_blocks, lambda b, c: (tail_copy(b).wait(), c)[1], 0)

    def copy(k, t):
        return pltpu.make_async_copy(hn_ref.at[_tile_rows(t)],
                                     xs_ref.at[_tile_rows(dest_ref[k, t])], sem)

    def start(t, c):
        for k in range(TOP_K):
            copy(k, t).start(priority=k % 2)
        return c

    def wait(t, c):
        for k in range(TOP_K):
            copy(k, t).wait()
        return c

    lax.fori_loop(0, tm, start, 0)
    lax.fori_loop(0, tm, wait, 0)


def _dispatch(dest, cnt, starts, hn, cap):
    n = hn.shape[0] // ROW_CHUNKS
    tm = min(T_MOVE, n)
    return pl.pallas_call(
        functools.partial(_dispatch_kernel, n_exp=cnt.shape[0], block_rows=TM_EXPERT),
        grid=(n // tm,),
        in_specs=[pl.BlockSpec((TOP_K, tm), lambda i: (0, i), memory_space=pltpu.SMEM),
                  pl.BlockSpec(memory_space=pltpu.SMEM),
                  pl.BlockSpec(memory_space=pltpu.SMEM),
                  pl.BlockSpec((tm * ROW_CHUNKS, LANES), lambda i: (i, 0))],
        out_specs=pl.BlockSpec(memory_space=pl.ANY),
        out_shape=jax.ShapeDtypeStruct((cap * ROW_CHUNKS, LANES), hn.dtype),
        scratch_shapes=[pltpu.VMEM((TM_EXPERT * ROW_CHUNKS, LANES), hn.dtype),
                        pltpu.SemaphoreType.DMA, pltpu.SemaphoreType.DMA],
        compiler_params=_cparams(("arbitrary",)),
        name="dispatch",
    )(dest, cnt, starts, hn)


def _experts_kernel(meta_ref, xs_ref, wg_ref, bg_ref, wu_ref, bu_ref, wd_ref, bd_ref, ys_ref,
                    wg_s, wu_s, wd_s):
    i = pl.program_id(0)
    tm = xs_ref.shape[0] // ROW_CHUNKS
    in_use = i < meta_ref[1, 0]
    new_expert = jnp.logical_or(i == 0, meta_ref[0, i] != meta_ref[0, jnp.maximum(i - 1, 0)])

    @pl.when(jnp.logical_not(in_use))
    def _():
        ys_ref[...] = jnp.zeros_like(ys_ref)

    @pl.when(jnp.logical_and(in_use, new_expert))
    def _():
        wg_s[...] = wg_ref[0, 0].astype(BF16)
        wu_s[...] = wu_ref[0, 0].astype(BF16)
        wd_s[...] = wd_ref[0, 0].astype(BF16)

    @pl.when(in_use)
    def _():
        x = _load_token_tiles(xs_ref, tm).astype(BF16)
        gl = jnp.dot(x, wg_s[...], preferred_element_type=F32) + bg_ref[0, 0]
        up = jnp.dot(x, wu_s[...], preferred_element_type=F32) + bu_ref[0, 0]
        gl = jnp.minimum(gl, SWIGLU_LIMIT)
        up = jnp.clip(up, -SWIGLU_LIMIT, SWIGLU_LIMIT)
        act = gl * (1.0 / (1.0 + jnp.exp(-SWIGLU_ALPHA * gl)))
        hmid = ((up + 1.0) * act).astype(BF16)
        y = jnp.dot(hmid, wd_s[...], preferred_element_type=F32) + bd_ref[0, 0]
        ys_ref[...] = y.astype(ys_ref.dtype)


def _experts(meta, xs, layer, wg, bg, wu, bu, wd, bd):
    cap = xs.shape[0] // ROW_CHUNKS
    _, _, d, de = wg.shape
    tm = TM_EXPERT
    n_blocks = cap // tm

    def blk(i, meta):
        return jnp.minimum(i, meta[1, 0] - 1)

    rows = lambda i, meta: (blk(i, meta), 0)
    wmap = lambda i, meta: (layer, meta[0, blk(i, meta)], 0, 0)
    grid_spec = pltpu.PrefetchScalarGridSpec(
        num_scalar_prefetch=1,
        grid=(n_blocks,),
        in_specs=[pl.BlockSpec((tm * ROW_CHUNKS, LANES), rows),
                  pl.BlockSpec((1, 1, d, de), wmap),
                  pl.BlockSpec((1, 1, 1, de), wmap),
                  pl.BlockSpec((1, 1, d, de), wmap),
                  pl.BlockSpec((1, 1, 1, de), wmap),
                  pl.BlockSpec((1, 1, de, d), wmap),
                  pl.BlockSpec((1, 1, 1, d), wmap)],
        out_specs=pl.BlockSpec((tm, d), lambda i, meta: (i, 0)),
        scratch_shapes=[pltpu.VMEM((d, de), BF16), pltpu.VMEM((d, de), BF16),
                        pltpu.VMEM((de, d), BF16)],
    )
    return pl.pallas_call(
        _experts_kernel,
        grid_spec=grid_spec,
        out_shape=jax.ShapeDtypeStruct((cap, d), BF16),
        compiler_params=_cparams(("arbitrary",)),
        name="experts",
    )(meta, xs, wg, bg, wu, bu, wd, bd)


def _combine_kernel(win_ref, dest_ref, idx_ref, gate_ref, x1_ref, ys_ref, o_ref, buf_ref, sem,
                    *, n_exp):
    i = pl.program_id(0)
    n_tiles = pl.num_programs(0)
    tm = x1_ref.shape[0]
    last_window = ys_ref.shape[0] - WINDOW_ROWS

    def window_start(tile, e, rnd):
        return jnp.minimum(win_ref[tile, e] + rnd * WINDOW_ROWS, last_window)

    def window_copy(tile, e, rnd, slot):
        rows = pl.ds(pl.multiple_of(window_start(tile, e, rnd), BF16_ROWS), WINDOW_ROWS)
        return pltpu.make_async_copy(ys_ref.at[rows],
                                     buf_ref.at[slot, pl.ds(e * WINDOW_ROWS, WINDOW_ROWS)],
                                     sem.at[slot])

    def fetch(tile, rnd, slot):
        for e in range(n_exp):
            window_copy(tile, e, rnd, slot).start(priority=e % 2)

    def drain(tile, rnd, slot):
        for e in range(n_exp):
            window_copy(tile, e, rnd, slot).wait()

    @pl.when(i == 0)
    def _():
        fetch(0, 0, 0)

    @pl.when(i + 1 < n_tiles)
    def _():
        fetch(i + 1, 0, (i + 1) % 2)

    idx = idx_ref[...]
    dest = dest_ref[...]
    gates = gate_ref[...]
    chunk = SEL_CHUNK_WINDOWS * WINDOW_ROWS
    lane = _lane_iota((tm, chunk)).astype(F32)

    def weighted_rows(rnd, slot):
        start_tok = jnp.zeros(idx.shape, I32)
        for e in range(n_exp):
            start_tok = jnp.where(idx == e, window_start(i, e, rnd), start_tok)
        local = dest - start_tok
        col = jnp.where((local >= 0) & (local < WINDOW_ROWS), idx * WINDOW_ROWS + local, -1)
        packed = jnp.concatenate(
            [col.astype(F32), gates, jnp.zeros((LANES - 2 * TOP_K, tm), F32)], axis=0)
        by_tok = jnp.transpose(packed)

        def spread(j):
            one = jnp.broadcast_to(by_tok[:, j:j + 1], (tm, LANES))
            return jnp.concatenate([one] * (chunk // LANES), axis=1)

        cols = [spread(k) for k in range(TOP_K)]
        gts = [spread(TOP_K + k) for k in range(TOP_K)]
        total = None
        for c in range(n_exp // SEL_CHUNK_WINDOWS):
            sel = jnp.zeros((tm, chunk), F32)
            for k in range(TOP_K):
                sel = jnp.where(lane == cols[k] - float(c * chunk), gts[k], sel)
            part = jnp.dot(sel.astype(BF16), buf_ref[slot, c * chunk:(c + 1) * chunk, :],
                           preferred_element_type=F32)
            total = part if total is None else total + part
        return total

    slot = i % 2
    for s in range(2):
        @pl.when(slot == s)
        def _(s=s):
            drain(i, 0, s)
            o_ref[...] = x1_ref[...] + weighted_rows(0, s)

    for s in range(2):
        @pl.when(slot == s)
        def _(s=s):
            def extra_round(rnd, c):
                fetch(i, rnd, s)
                drain(i, rnd, s)
                o_ref[...] += weighted_rows(rnd, s)
                return c

            lax.fori_loop(1, win_ref[i, n_exp], extra_round, 0)


def _combine(win, dest, idx, gates, x1, ys, n_exp):
    n, d = x1.shape
    tm = min(TM_OUT, n)
    vec = lambda i, win: (0, i)
    grid_spec = pltpu.PrefetchScalarGridSpec(
        num_scalar_prefetch=1,
        grid=(n // tm,),
        in_specs=[pl.BlockSpec((TOP_K, tm), vec),
                  pl.BlockSpec((TOP_K, tm), vec),
                  pl.BlockSpec((TOP_K, tm), vec),
                  pl.BlockSpec((tm, d), lambda i, win: (i, 0)),
                  pl.BlockSpec(memory_space=pl.ANY)],
        out_specs=pl.BlockSpec((tm, d), lambda i, win: (i, 0)),
        scratch_shapes=[pltpu.VMEM((2, n_exp * WINDOW_ROWS, d), ys.dtype),
                        pltpu.SemaphoreType.DMA((2,))],
    )
    return pl.pallas_call(
        functools.partial(_combine_kernel, n_exp=n_exp),
        grid_spec=grid_spec,
        out_shape=jax.ShapeDtypeStruct((n, d), F32),
        compiler_params=_cparams(("arbitrary",)),
        name="combine",
    )(win, dest, idx, gates, x1, ys)


def _pad_lanes(v, width=LANES):
    v = v.astype(F32).reshape(1, -1)
    return jnp.pad(v, ((0, 0), (0, width - v.shape[1])))


def _layer(x2, mem, bsz, seq, p, layer, experts):
    n, d = x2.shape
    f32 = F32
    w_in = p['w_in']
    off_q = 2 * GROUP_WIDTH
    off_k, off_v, off_f = off_q + GROUP_WIDTH, off_q + 2 * GROUP_WIDTH, off_q + 3 * GROUP_WIDTH
    off_qm = off_f + N_HEADS
    w_f = w_in[:, off_f:off_qm]
    w_f_pad = jnp.pad(jnp.concatenate([w_f, w_f, w_f], axis=1), ((0, 0), (0, LANES - 3 * N_HEADS)))
    w_all = jnp.concatenate([w_in[:, :off_f], w_in[:, off_qm:], w_f_pad], axis=1).astype(BF16)
    fb = p['fox_forget_b'].astype(f32)
    fb_pad = _pad_lanes(jnp.concatenate([fb, fb, fb]))
    tm_in = min(TM_INPROJ, seq)
    tri = jnp.tril(jnp.ones((tm_in, tm_in), f32)).astype(BF16)

    sel = jnp.zeros((3, LANES, N_HEADS, LANES), f32)
    for part in range(3):
        for hd in range(N_HEADS):
            sel = sel.at[part, hd, hd, HEAD_DIM + part].set(1.0)
            sel = sel.at[part, hd, hd, HEAD_DIM + 3 + part].set(-1.0)
    sel = sel.reshape(3 * LANES, N_HEADS * LANES).astype(BF16)
    head_of = jnp.arange(GROUP_WIDTH) // HEAD_DIM
    ones_blk = (head_of[:, None] == head_of[None, :]).astype(BF16)
    ones2 = jnp.concatenate([ones_blk, ones_blk], axis=0)

    def head_gain(g):
        return jnp.tile(g.astype(f32).reshape(1, HEAD_DIM), (1, N_HEADS))

    ua, ub, qa, ka, va, qm = _inproj(
        x2, seq, p['mix_norm_g'].reshape(1, d).astype(f32), w_all,
        head_gain(p['fox_q_g']), head_gain(p['fox_k_g']), head_gain(p['mem_q_g']), fb_pad, tri,
        sel, ones2)

    bmat, coef, cmat = _s5_constants(p['ssm_lambda_re'], p['ssm_lambda_im'], p['ssm_log_dt'],
                                     p['ssm_b_re'], p['ssm_b_im'], p['ssm_c_re'], p['ssm_c_im'])
    ya = _s5(ua, bsz, seq, bmat, coef, cmat,
             p['ssm_d'].reshape(1, GROUP_WIDTH).astype(f32), p['ssm_glu_w'].astype(BF16),
             p['ssm_glu_b'].reshape(1, GROUP_WIDTH).astype(f32),
             p['ssm_out_g'].reshape(1, GROUP_WIDTH).astype(f32))

    pw = p['pool_w'].astype(f32)
    w_blk = jnp.zeros((GROUP_WIDTH, GROUP_WIDTH), f32)
    for gi in range(len(POOL_WINDOWS)):
        w_blk = w_blk.at[gi * HEAD_DIM:(gi + 1) * HEAD_DIM, gi * HEAD_DIM:(gi + 1) * HEAD_DIM].set(pw[gi])
    yb = _pool(ub, bsz, seq, w_blk.astype(BF16), p['pool_scale'].reshape(1, GROUP_WIDTH).astype(f32))

    yc = _fox(qa, ka, va, bsz, seq, p['fox_out_g'].reshape(1, GROUP_WIDTH).astype(f32))

    mk, mv = _memkv(mem, p['mem_norm_g'].reshape(1, d).astype(f32), p['mem_w_kv'].astype(BF16),
                    _pad_lanes(p['mem_k_g']))
    ym = _memattn(qm, mk, mv, bsz, seq, p['mem_out_g'].reshape(1, GROUP_WIDTH).astype(f32))

    n_exp = p['router_w'].shape[1]
    tm_out = min(TM_OUT, n)
    ustrict = jnp.triu(jnp.ones((tm_out, tm_out), f32), k=1).astype(BF16)
    rw = p['router_w'].astype(f32)
    rw_hi = rw.astype(BF16)
    rw_lo = (rw - rw_hi.astype(f32)).astype(BF16)
    rw_parts = jnp.pad(jnp.concatenate([rw_hi, rw_lo], axis=1), ((0, 0), (0, LANES - 2 * n_exp)))
    x1, hn, idx, gates, rank, cnt, toff = _outproj(
        x2, ya, yb, yc, ym, p['w_out'].astype(BF16), p['ffn_norm_g'].reshape(1, d).astype(f32),
        rw_parts, p['router_b'].reshape(n_exp, 1).astype(f32), ustrict)

    cap = n * TOP_K + n_exp * TM_EXPERT
    cnt_i = cnt[:, 0].astype(I32)
    dest, meta, starts, win = _plan(cnt_i, idx, rank, toff, cap // TM_EXPERT)
    xs = _dispatch(dest, cnt_i, starts, hn, cap)
    ys = _experts(meta, xs, layer, *experts)
    win_tab = win[::SUBLANES, :n_exp + 1]
    return _combine(win_tab, dest, idx, gates, x1, ys, n_exp)


_PARAM_NAMES = ('mix_norm_g', 'w_in', 'ssm_lambda_re', 'ssm_lambda_im', 'ssm_log_dt',
                'ssm_b_re', 'ssm_b_im', 'ssm_c_re', 'ssm_c_im', 'ssm_d', 'ssm_glu_w', 'ssm_glu_b',
                'ssm_out_g', 'pool_w', 'pool_scale', 'fox_forget_b', 'fox_q_g', 'fox_k_g',
                'fox_out_g', 'mem_norm_g', 'mem_w_kv', 'mem_q_g', 'mem_k_g', 'mem_out_g', 'w_out',
                'ffn_norm_g', 'router_w', 'router_b', 'exp_w_gate', 'exp_b_gate', 'exp_w_up',
                'exp_b_up', 'exp_w_down', 'exp_b_down')


def kernel(x, mem, mix_norm_g, w_in, ssm_lambda_re, ssm_lambda_im, ssm_log_dt, ssm_b_re, ssm_b_im,
           ssm_c_re, ssm_c_im, ssm_d, ssm_glu_w, ssm_glu_b, ssm_out_g, pool_w, pool_scale,
           fox_forget_b, fox_q_g, fox_k_g, fox_out_g, mem_norm_g, mem_w_kv, mem_q_g, mem_k_g,
           mem_out_g, w_out, ffn_norm_g, router_w, router_b, exp_w_gate, exp_b_gate, exp_w_up,
           exp_b_up, exp_w_down, exp_b_down):
    stacked = dict(zip(_PARAM_NAMES, (
        mix_norm_g, w_in, ssm_lambda_re, ssm_lambda_im, ssm_log_dt, ssm_b_re, ssm_b_im, ssm_c_re,
        ssm_c_im, ssm_d, ssm_glu_w, ssm_glu_b, ssm_out_g, pool_w, pool_scale, fox_forget_b,
        fox_q_g, fox_k_g, fox_out_g, mem_norm_g, mem_w_kv, mem_q_g, mem_k_g, mem_out_g, w_out,
        ffn_norm_g, router_w, router_b, exp_w_gate, exp_b_gate, exp_w_up, exp_b_up, exp_w_down,
        exp_b_down)))
    bsz, seq, d = x.shape
    depth = w_in.shape[0]
    x2 = x.reshape(bsz * seq, d).astype(F32)
    mem = mem.astype(F32)
    expert_names = ('exp_w_gate', 'exp_b_gate', 'exp_w_up', 'exp_b_up', 'exp_w_down', 'exp_b_down')
    experts = tuple(stacked[k].astype(F32) if stacked[k].ndim == 4
                    else stacked[k].astype(F32)[:, :, None, :] for k in expert_names)
    for layer in range(depth):
        x2 = _layer(x2, mem, bsz, seq,
                    {k: v[layer] for k, v in stacked.items() if k not in expert_names},
                    layer, experts)
    return x2.reshape(bsz, seq, d).astype(x.dtype)
```

```python
import functools
import math

import jax
import jax.numpy as jnp
from jax import lax
from jax.experimental import pallas as pl
from jax.experimental.pallas import tpu as pltpu

F32 = jnp.float32
BF16 = jnp.bfloat16
I32 = jnp.int32

EPS = 1e-6
HEAD_DIM = 64
N_HEADS = 4
GROUP_WIDTH = 256
LANES = 128
SUBLANES = 8
ROW_CHUNKS = 8
SSM_GROUPS = 16
SSM_CH = 16
SSM_STATE = 64
SSM_LANES = SSM_GROUPS * SSM_STATE
POOL_WINDOWS = (2, 4, 8, 16)
POOL_HALO = 16
TOP_K = 4
SWIGLU_LIMIT = 7.0
SWIGLU_ALPHA = 1.702
VMEM_LIMIT = 56 * 1024 * 1024

TM_INPROJ = 512
T_S5 = 512
T_POOL = 512
T_ATT = 512
TM_OUT = 512
TM_EXPERT = 512
T_MOVE = 256
BF16_ROWS = 16
WINDOW_SHIFT = 7
WINDOW_ROWS = 1 << WINDOW_SHIFT
SEL_CHUNK_WINDOWS = 8
SPILL_WINDOWS = 8
KEY_STRIDE = 1 << 16

NEG_INF = float("-inf")
LOG2E = 1.4426950408889634


def _cparams(sem):
    return pltpu.CompilerParams(dimension_semantics=sem, vmem_limit_bytes=VMEM_LIMIT)


def _lane_iota(shape):
    return lax.broadcasted_iota(I32, shape, len(shape) - 1)


def _split3(x):
    hi = x.astype(BF16).astype(F32)
    r = x - hi
    mid = r.astype(BF16).astype(F32)
    lo = r - mid
    return hi, mid, lo


def _head_slab(slab, odd):
    lane = _lane_iota(slab.shape)
    if odd:
        slab = pltpu.roll(slab, HEAD_DIM, axis=1)
    return jnp.where(lane < HEAD_DIM, slab, 0.0)


def _head_norm(xh, gain):
    ss = jnp.sum(xh * xh, axis=-1, keepdims=True)
    return xh * jnp.broadcast_to(lax.rsqrt(ss * (1.0 / HEAD_DIM) + EPS), xh.shape) * gain


def _divide_by_denominator(acc):
    inv = 1.0 / acc[:, HEAD_DIM:HEAD_DIM + 1]
    return acc * jnp.broadcast_to(inv, acc.shape)


def _join_heads(o_even, o_odd):
    lane = _lane_iota(o_even.shape)
    return jnp.where(lane < HEAD_DIM, o_even, pltpu.roll(o_odd, HEAD_DIM, axis=1))


def _store_token_tiles(ref, val):
    t = val.shape[0]
    for s in range(ROW_CHUNKS):
        ref[pl.ds(s, t, stride=ROW_CHUNKS), :] = val[:, s * LANES:(s + 1) * LANES]


def _load_token_tiles(ref, t):
    return jnp.concatenate([ref[pl.ds(s, t, stride=ROW_CHUNKS), :] for s in range(ROW_CHUNKS)],
                           axis=1)


def _group_norm_pairs(pairs, gain_ref, out_ref):
    ss = jnp.sum(sum(p * p for p in pairs), axis=-1, keepdims=True)
    scale = jnp.broadcast_to(lax.rsqrt(ss * (1.0 / GROUP_WIDTH) + EPS), pairs[0].shape)
    for i, p in enumerate(pairs):
        sl = slice(i * LANES, (i + 1) * LANES)
        out_ref[:, sl] = (p * scale * gain_ref[:, sl]).astype(out_ref.dtype)


COL_A, COL_B, COL_Q, COL_K, COL_V, COL_QM, COL_F = 0, 256, 512, 768, 1024, 1280, 1536
IN_COLS_PADDED = COL_F + LANES


def _inproj_kernel(x_ref, g_ref, w_ref, qg_ref, kg_ref, mqg_ref, fb_ref, tri_ref, sel_ref, ones_ref,
                   ua_ref, ub_ref, qa_ref, ka_ref, va_ref, qm_ref, carry_ref, *, tiles_per_seq):
    i = pl.program_id(0)

    @pl.when(i % tiles_per_seq == 0)
    def _():
        carry_ref[...] = jnp.zeros_like(carry_ref)

    x = x_ref[...]
    h = x * lax.rsqrt(jnp.mean(x * x, axis=-1, keepdims=True) + EPS) * g_ref[...]
    proj = jnp.dot(h.astype(BF16), w_ref[...], preferred_element_type=F32)
    ua_ref[...] = proj[:, COL_A:COL_A + GROUP_WIDTH]
    ub_ref[...] = proj[:, COL_B:COL_B + GROUP_WIDTH]

    z = proj[:, COL_F:COL_F + LANES] + fb_ref[...]
    lane = _lane_iota(z.shape)
    logf = jnp.minimum(z, 0.0) - jnp.log(1.0 + jnp.exp(-jnp.abs(z)))
    hi, mid, lo = _split3(logf)
    packed = jnp.where(lane < 4, hi, jnp.where(lane < 8, mid, jnp.where(lane < 12, lo, 0.0)))
    cs = jnp.dot(tri_ref[...], packed.astype(BF16), preferred_element_type=F32)
    cum = cs + pltpu.roll(cs, LANES - 4, axis=1) + pltpu.roll(cs, LANES - 8, axis=1)
    cum = cum + carry_ref[...]
    carry_ref[...] = cum[cum.shape[0] - 1:, :]

    scale = HEAD_DIM ** -0.5 * LOG2E
    c_parts = jnp.concatenate(_split3(cum * LOG2E), axis=1).astype(BF16)
    bias = jnp.dot(c_parts, sel_ref[...], preferred_element_type=F32)

    def heads_normed(col, gain_ref):
        t = proj[:, col:col + GROUP_WIDTH]
        sq = t * t
        sq_hi = sq.astype(BF16)
        sq_lo = (sq - sq_hi.astype(F32)).astype(BF16)
        ss = jnp.dot(jnp.concatenate([sq_hi, sq_lo], axis=1), ones_ref[...],
                     preferred_element_type=F32)
        return t * lax.rsqrt(ss * (1.0 / HEAD_DIM) + EPS) * gain_ref[...]

    qn_all = heads_normed(COL_Q, qg_ref) * scale
    kn_all = heads_normed(COL_K, kg_ref)
    mn_all = heads_normed(COL_QM, mqg_ref) * scale
    q_bias = (lane >= 64) & (lane < 67)
    k_bias = (lane >= 67) & (lane < 70)
    for hd in range(N_HEADS):
        pair, odd = hd // 2, hd % 2
        pair_sl = slice(pair * LANES, (pair + 1) * LANES)
        b = bias[:, hd * LANES:(hd + 1) * LANES]
        qn = _head_slab(qn_all[:, pair_sl], odd)
        qa_ref[hd] = jnp.where(q_bias, b, jnp.where(k_bias, 1.0, qn)).astype(BF16)
        kn = _head_slab(kn_all[:, pair_sl], odd)
        ka_ref[hd] = jnp.where(k_bias, b, jnp.where(q_bias, 1.0, kn)).astype(BF16)
        vs = _head_slab(proj[:, COL_V + pair * LANES:COL_V + (pair + 1) * LANES], odd)
        va_ref[hd] = jnp.where(lane == 64, 1.0, vs).astype(BF16)
        qm_ref[hd] = _head_slab(mn_all[:, pair_sl], odd).astype(BF16)


def _inproj(x2, seq, mix_g, w_all, qg, kg, mqg, fb, tri, sel, ones):
    n, d = x2.shape
    tm = min(TM_INPROJ, seq)
    grid = (n // tm,)
    const = lambda i: (0, 0)
    row = lambda i: (i, 0)
    hrow = lambda i: (0, i, 0)
    out_shape = (
        jax.ShapeDtypeStruct((n, GROUP_WIDTH), F32),
        jax.ShapeDtypeStruct((n, GROUP_WIDTH), F32),
        jax.ShapeDtypeStruct((N_HEADS, n, LANES), BF16),
        jax.ShapeDtypeStruct((N_HEADS, n, LANES), BF16),
        jax.ShapeDtypeStruct((N_HEADS, n, LANES), BF16),
        jax.ShapeDtypeStruct((N_HEADS, n, LANES), BF16),
    )
    return pl.pallas_call(
        functools.partial(_inproj_kernel, tiles_per_seq=seq // tm),
        grid=grid,
        in_specs=[
            pl.BlockSpec((tm, d), row),
            pl.BlockSpec((1, d), const),
            pl.BlockSpec((d, IN_COLS_PADDED), const),
            pl.BlockSpec((1, GROUP_WIDTH), const),
            pl.BlockSpec((1, GROUP_WIDTH), const),
            pl.BlockSpec((1, GROUP_WIDTH), const),
            pl.BlockSpec((1, LANES), const),
            pl.BlockSpec((tm, tm), const),
            pl.BlockSpec((3 * LANES, N_HEADS * LANES), const),
            pl.BlockSpec((2 * GROUP_WIDTH, GROUP_WIDTH), const),
        ],
        out_specs=(
            pl.BlockSpec((tm, GROUP_WIDTH), row),
            pl.BlockSpec((tm, GROUP_WIDTH), row),
            pl.BlockSpec((N_HEADS, tm, LANES), hrow),
            pl.BlockSpec((N_HEADS, tm, LANES), hrow),
            pl.BlockSpec((N_HEADS, tm, LANES), hrow),
            pl.BlockSpec((N_HEADS, tm, LANES), hrow),
        ),
        out_shape=out_shape,
        scratch_shapes=[pltpu.VMEM((1, LANES), F32)],
        compiler_params=_cparams(("arbitrary",)),
        name="inproj",
    )(x2, mix_g, w_all, qg, kg, mqg, fb, tri, sel, ones)


def _s5_kernel(u_ref, bmat_ref, coef_ref, cmat_ref, d_ref, gluw_ref, glub_ref, og_ref,
               o_ref, x_scr, carry_ref):
    @pl.when(pl.program_id(1) == 0)
    def _():
        carry_ref[...] = jnp.zeros_like(carry_ref)

    u = u_ref[...]
    t = u.shape[0]
    x_scr[...] = jnp.dot(u.astype(BF16), bmat_ref[...], preferred_element_type=F32)
    n_lane_blocks = SSM_LANES // LANES

    def group(gi, carry):
        r0 = pl.multiple_of(gi * SUBLANES, SUBLANES)
        rows = pl.ds(r0, SUBLANES)
        new = []
        for lb in range(n_lane_blocks):
            re_sl = slice(lb * LANES, (lb + 1) * LANES)
            im_sl = slice(SSM_LANES + lb * LANES, SSM_LANES + (lb + 1) * LANES)
            xr = x_scr[rows, re_sl]
            xi = x_scr[rows, im_sl]
            for s, k in enumerate((1, 2, 4)):
                cr = coef_ref[2 * s, :, re_sl]
                ci = coef_ref[2 * s + 1, :, re_sl]
                sr = pltpu.roll(xr, k, axis=0)
                si = pltpu.roll(xi, k, axis=0)
                xr, xi = xr + cr * sr - ci * si, xi + cr * si + ci * sr
            pr = coef_ref[6, :, re_sl]
            pi_ = coef_ref[7, :, re_sl]
            cbr, cbi = carry[2 * lb], carry[2 * lb + 1]
            xr, xi = xr + pr * cbr - pi_ * cbi, xi + pr * cbi + pi_ * cbr
            x_scr[rows, re_sl] = xr
            x_scr[rows, im_sl] = xi
            new.append(jnp.broadcast_to(xr[SUBLANES - 1:, :], xr.shape))
            new.append(jnp.broadcast_to(xi[SUBLANES - 1:, :], xi.shape))
        return tuple(new)

    carry0 = []
    for lb in range(n_lane_blocks):
        carry0.append(carry_ref[:, lb * LANES:(lb + 1) * LANES])
        carry0.append(carry_ref[:, SSM_LANES + lb * LANES:SSM_LANES + (lb + 1) * LANES])
    carry = lax.fori_loop(0, t // SUBLANES, group, tuple(carry0))
    for lb in range(n_lane_blocks):
        carry_ref[:, lb * LANES:(lb + 1) * LANES] = carry[2 * lb]
        carry_ref[:, SSM_LANES + lb * LANES:SSM_LANES + (lb + 1) * LANES] = carry[2 * lb + 1]

    y = jnp.dot(x_scr[...].astype(BF16), cmat_ref[...], preferred_element_type=F32) + d_ref[...] * u
    z = jax.nn.gelu(y, approximate=True)
    gate = jnp.dot(z.astype(BF16), gluw_ref[...], preferred_element_type=F32) + glub_ref[...]
    out = z * (1.0 / (1.0 + jnp.exp(-gate)))
    _group_norm_pairs([out[:, :LANES], out[:, LANES:]], og_ref, o_ref)


def _s5(ua, bsz, seq, bmat, coef, cmat, dvec, gluw, glub, og):
    n = ua.shape[0]
    t = min(T_S5, seq)
    nt = seq // t
    row = lambda b, j: (b * nt + j, 0)
    c2 = lambda b, j: (0, 0)
    c3 = lambda b, j: (0, 0, 0)
    return pl.pallas_call(
        _s5_kernel,
        grid=(bsz, nt),
        in_specs=[
            pl.BlockSpec((t, GROUP_WIDTH), row),
            pl.BlockSpec((GROUP_WIDTH, 2 * SSM_LANES), c2),
            pl.BlockSpec((8, SUBLANES, SSM_LANES), c3),
            pl.BlockSpec((2 * SSM_LANES, GROUP_WIDTH), c2),
            pl.BlockSpec((1, GROUP_WIDTH), c2),
            pl.BlockSpec((GROUP_WIDTH, GROUP_WIDTH), c2),
            pl.BlockSpec((1, GROUP_WIDTH), c2),
            pl.BlockSpec((1, GROUP_WIDTH), c2),
        ],
        out_specs=pl.BlockSpec((t, GROUP_WIDTH), row),
        out_shape=jax.ShapeDtypeStruct((n, GROUP_WIDTH), BF16),
        scratch_shapes=[pltpu.VMEM((t, 2 * SSM_LANES), F32),
                        pltpu.VMEM((SUBLANES, 2 * SSM_LANES), F32)],
        compiler_params=_cparams(("arbitrary", "arbitrary")),
        name="s5",
    )(ua, bmat, coef, cmat, dvec, gluw, glub, og)


def _s5_constants(lam_re, lam_im, log_dt, b_re, b_im, c_re, c_im):
    lr = lam_re.astype(F32)
    li = lam_im.astype(F32)
    dt = jnp.exp(log_dt.astype(F32))[:, None]
    mag = jnp.exp(lr * dt)
    a_re = mag * jnp.cos(li * dt)
    a_im = mag * jnp.sin(li * dt)
    den = lr * lr + li * li
    n_re = a_re - 1.0
    n_im = a_im
    k_re = (n_re * lr + n_im * li) / den
    k_im = (n_im * lr - n_re * li) / den
    br = b_re.astype(F32)
    bi = b_im.astype(F32)
    bb_re = k_re[..., None] * br - k_im[..., None] * bi
    bb_im = k_re[..., None] * bi + k_im[..., None] * br
    eye = jnp.eye(SSM_GROUPS, dtype=F32)
    bm_re = jnp.einsum('gph,gk->ghkp', bb_re, eye).reshape(GROUP_WIDTH, SSM_LANES)
    bm_im = jnp.einsum('gph,gk->ghkp', bb_im, eye).reshape(GROUP_WIDTH, SSM_LANES)
    bmat = jnp.concatenate([bm_re, bm_im], axis=1).astype(BF16)
    cm_re = jnp.einsum('ghp,gk->gpkh', c_re.astype(F32), eye).reshape(SSM_LANES, GROUP_WIDTH)
    cm_im = jnp.einsum('ghp,gk->gpkh', c_im.astype(F32), eye).reshape(SSM_LANES, GROUP_WIDTH)
    cmat = jnp.concatenate([cm_re, -cm_im], axis=0).astype(BF16)

    ar = a_re.reshape(1, SSM_LANES)
    ai = a_im.reshape(1, SSM_LANES)

    def cmul(x, y):
        return x[0] * y[0] - x[1] * y[1], x[0] * y[1] + x[1] * y[0]

    pows = [(ar, ai)]
    for _ in range(SUBLANES - 1):
        pows.append(cmul(pows[-1], (ar, ai)))
    rows = jnp.arange(SUBLANES, dtype=I32)[:, None]
    planes = []
    for k in (1, 2, 4):
        m = (rows >= k).astype(F32)
        planes += [m * pows[k - 1][0], m * pows[k - 1][1]]
    planes += [jnp.concatenate([p[0] for p in pows], axis=0),
               jnp.concatenate([p[1] for p in pows], axis=0)]
    coef = jnp.stack(planes, axis=0)
    return bmat, coef, cmat


def _pool_kernel(v_ref, w_ref, g_ref, o_ref, ext_ref):
    j = pl.program_id(1)
    t = v_ref.shape[0]

    @pl.when(j == 0)
    def _():
        ext_ref[0:POOL_HALO, :] = jnp.zeros((POOL_HALO, GROUP_WIDTH), F32)

    v = v_ref[...]
    ext_ref[POOL_HALO:POOL_HALO + t, :] = v
    cur = ext_ref[...]
    width = 1
    wins = {}
    while width < POOL_WINDOWS[-1]:
        cur = cur + pltpu.roll(cur, width, axis=0)
        width *= 2
        wins[width] = cur[POOL_HALO:, :]
    lane = _lane_iota(v.shape)
    pos = (j * t + lax.broadcasted_iota(I32, v.shape, 0) + 1).astype(F32)
    mean = None
    for gi, w in enumerate(POOL_WINDOWS):
        m = wins[w] / jnp.minimum(pos, float(w))
        mean = m if mean is None else jnp.where(lane >= gi * HEAD_DIM, m, mean)
    mixed = jnp.dot((mean - v).astype(BF16), w_ref[...], preferred_element_type=F32)
    _group_norm_pairs([mixed[:, :LANES], mixed[:, LANES:]], g_ref, o_ref)
    ext_ref[0:POOL_HALO, :] = v[t - POOL_HALO:, :]


def _pool(ub, bsz, seq, w_blk, g):
    n = ub.shape[0]
    t = min(T_POOL, seq)
    nt = seq // t
    row = lambda b, j: (b * nt + j, 0)
    c2 = lambda b, j: (0, 0)
    return pl.pallas_call(
        _pool_kernel,
        grid=(bsz, nt),
        in_specs=[pl.BlockSpec((t, GROUP_WIDTH), row),
                  pl.BlockSpec((GROUP_WIDTH, GROUP_WIDTH), c2),
                  pl.BlockSpec((1, GROUP_WIDTH), c2)],
        out_specs=pl.BlockSpec((t, GROUP_WIDTH), row),
        out_shape=jax.ShapeDtypeStruct((n, GROUP_WIDTH), BF16),
        scratch_shapes=[pltpu.VMEM((t + POOL_HALO, GROUP_WIDTH), F32)],
        compiler_params=_cparams(("arbitrary", "arbitrary")),
        name="pool",
    )(ub, w_blk, g)


def _fox_kernel(qt_ref, kt_ref, qa_ref, ka_ref, va_ref, g_ref, o_ref, m_ref, acc_ref):
    p_id = pl.program_id(1)
    qi = qt_ref[p_id]
    ki = kt_ref[p_id]
    tq = qa_ref.shape[1]
    tk = ka_ref.shape[1]

    @pl.when(ki == 0)
    def _():
        m_ref[...] = jnp.full_like(m_ref, NEG_INF)
        acc_ref[...] = jnp.zeros_like(acc_ref)

    def step(on_diagonal):
        if on_diagonal:
            causal = (lax.broadcasted_iota(I32, (tq, tk), 0)
                      >= lax.broadcasted_iota(I32, (tq, tk), 1))
        def scores(hd):
            return lax.dot_general(qa_ref[hd], ka_ref[hd], (((1,), (1,)), ((), ())),
                                   preferred_element_type=F32)

        s_next = scores(0)
        for hd in range(N_HEADS):
            s = s_next
            if hd + 1 < N_HEADS:
                s_next = scores(hd + 1)
            if on_diagonal:
                s = jnp.where(causal, s, NEG_INF)
            m_prev = m_ref[hd]
            m_new = jnp.maximum(m_prev, jnp.broadcast_to(jnp.max(s, axis=-1, keepdims=True),
                                                         m_prev.shape))
            alpha = jnp.exp2(m_prev - m_new)
            p = jnp.exp2(s - jnp.concatenate([m_new] * (tk // LANES), axis=1))
            acc_ref[hd] = alpha * acc_ref[hd] + jnp.dot(p.astype(BF16), va_ref[hd],
                                                        preferred_element_type=F32)
            m_ref[hd] = m_new

    @pl.when(ki < qi)
    def _():
        step(False)

    @pl.when(ki == qi)
    def _():
        step(True)
        heads = []
        for hd in range(N_HEADS):
            acc = acc_ref[hd]
            heads.append(_divide_by_denominator(acc))
        _group_norm_pairs([_join_heads(heads[0], heads[1]), _join_heads(heads[2], heads[3])],
                          g_ref, o_ref)


def _fox(qa, ka, va, bsz, seq, g):
    n = qa.shape[1]
    t = min(T_ATT, seq)
    nq = seq // t
    pairs = [(q, k) for q in range(nq) for k in range(q + 1)]
    qt = jnp.asarray([p[0] for p in pairs], I32)
    kt = jnp.asarray([p[1] for p in pairs], I32)
    qmap = lambda b, p, qt, kt: (0, b * nq + qt[p], 0)
    kmap = lambda b, p, qt, kt: (0, b * nq + kt[p], 0)
    grid_spec = pltpu.PrefetchScalarGridSpec(
        num_scalar_prefetch=2,
        grid=(bsz, len(pairs)),
        in_specs=[pl.BlockSpec((N_HEADS, t, LANES), qmap),
                  pl.BlockSpec((N_HEADS, t, LANES), kmap),
                  pl.BlockSpec((N_HEADS, t, LANES), kmap),
                  pl.BlockSpec((1, GROUP_WIDTH), lambda b, p, qt, kt: (0, 0))],
        out_specs=pl.BlockSpec((t, GROUP_WIDTH), lambda b, p, qt, kt: (b * nq + qt[p], 0)),
        scratch_shapes=[pltpu.VMEM((N_HEADS, t, LANES), F32),
                        pltpu.VMEM((N_HEADS, t, LANES), F32)],
    )
    return pl.pallas_call(
        _fox_kernel,
        grid_spec=grid_spec,
        out_shape=jax.ShapeDtypeStruct((n, GROUP_WIDTH), BF16),
        compiler_params=_cparams(("arbitrary", "arbitrary")),
        name="fox",
    )(qt, kt, qa, ka, va, g)


def _memkv_kernel(mem_ref, g_ref, w_ref, kg_ref, mk_ref, mv_ref):
    x = mem_ref[0]
    h = x * lax.rsqrt(jnp.mean(x * x, axis=-1, keepdims=True) + EPS) * g_ref[...]
    kv = jnp.dot(h.astype(BF16), w_ref[...], preferred_element_type=F32)
    lane = _lane_iota((x.shape[0], LANES))
    for hd in range(N_HEADS):
        pair, odd = hd // 2, hd % 2
        ks = _head_slab(kv[:, pair * LANES:(pair + 1) * LANES], odd)
        mk_ref[0, hd] = _head_norm(ks, kg_ref[...]).astype(BF16)
        vs = _head_slab(kv[:, GROUP_WIDTH + pair * LANES:GROUP_WIDTH + (pair + 1) * LANES], odd)
        mv_ref[0, hd] = jnp.where(lane == HEAD_DIM, 1.0, vs).astype(BF16)


def _memkv(mem, g, w_kv, kg):
    bsz, m, d = mem.shape
    c2 = lambda b: (0, 0)
    out = jax.ShapeDtypeStruct((bsz, N_HEADS, m, LANES), BF16)
    return pl.pallas_call(
        _memkv_kernel,
        grid=(bsz,),
        in_specs=[pl.BlockSpec((1, m, d), lambda b: (b, 0, 0)),
                  pl.BlockSpec((1, d), c2),
                  pl.BlockSpec((d, 2 * GROUP_WIDTH), c2),
                  pl.BlockSpec((1, LANES), c2)],
        out_specs=(pl.BlockSpec((1, N_HEADS, m, LANES), lambda b: (b, 0, 0, 0)),
                   pl.BlockSpec((1, N_HEADS, m, LANES), lambda b: (b, 0, 0, 0))),
        out_shape=(out, out),
        compiler_params=_cparams(("arbitrary",)),
        name="memkv",
    )(mem, g, w_kv, kg)


def _memattn_kernel(qm_ref, mk_ref, mv_ref, g_ref, o_ref):
    heads = []
    for hd in range(N_HEADS):
        s = lax.dot_general(qm_ref[hd], mk_ref[0, hd], (((1,), (1,)), ((), ())),
                            preferred_element_type=F32)
        m = jnp.broadcast_to(jnp.max(s, axis=-1, keepdims=True), (s.shape[0], LANES))
        p = jnp.exp2(s - jnp.concatenate([m] * (s.shape[1] // LANES), axis=1))
        acc = jnp.dot(p.astype(BF16), mv_ref[0, hd], preferred_element_type=F32)
        heads.append(_divide_by_denominator(acc))
    _group_norm_pairs([_join_heads(heads[0], heads[1]), _join_heads(heads[2], heads[3])],
                      g_ref, o_ref)


def _memattn(qm, mk, mv, bsz, seq, g):
    n = qm.shape[1]
    m = mk.shape[2]
    t = min(T_ATT, seq)
    nt = seq // t
    return pl.pallas_call(
        _memattn_kernel,
        grid=(bsz, nt),
        in_specs=[pl.BlockSpec((N_HEADS, t, LANES), lambda b, j: (0, b * nt + j, 0)),
                  pl.BlockSpec((1, N_HEADS, m, LANES), lambda b, j: (b, 0, 0, 0)),
                  pl.BlockSpec((1, N_HEADS, m, LANES), lambda b, j: (b, 0, 0, 0)),
                  pl.BlockSpec((1, GROUP_WIDTH), lambda b, j: (0, 0))],
        out_specs=pl.BlockSpec((t, GROUP_WIDTH), lambda b, j: (b * nt + j, 0)),
        out_shape=jax.ShapeDtypeStruct((n, GROUP_WIDTH), BF16),
        compiler_params=_cparams(("arbitrary", "arbitrary")),
        name="memattn",
    )(qm, mk, mv, g)


def _outproj_kernel(x_ref, ya_ref, yb_ref, yc_ref, ym_ref, w_ref, fg_ref, rw_ref, rb_ref, us_ref,
                    x1_ref, hn_ref, idx_ref, gate_ref, rank_ref, cnt_ref, toff_ref, carry_ref):
    @pl.when(pl.program_id(0) == 0)
    def _():
        carry_ref[...] = jnp.zeros_like(carry_ref)

    merged = jnp.concatenate([ya_ref[...], yb_ref[...], yc_ref[...], ym_ref[...]], axis=1)
    acc = x_ref[...] + jnp.dot(merged, w_ref[...], preferred_element_type=F32)
    x1_ref[...] = acc
    hn = acc * lax.rsqrt(jnp.mean(acc * acc, axis=-1, keepdims=True) + EPS) * fg_ref[...]
    _store_token_tiles(hn_ref, hn)
    n_exp = rb_ref.shape[0]
    tm = hn.shape[0]
    hn_hi = hn.astype(BF16)
    hn_lo = (hn - hn_hi.astype(F32)).astype(BF16)
    parts = jnp.dot(jnp.concatenate([hn_hi, hn_lo], axis=0), rw_ref[...],
                    preferred_element_type=F32)
    top = jnp.transpose(parts[:tm])
    bot = jnp.transpose(parts[tm:])
    logits = (top[:n_exp] + top[n_exp:2 * n_exp] + bot[:n_exp] + bot[n_exp:2 * n_exp]
              + rb_ref[...])
    e_iota = lax.broadcasted_iota(I32, (n_exp, tm), 0).astype(F32)
    work = logits
    vals, onehots = [], []
    for k in range(TOP_K):
        m = jnp.max(work, axis=0, keepdims=True)
        sel = jnp.min(jnp.where(work == m, e_iota, float(n_exp)), axis=0, keepdims=True)
        hot = e_iota == sel
        idx_ref[k:k + 1, :] = sel.astype(I32)
        vals.append(m)
        onehots.append(hot.astype(F32))
        work = jnp.where(hot, NEG_INF, work)
    exps = [jnp.exp(v - vals[0]) for v in vals]
    denom = exps[0] + exps[1] + exps[2] + exps[3]
    for k in range(TOP_K):
        gate_ref[k:k + 1, :] = exps[k] / denom
    stacked = jnp.concatenate(onehots, axis=0).astype(BF16)
    prefix = jnp.dot(stacked, us_ref[...], preferred_element_type=F32)
    base = carry_ref[...]
    base_sq = jnp.concatenate([base, jnp.zeros((LANES - n_exp, LANES), F32)], axis=0)
    toff_ref[...] = jnp.transpose(base_sq)[:SUBLANES, :]
    for k in range(TOP_K):
        hot = onehots[k]
        pk = prefix[k * n_exp:(k + 1) * n_exp, :]
        rank = jnp.sum(hot * (pk + base[:, 0:1]), axis=0, keepdims=True)
        rank_ref[k:k + 1, :] = rank.astype(I32)
        base = base + jnp.sum(hot, axis=1, keepdims=True)
    carry_ref[...] = base
    cnt_ref[...] = base


def _outproj(x2, ya, yb, yc, ym, w_out, fg, rw, rb, ustrict):
    n, d = x2.shape
    n_exp = rb.shape[0]
    tm = min(TM_OUT, n)
    row = lambda i: (i, 0)
    col = lambda i: (0, i)
    const = lambda i: (0, 0)
    return pl.pallas_call(
        _outproj_kernel,
        grid=(n // tm,),
        in_specs=[pl.BlockSpec((tm, d), row)]
        + [pl.BlockSpec((tm, GROUP_WIDTH), row)] * 4
        + [pl.BlockSpec((d, d), const),
           pl.BlockSpec((1, d), const),
           pl.BlockSpec((d, LANES), const),
           pl.BlockSpec((n_exp, 1), const),
           pl.BlockSpec((tm, tm), const)],
        out_specs=(pl.BlockSpec((tm, d), row),
                   pl.BlockSpec((tm * ROW_CHUNKS, LANES), row),
                   pl.BlockSpec((TOP_K, tm), col),
                   pl.BlockSpec((TOP_K, tm), col),
                   pl.BlockSpec((TOP_K, tm), col),
                   pl.BlockSpec((n_exp, LANES), const),
                   pl.BlockSpec((SUBLANES, LANES), row)),
        out_shape=(jax.ShapeDtypeStruct((n, d), F32),
                   jax.ShapeDtypeStruct((n * ROW_CHUNKS, LANES), F32),
                   jax.ShapeDtypeStruct((TOP_K, n), I32),
                   jax.ShapeDtypeStruct((TOP_K, n), F32),
                   jax.ShapeDtypeStruct((TOP_K, n), I32),
                   jax.ShapeDtypeStruct((n_exp, LANES), F32),
                   jax.ShapeDtypeStruct((n // tm * SUBLANES, LANES), F32)),
        scratch_shapes=[pltpu.VMEM((n_exp, LANES), F32)],
        compiler_params=_cparams(("arbitrary",)),
        name="outproj",
    )(x2, ya, yb, yc, ym, w_out, fg, rw, rb, ustrict)


def _plan_kernel(cnt_ref, idx_ref, rank_ref, toff_ref, dest_ref, meta_ref, start_ref, win_ref,
                 *, n_exp, block_rows):
    def body(e, off):
        start_ref[e] = off
        return off + (cnt_ref[e] + block_rows - 1) // block_rows * block_rows

    total = lax.fori_loop(0, n_exp, body, jnp.int32(0))
    idx = idx_ref[...]
    dest = rank_ref[...]
    blk_start = lax.broadcasted_iota(I32, meta_ref.shape, 1) * block_rows
    blk_e = jnp.zeros(meta_ref.shape, I32)
    lane = _lane_iota((SUBLANES, LANES))
    start_l = jnp.zeros((SUBLANES, LANES), I32)
    cnt_l = jnp.zeros((SUBLANES, LANES), I32)
    for e in range(n_exp):
        dest = dest + jnp.where(idx == e, start_ref[e], 0)
        start_l = jnp.where(lane == e, start_ref[e], start_l)
        cnt_l = jnp.where(lane == e, cnt_ref[e], cnt_l)
        if e > 0:
            blk_e = blk_e + (blk_start >= start_ref[e]).astype(I32)
    dest_ref[...] = dest
    sub = lax.broadcasted_iota(I32, meta_ref.shape, 0)
    meta_ref[...] = jnp.where(sub == 0, blk_e, total // block_rows)

    toff = toff_ref[...].astype(I32)
    n_rows = toff.shape[0]
    nxt = jnp.concatenate([toff[SUBLANES:], cnt_l], axis=0) if n_rows > SUBLANES else cnt_l
    first = jnp.concatenate([start_l] * (n_rows // SUBLANES), axis=0) + toff
    aligned = jnp.bitwise_and(first, -BF16_ROWS)
    lanes = _lane_iota(toff.shape)
    span = jnp.where(lanes < n_exp, first - aligned + (nxt - toff), 0)
    n_win = jnp.right_shift(span + (WINDOW_ROWS - 1), WINDOW_SHIFT)
    win_ref[...] = jnp.where(lanes < n_exp, aligned, pltpu.roll(n_win, n_exp, axis=1))


def _plan(cnt, idx, rank, toff, n_blocks):
    n_exp = cnt.shape[0]
    nb_pad = (n_blocks + LANES - 1) // LANES * LANES
    return pl.pallas_call(
        functools.partial(_plan_kernel, n_exp=n_exp, block_rows=TM_EXPERT),
        in_specs=[pl.BlockSpec(memory_space=pltpu.SMEM),
                  pl.BlockSpec(memory_space=pltpu.VMEM),
                  pl.BlockSpec(memory_space=pltpu.VMEM),
                  pl.BlockSpec(memory_space=pltpu.VMEM)],
        out_specs=(pl.BlockSpec(memory_space=pltpu.VMEM),
                   pl.BlockSpec(memory_space=pltpu.VMEM),
                   pl.BlockSpec(memory_space=pltpu.SMEM),
                   pl.BlockSpec(memory_space=pltpu.VMEM)),
        out_shape=(jax.ShapeDtypeStruct(idx.shape, I32),
                   jax.ShapeDtypeStruct((SUBLANES, nb_pad), I32),
                   jax.ShapeDtypeStruct((n_exp,), I32),
                   jax.ShapeDtypeStruct(toff.shape, I32)),
        compiler_params=pltpu.CompilerParams(vmem_limit_bytes=VMEM_LIMIT),
        name="plan",
    )(cnt, idx, rank, toff)


def _tile_rows(row):
    return pl.ds(pl.multiple_of(row * ROW_CHUNKS, ROW_CHUNKS), ROW_CHUNKS)


def _dispatch_kernel(dest_ref, cnt_ref, start_ref, hn_ref, xs_ref, zero_ref, sem, zsem,
                     *, n_exp, block_rows):
    tm = dest_ref.shape[1]

    def pad_copy(e, r):
        return pltpu.make_async_copy(zero_ref.at[pl.ds(0, ROW_CHUNKS)],
                                     xs_ref.at[_tile_rows(start_ref[e] + r)], zsem)

    def pad_bounds(e):
        cnt = cnt_ref[e]
        return cnt, (cnt + block_rows - 1) // block_rows * block_rows

    @pl.when(pl.program_id(0) == 0)
    def _():
        zero_ref[...] = jnp.zeros_like(zero_ref)

        def pad_start(e, c):
            lo, hi = pad_bounds(e)
            return lax.fori_loop(lo, hi, lambda r, c: (pad_copy(e, r).start(), c)[1], c)

        def pad_wait(e, c):
            lo, hi = pad_bounds(e)
            return lax.fori_loop(lo, hi, lambda r, c: (pad_copy(e, r).wait(), c)[1], c)

        lax.fori_loop(0, n_exp, pad_start, 0)
        lax.fori_loop(0, n_exp, pad_wait, 0)

        last = n_exp - 1
        used = start_ref[last] + pad_bounds(last)[1]
        n_blocks = xs_ref.shape[0] // (ROW_CHUNKS * block_rows)

        def tail_copy(b):
            rows = pl.ds(pl.multiple_of(b * (block_rows * ROW_CHUNKS), block_rows * ROW_CHUNKS),
                         block_rows * ROW_CHUNKS)
            return pltpu.make_async_copy(zero_ref, xs_ref.at[rows], zsem)

        lax.fori_loop(used // block_rows, n_blocks, lambda b, c: (tail_copy(b).start(), c)[1], 0)
        lax.fori_loop(used // block_rows, n_blocks, lambda b, c: (tail_copy(b).wait(), c)[1], 0)

    def copy(k, t):
        return pltpu.make_async_copy(hn_ref.at[_tile_rows(t)],
                                     xs_ref.at[_tile_rows(dest_ref[k, t])], sem)

    def start(t, c):
        for k in range(TOP_K):
            copy(k, t).start(priority=k % 2)
        return c

    def wait(t, c):
        for k in range(TOP_K):
            copy(k, t).wait()
        return c

    lax.fori_loop(0, tm, start, 0)
    lax.fori_loop(0, tm, wait, 0)


def _dispatch(dest, cnt, starts, hn, cap):
    n = hn.shape[0] // ROW_CHUNKS
    tm = min(T_MOVE, n)
    return pl.pallas_call(
        functools.partial(_dispatch_kernel, n_exp=cnt.shape[0], block_rows=TM_EXPERT),
        grid=(n // tm,),
        in_specs=[pl.BlockSpec((TOP_K, tm), lambda i: (0, i), memory_space=pltpu.SMEM),
                  pl.BlockSpec(memory_space=pltpu.SMEM),
                  pl.BlockSpec(memory_space=pltpu.SMEM),
                  pl.BlockSpec((tm * ROW_CHUNKS, LANES), lambda i: (i, 0))],
        out_specs=pl.BlockSpec(memory_space=pl.ANY),
        out_shape=jax.ShapeDtypeStruct((cap * ROW_CHUNKS, LANES), hn.dtype),
        scratch_shapes=[pltpu.VMEM((TM_EXPERT * ROW_CHUNKS, LANES), hn.dtype),
                        pltpu.SemaphoreType.DMA, pltpu.SemaphoreType.DMA],
        compiler_params=_cparams(("arbitrary",)),
        name="dispatch",
    )(dest, cnt, starts, hn)


def _experts_kernel(meta_ref, xs_ref, wg_ref, bg_ref, wu_ref, bu_ref, wd_ref, bd_ref, ys_ref,
                    wg_s, wu_s, wd_s):
    i = pl.program_id(0)
    tm = xs_ref.shape[0] // ROW_CHUNKS
    in_use = i < meta_ref[1, 0]
    new_expert = jnp.logical_or(i == 0, meta_ref[0, i] != meta_ref[0, jnp.maximum(i - 1, 0)])

    @pl.when(jnp.logical_not(in_use))
    def _():
        ys_ref[...] = jnp.zeros_like(ys_ref)

    @pl.when(jnp.logical_and(in_use, new_expert))
    def _():
        wg_s[...] = wg_ref[0, 0].astype(BF16)
        wu_s[...] = wu_ref[0, 0].astype(BF16)
        wd_s[...] = wd_ref[0, 0].astype(BF16)

    @pl.when(in_use)
    def _():
        x = _load_token_tiles(xs_ref, tm).astype(BF16)
        gl = jnp.dot(x, wg_s[...], preferred_element_type=F32) + bg_ref[0, 0]
        up = jnp.dot(x, wu_s[...], preferred_element_type=F32) + bu_ref[0, 0]
        gl = jnp.minimum(gl, SWIGLU_LIMIT)
        up = jnp.clip(up, -SWIGLU_LIMIT, SWIGLU_LIMIT)
        act = gl * (1.0 / (1.0 + jnp.exp(-SWIGLU_ALPHA * gl)))
        hmid = ((up + 1.0) * act).astype(BF16)
        y = jnp.dot(hmid, wd_s[...], preferred_element_type=F32) + bd_ref[0, 0]
        ys_ref[...] = y.astype(ys_ref.dtype)


def _experts(meta, xs, layer, wg, bg, wu, bu, wd, bd):
    cap = xs.shape[0] // ROW_CHUNKS
    _, _, d, de = wg.shape
    tm = TM_EXPERT
    n_blocks = cap // tm

    def blk(i, meta):
        return jnp.minimum(i, meta[1, 0] - 1)

    rows = lambda i, meta: (blk(i, meta), 0)
    wmap = lambda i, meta: (layer, meta[0, blk(i, meta)], 0, 0)
    grid_spec = pltpu.PrefetchScalarGridSpec(
        num_scalar_prefetch=1,
        grid=(n_blocks,),
        in_specs=[pl.BlockSpec((tm * ROW_CHUNKS, LANES), rows),
                  pl.BlockSpec((1, 1, d, de), wmap),
                  pl.BlockSpec((1, 1, 1, de), wmap),
                  pl.BlockSpec((1, 1, d, de), wmap),
                  pl.BlockSpec((1, 1, 1, de), wmap),
                  pl.BlockSpec((1, 1, de, d), wmap),
                  pl.BlockSpec((1, 1, 1, d), wmap)],
        out_specs=pl.BlockSpec((tm, d), lambda i, meta: (i, 0)),
        scratch_shapes=[pltpu.VMEM((d, de), BF16), pltpu.VMEM((d, de), BF16),
                        pltpu.VMEM((de, d), BF16)],
    )
    return pl.pallas_call(
        _experts_kernel,
        grid_spec=grid_spec,
        out_shape=jax.ShapeDtypeStruct((cap, d), BF16),
        compiler_params=_cparams(("arbitrary",)),
        name="experts",
    )(meta, xs, wg, bg, wu, bu, wd, bd)


def _combine_kernel(win_ref, dest_ref, idx_ref, gate_ref, x1_ref, ys_ref, o_ref, buf_ref, one_ref,
                    sem, one_sem, *, n_exp):
    i = pl.program_id(0)
    n_tiles = pl.num_programs(0)
    tm = x1_ref.shape[0]
    last_window = ys_ref.shape[0] - WINDOW_ROWS

    def window_start(tile, e, rnd):
        return jnp.minimum(win_ref[tile, e] + rnd * WINDOW_ROWS, last_window)

    def ys_window(tile, e, rnd):
        return ys_ref.at[pl.ds(pl.multiple_of(window_start(tile, e, rnd), BF16_ROWS), WINDOW_ROWS)]

    def buf_window(slot, w):
        return buf_ref.at[slot, pl.ds(pl.multiple_of(w * WINDOW_ROWS, WINDOW_ROWS), WINDOW_ROWS)]

    def for_spills(tile, fn):
        def body(e, j):
            hit = jnp.logical_and(win_ref[tile, n_exp + e] > 1, j < SPILL_WINDOWS)

            @pl.when(hit)
            def _():
                fn(e, j)

            return j + hit.astype(I32)

        lax.fori_loop(0, n_exp, body, jnp.int32(0))

    def fetch(tile, slot):
        for e in range(n_exp):
            pltpu.make_async_copy(ys_window(tile, e, 0), buf_window(slot, e),
                                  sem.at[slot]).start(priority=e % 2)
        for_spills(tile, lambda e, j: pltpu.make_async_copy(
            ys_window(tile, e, 1), buf_window(slot, n_exp + j), sem.at[slot]).start())

    def drain(tile, slot):
        for e in range(n_exp):
            pltpu.make_async_copy(ys_window(tile, e, 0), buf_window(slot, e), sem.at[slot]).wait()
        for_spills(tile, lambda e, j: pltpu.make_async_copy(
            ys_window(tile, e, 1), buf_window(slot, n_exp + j), sem.at[slot]).wait())

    @pl.when(i == 0)
    def _():
        fetch(0, 0)

    @pl.when(i + 1 < n_tiles)
    def _():
        fetch(i + 1, (i + 1) % 2)

    idx = idx_ref[...]
    dest = dest_ref[...]
    gates = gate_ref[...]
    chunk = SEL_CHUNK_WINDOWS * WINDOW_ROWS
    lane = _lane_iota((tm, chunk)).astype(F32)

    first_tok = jnp.zeros(idx.shape, I32)
    clamped_tok = jnp.zeros(idx.shape, I32)
    for e in range(n_exp):
        first_tok = jnp.where(idx == e, win_ref[i, e], first_tok)
        clamped_tok = jnp.where(idx == e, window_start(i, e, 0), clamped_tok)
    local = dest - clamped_tok
    col = jnp.where(dest - first_tok < WINDOW_ROWS, idx * WINDOW_ROWS + local, -1)
    key = idx * KEY_STRIDE + (dest - first_tok)
    packed = jnp.concatenate([col.astype(F32), gates, key.astype(F32),
                              jnp.zeros((LANES - 3 * TOP_K, tm), F32)], axis=0)
    by_tok = jnp.transpose(packed)

    def spread(j, width):
        one = jnp.broadcast_to(by_tok[:, j:j + 1], (tm, LANES))
        return jnp.concatenate([one] * (width // LANES), axis=1) if width > LANES else one

    slot = i % 2
    drain(i, slot)
    cols = [spread(k, chunk) for k in range(TOP_K)]
    gts = [spread(TOP_K + k, chunk) for k in range(TOP_K)]
    total = x1_ref[...]
    for c in range(n_exp // SEL_CHUNK_WINDOWS):
        sel = jnp.zeros((tm, chunk), F32)
        for k in range(TOP_K):
            sel = jnp.where(lane == cols[k] - float(c * chunk), gts[k], sel)
        total = total + jnp.dot(sel.astype(BF16), buf_ref[slot, c * chunk:(c + 1) * chunk, :],
                                preferred_element_type=F32)
    o_ref[...] = total

    lane1 = _lane_iota((tm, LANES))
    keys1 = [spread(2 * TOP_K + k, LANES) for k in range(TOP_K)]
    gts1 = [spread(TOP_K + k, LANES) for k in range(TOP_K)]

    def add_window(e, rnd, rows):
        shift = win_ref[i, e] + rnd * WINDOW_ROWS - window_start(i, e, rnd)
        want = jnp.where(lane1 >= shift, lane1 + (e * KEY_STRIDE + rnd * WINDOW_ROWS - shift),
                         -1).astype(F32)
        sel = jnp.zeros((tm, LANES), F32)
        for k in range(TOP_K):
            sel = jnp.where(want == keys1[k], gts1[k], sel)
        o_ref[...] += jnp.dot(sel.astype(BF16), rows, preferred_element_type=F32)

    for_spills(i, lambda e, j: add_window(e, 1, buf_window(slot, n_exp + j)[...]))

    def on_demand(e, j):
        n_win = win_ref[i, n_exp + e]
        hit = n_win > 1
        ahead = jnp.logical_and(hit, j < SPILL_WINDOWS)

        def one(rnd, c):
            cp = pltpu.make_async_copy(ys_window(i, e, rnd), one_ref, one_sem)
            cp.start()
            cp.wait()
            add_window(e, rnd, one_ref[...])
            return c

        lax.fori_loop(jnp.where(ahead, 2, 1), n_win, one, 0)
        return j + hit.astype(I32)

    lax.fori_loop(0, n_exp, on_demand, jnp.int32(0))


def _combine(win, dest, idx, gates, x1, ys, n_exp):
    n, d = x1.shape
    tm = min(TM_OUT, n)
    vec = lambda i, win: (0, i)
    grid_spec = pltpu.PrefetchScalarGridSpec(
        num_scalar_prefetch=1,
        grid=(n // tm,),
        in_specs=[pl.BlockSpec((TOP_K, tm), vec),
                  pl.BlockSpec((TOP_K, tm), vec),
                  pl.BlockSpec((TOP_K, tm), vec),
                  pl.BlockSpec((tm, d), lambda i, win: (i, 0)),
                  pl.BlockSpec(memory_space=pl.ANY)],
        out_specs=pl.BlockSpec((tm, d), lambda i, win: (i, 0)),
        scratch_shapes=[pltpu.VMEM((2, (n_exp + SPILL_WINDOWS) * WINDOW_ROWS, d), ys.dtype),
                        pltpu.VMEM((WINDOW_ROWS, d), ys.dtype),
                        pltpu.SemaphoreType.DMA((2,)),
                        pltpu.SemaphoreType.DMA],
    )
    return pl.pallas_call(
        functools.partial(_combine_kernel, n_exp=n_exp),
        grid_spec=grid_spec,
        out_shape=jax.ShapeDtypeStruct((n, d), F32),
        compiler_params=_cparams(("arbitrary",)),
        name="combine",
    )(win, dest, idx, gates, x1, ys)


def _pad_lanes(v, width=LANES):
    v = v.astype(F32).reshape(1, -1)
    return jnp.pad(v, ((0, 0), (0, width - v.shape[1])))


def _layer(x2, mem, bsz, seq, p, layer, experts):
    n, d = x2.shape
    f32 = F32
    w_in = p['w_in']
    off_q = 2 * GROUP_WIDTH
    off_k, off_v, off_f = off_q + GROUP_WIDTH, off_q + 2 * GROUP_WIDTH, off_q + 3 * GROUP_WIDTH
    off_qm = off_f + N_HEADS
    w_f = w_in[:, off_f:off_qm]
    w_f_pad = jnp.pad(jnp.concatenate([w_f, w_f, w_f], axis=1), ((0, 0), (0, LANES - 3 * N_HEADS)))
    w_all = jnp.concatenate([w_in[:, :off_f], w_in[:, off_qm:], w_f_pad], axis=1).astype(BF16)
    fb = p['fox_forget_b'].astype(f32)
    fb_pad = _pad_lanes(jnp.concatenate([fb, fb, fb]))
    tm_in = min(TM_INPROJ, seq)
    tri = jnp.tril(jnp.ones((tm_in, tm_in), f32)).astype(BF16)

    sel = jnp.zeros((3, LANES, N_HEADS, LANES), f32)
    for part in range(3):
        for hd in range(N_HEADS):
            sel = sel.at[part, hd, hd, HEAD_DIM + part].set(1.0)
            sel = sel.at[part, hd, hd, HEAD_DIM + 3 + part].set(-1.0)
    sel = sel.reshape(3 * LANES, N_HEADS * LANES).astype(BF16)
    head_of = jnp.arange(GROUP_WIDTH) // HEAD_DIM
    ones_blk = (head_of[:, None] == head_of[None, :]).astype(BF16)
    ones2 = jnp.concatenate([ones_blk, ones_blk], axis=0)

    def head_gain(g):
        return jnp.tile(g.astype(f32).reshape(1, HEAD_DIM), (1, N_HEADS))

    ua, ub, qa, ka, va, qm = _inproj(
        x2, seq, p['mix_norm_g'].reshape(1, d).astype(f32), w_all,
        head_gain(p['fox_q_g']), head_gain(p['fox_k_g']), head_gain(p['mem_q_g']), fb_pad, tri,
        sel, ones2)

    bmat, coef, cmat = _s5_constants(p['ssm_lambda_re'], p['ssm_lambda_im'], p['ssm_log_dt'],
                                     p['ssm_b_re'], p['ssm_b_im'], p['ssm_c_re'], p['ssm_c_im'])
    ya = _s5(ua, bsz, seq, bmat, coef, cmat,
             p['ssm_d'].reshape(1, GROUP_WIDTH).astype(f32), p['ssm_glu_w'].astype(BF16),
             p['ssm_glu_b'].reshape(1, GROUP_WIDTH).astype(f32),
             p['ssm_out_g'].reshape(1, GROUP_WIDTH).astype(f32))

    pw = p['pool_w'].astype(f32)
    w_blk = jnp.zeros((GROUP_WIDTH, GROUP_WIDTH), f32)
    for gi in range(len(POOL_WINDOWS)):
        w_blk = w_blk.at[gi * HEAD_DIM:(gi + 1) * HEAD_DIM, gi * HEAD_DIM:(gi + 1) * HEAD_DIM].set(pw[gi])
    yb = _pool(ub, bsz, seq, w_blk.astype(BF16), p['pool_scale'].reshape(1, GROUP_WIDTH).astype(f32))

    yc = _fox(qa, ka, va, bsz, seq, p['fox_out_g'].reshape(1, GROUP_WIDTH).astype(f32))

    mk, mv = _memkv(mem, p['mem_norm_g'].reshape(1, d).astype(f32), p['mem_w_kv'].astype(BF16),
                    _pad_lanes(p['mem_k_g']))
    ym = _memattn(qm, mk, mv, bsz, seq, p['mem_out_g'].reshape(1, GROUP_WIDTH).astype(f32))

    n_exp = p['router_w'].shape[1]
    tm_out = min(TM_OUT, n)
    ustrict = jnp.triu(jnp.ones((tm_out, tm_out), f32), k=1).astype(BF16)
    rw = p['router_w'].astype(f32)
    rw_hi = rw.astype(BF16)
    rw_lo = (rw - rw_hi.astype(f32)).astype(BF16)
    rw_parts = jnp.pad(jnp.concatenate([rw_hi, rw_lo], axis=1), ((0, 0), (0, LANES - 2 * n_exp)))
    x1, hn, idx, gates, rank, cnt, toff = _outproj(
        x2, ya, yb, yc, ym, p['w_out'].astype(BF16), p['ffn_norm_g'].reshape(1, d).astype(f32),
        rw_parts, p['router_b'].reshape(n_exp, 1).astype(f32), ustrict)

    cap = n * TOP_K + n_exp * TM_EXPERT
    cnt_i = cnt[:, 0].astype(I32)
    dest, meta, starts, win = _plan(cnt_i, idx, rank, toff, cap // TM_EXPERT)
    xs = _dispatch(dest, cnt_i, starts, hn, cap)
    ys = _experts(meta, xs, layer, *experts)
    win_tab = win[::SUBLANES, :2 * n_exp]
    return _combine(win_tab, dest, idx, gates, x1, ys, n_exp)


_PARAM_NAMES = ('mix_norm_g', 'w_in', 'ssm_lambda_re', 'ssm_lambda_im', 'ssm_log_dt',
                'ssm_b_re', 'ssm_b_im', 'ssm_c_re', 'ssm_c_im', 'ssm_d', 'ssm_glu_w', 'ssm_glu_b',
                'ssm_out_g', 'pool_w', 'pool_scale', 'fox_forget_b', 'fox_q_g', 'fox_k_g',
                'fox_out_g', 'mem_norm_g', 'mem_w_kv', 'mem_q_g', 'mem_k_g', 'mem_out_g', 'w_out',
                'ffn_norm_g', 'router_w', 'router_b', 'exp_w_gate', 'exp_b_gate', 'exp_w_up',
                'exp_b_up', 'exp_w_down', 'exp_b_down')


def kernel(x, mem, mix_norm_g, w_in, ssm_lambda_re, ssm_lambda_im, ssm_log_dt, ssm_b_re, ssm_b_im,
           ssm_c_re, ssm_c_im, ssm_d, ssm_glu_w, ssm_glu_b, ssm_out_g, pool_w, pool_scale,
           fox_forget_b, fox_q_g, fox_k_g, fox_out_g, mem_norm_g, mem_w_kv, mem_q_g, mem_k_g,
           mem_out_g, w_out, ffn_norm_g, router_w, router_b, exp_w_gate, exp_b_gate, exp_w_up,
           exp_b_up, exp_w_down, exp_b_down):
    stacked = dict(zip(_PARAM_NAMES, (
        mix_norm_g, w_in, ssm_lambda_re, ssm_lambda_im, ssm_log_dt, ssm_b_re, ssm_b_im, ssm_c_re,
        ssm_c_im, ssm_d, ssm_glu_w, ssm_glu_b, ssm_out_g, pool_w, pool_scale, fox_forget_b,
        fox_q_g, fox_k_g, fox_out_g, mem_norm_g, mem_w_kv, mem_q_g, mem_k_g, mem_out_g, w_out,
        ffn_norm_g, router_w, router_b, exp_w_gate, exp_b_gate, exp_w_up, exp_b_up, exp_w_down,
        exp_b_down)))
    bsz, seq, d = x.shape
    depth = w_in.shape[0]
    x2 = x.reshape(bsz * seq, d).astype(F32)
    mem = mem.astype(F32)
    expert_names = ('exp_w_gate', 'exp_b_gate', 'exp_w_up', 'exp_b_up', 'exp_w_down', 'exp_b_down')
    experts = tuple(stacked[k].astype(F32) if stacked[k].ndim == 4
                    else stacked[k].astype(F32)[:, :, None, :] for k in expert_names)
    for layer in range(depth):
        x2 = _layer(x2, mem, bsz, seq,
                    {k: v[layer] for k, v in stacked.items() if k not in expert_names},
                    layer, experts)
    return x2.reshape(bsz, seq, d).astype(x.dtype)
```

```python
import functools
import math

import jax
import jax.numpy as jnp
from jax import lax
from jax.experimental import pallas as pl
from jax.experimental.pallas import tpu as pltpu

F32 = jnp.float32
BF16 = jnp.bfloat16
I32 = jnp.int32

EPS = 1e-6
HEAD_DIM = 64
N_HEADS = 4
GROUP_WIDTH = 256
LANES = 128
SUBLANES = 8
ROW_CHUNKS = 8
SSM_GROUPS = 16
SSM_CH = 16
SSM_STATE = 64
SSM_LANES = SSM_GROUPS * SSM_STATE
POOL_WINDOWS = (2, 4, 8, 16)
POOL_HALO = 16
TOP_K = 4
SWIGLU_LIMIT = 7.0
SWIGLU_ALPHA = 1.702
VMEM_LIMIT = 56 * 1024 * 1024

TM_INPROJ = 512
T_S5 = 512
T_POOL = 512
T_ATT = 512
TM_OUT = 512
TM_EXPERT = 512
T_MOVE = 256
BF16_ROWS = 16
WINDOW_SHIFT = 7
WINDOW_ROWS = 1 << WINDOW_SHIFT
SEL_CHUNK_WINDOWS = 8
SPILL_WINDOWS = 8
KEY_STRIDE = 1 << 16

NEG_INF = float("-inf")
LOG2E = 1.4426950408889634


def _cparams(sem):
    return pltpu.CompilerParams(dimension_semantics=sem, vmem_limit_bytes=VMEM_LIMIT)


def _lane_iota(shape):
    return lax.broadcasted_iota(I32, shape, len(shape) - 1)


def _split3(x):
    hi = x.astype(BF16).astype(F32)
    r = x - hi
    mid = r.astype(BF16).astype(F32)
    lo = r - mid
    return hi, mid, lo


def _head_slab(slab, odd):
    lane = _lane_iota(slab.shape)
    if odd:
        slab = pltpu.roll(slab, HEAD_DIM, axis=1)
    return jnp.where(lane < HEAD_DIM, slab, 0.0)


def _head_norm(xh, gain):
    ss = jnp.sum(xh * xh, axis=-1, keepdims=True)
    return xh * jnp.broadcast_to(lax.rsqrt(ss * (1.0 / HEAD_DIM) + EPS), xh.shape) * gain


def _divide_by_denominator(acc):
    inv = 1.0 / acc[:, HEAD_DIM:HEAD_DIM + 1]
    return acc * jnp.broadcast_to(inv, acc.shape)


def _join_heads(o_even, o_odd):
    lane = _lane_iota(o_even.shape)
    return jnp.where(lane < HEAD_DIM, o_even, pltpu.roll(o_odd, HEAD_DIM, axis=1))


def _store_token_tiles(ref, val):
    t = val.shape[0]
    for s in range(ROW_CHUNKS):
        ref[pl.ds(s, t, stride=ROW_CHUNKS), :] = val[:, s * LANES:(s + 1) * LANES]


def _load_token_tiles(ref, t):
    return jnp.concatenate([ref[pl.ds(s, t, stride=ROW_CHUNKS), :] for s in range(ROW_CHUNKS)],
                           axis=1)


def _group_norm_pairs(pairs, gain_ref, out_ref):
    ss = jnp.sum(sum(p * p for p in pairs), axis=-1, keepdims=True)
    scale = jnp.broadcast_to(lax.rsqrt(ss * (1.0 / GROUP_WIDTH) + EPS), pairs[0].shape)
    for i, p in enumerate(pairs):
        sl = slice(i * LANES, (i + 1) * LANES)
        out_ref[:, sl] = (p * scale * gain_ref[:, sl]).astype(out_ref.dtype)


COL_A, COL_B, COL_Q, COL_K, COL_V, COL_QM, COL_F = 0, 256, 512, 768, 1024, 1280, 1536
IN_COLS_PADDED = COL_F + LANES


def _inproj_kernel(x_ref, g_ref, w_ref, qg_ref, kg_ref, mqg_ref, fb_ref, tri_ref, sel_ref, ones_ref,
                   ua_ref, ub_ref, qa_ref, ka_ref, va_ref, qm_ref, carry_ref, *, tiles_per_seq):
    i = pl.program_id(0)

    @pl.when(i % tiles_per_seq == 0)
    def _():
        carry_ref[...] = jnp.zeros_like(carry_ref)

    x = x_ref[...]
    h = x * lax.rsqrt(jnp.mean(x * x, axis=-1, keepdims=True) + EPS) * g_ref[...]
    proj = jnp.dot(h.astype(BF16), w_ref[...], preferred_element_type=F32)
    ua_ref[0] = proj[:, COL_A:COL_A + LANES]
    ua_ref[1] = proj[:, COL_A + LANES:COL_A + GROUP_WIDTH]
    ub_ref[...] = proj[:, COL_B:COL_B + GROUP_WIDTH]

    z = proj[:, COL_F:COL_F + LANES] + fb_ref[...]
    lane = _lane_iota(z.shape)
    logf = jnp.minimum(z, 0.0) - jnp.log(1.0 + jnp.exp(-jnp.abs(z)))
    hi, mid, lo = _split3(logf)
    packed = jnp.where(lane < 4, hi, jnp.where(lane < 8, mid, jnp.where(lane < 12, lo, 0.0)))
    cs = jnp.dot(tri_ref[...], packed.astype(BF16), preferred_element_type=F32)
    cum = cs + pltpu.roll(cs, LANES - 4, axis=1) + pltpu.roll(cs, LANES - 8, axis=1)
    cum = cum + carry_ref[...]
    carry_ref[...] = cum[cum.shape[0] - 1:, :]

    scale = HEAD_DIM ** -0.5 * LOG2E
    c_parts = jnp.concatenate(_split3(cum * LOG2E), axis=1).astype(BF16)
    bias = jnp.dot(c_parts, sel_ref[...], preferred_element_type=F32)

    def heads_normed(col, gain_ref):
        t = proj[:, col:col + GROUP_WIDTH]
        sq = t * t
        sq_hi = sq.astype(BF16)
        sq_lo = (sq - sq_hi.astype(F32)).astype(BF16)
        ss = jnp.dot(jnp.concatenate([sq_hi, sq_lo], axis=1), ones_ref[...],
                     preferred_element_type=F32)
        return t * lax.rsqrt(ss * (1.0 / HEAD_DIM) + EPS) * gain_ref[...]

    qn_all = heads_normed(COL_Q, qg_ref) * scale
    kn_all = heads_normed(COL_K, kg_ref)
    mn_all = heads_normed(COL_QM, mqg_ref) * scale
    q_bias = (lane >= 64) & (lane < 67)
    k_bias = (lane >= 67) & (lane < 70)
    for hd in range(N_HEADS):
        pair, odd = hd // 2, hd % 2
        pair_sl = slice(pair * LANES, (pair + 1) * LANES)
        b = bias[:, hd * LANES:(hd + 1) * LANES]
        qn = _head_slab(qn_all[:, pair_sl], odd)
        qa_ref[hd] = jnp.where(q_bias, b, jnp.where(k_bias, 1.0, qn)).astype(BF16)
        kn = _head_slab(kn_all[:, pair_sl], odd)
        ka_ref[hd] = jnp.where(k_bias, b, jnp.where(q_bias, 1.0, kn)).astype(BF16)
        vs = _head_slab(proj[:, COL_V + pair * LANES:COL_V + (pair + 1) * LANES], odd)
        va_ref[hd] = jnp.where(lane == 64, 1.0, vs).astype(BF16)
        qm_ref[hd] = _head_slab(mn_all[:, pair_sl], odd).astype(BF16)


def _inproj(x2, seq, mix_g, w_all, qg, kg, mqg, fb, tri, sel, ones):
    n, d = x2.shape
    tm = min(TM_INPROJ, seq)
    grid = (n // tm,)
    const = lambda i: (0, 0)
    row = lambda i: (i, 0)
    hrow = lambda i: (0, i, 0)
    out_shape = (
        jax.ShapeDtypeStruct((2, n, LANES), F32),
        jax.ShapeDtypeStruct((n, GROUP_WIDTH), F32),
        jax.ShapeDtypeStruct((N_HEADS, n, LANES), BF16),
        jax.ShapeDtypeStruct((N_HEADS, n, LANES), BF16),
        jax.ShapeDtypeStruct((N_HEADS, n, LANES), BF16),
        jax.ShapeDtypeStruct((N_HEADS, n, LANES), BF16),
    )
    return pl.pallas_call(
        functools.partial(_inproj_kernel, tiles_per_seq=seq // tm),
        grid=grid,
        in_specs=[
            pl.BlockSpec((tm, d), row),
            pl.BlockSpec((1, d), const),
            pl.BlockSpec((d, IN_COLS_PADDED), const),
            pl.BlockSpec((1, GROUP_WIDTH), const),
            pl.BlockSpec((1, GROUP_WIDTH), const),
            pl.BlockSpec((1, GROUP_WIDTH), const),
            pl.BlockSpec((1, LANES), const),
            pl.BlockSpec((tm, tm), const),
            pl.BlockSpec((3 * LANES, N_HEADS * LANES), const),
            pl.BlockSpec((2 * GROUP_WIDTH, GROUP_WIDTH), const),
        ],
        out_specs=(
            pl.BlockSpec((2, tm, LANES), hrow),
            pl.BlockSpec((tm, GROUP_WIDTH), row),
            pl.BlockSpec((N_HEADS, tm, LANES), hrow),
            pl.BlockSpec((N_HEADS, tm, LANES), hrow),
            pl.BlockSpec((N_HEADS, tm, LANES), hrow),
            pl.BlockSpec((N_HEADS, tm, LANES), hrow),
        ),
        out_shape=out_shape,
        scratch_shapes=[pltpu.VMEM((1, LANES), F32)],
        compiler_params=_cparams(("arbitrary",)),
        name="inproj",
    )(x2, mix_g, w_all, qg, kg, mqg, fb, tri, sel, ones)


def _s5_kernel(u_ref, bmat_ref, coef_ref, pw_ref, cmat_ref, d_ref, gluw_ref, glub_ref, og_ref,
               o_ref, u_scr, x_scr, y_scr, carry_ref):
    @pl.when(pl.program_id(1) == 0)
    def _():
        carry_ref[...] = jnp.zeros_like(carry_ref)

    t = u_ref.shape[1]
    g = t // SUBLANES
    halves = [slice(h * LANES, (h + 1) * LANES) for h in range(GROUP_WIDTH // LANES)]
    for j in range(g):
        for h, sl in enumerate(halves):
            u_scr[j * SUBLANES:(j + 1) * SUBLANES, sl] = u_ref[h, pl.ds(j, SUBLANES, stride=g), :]
    u = u_scr[...]
    x_scr[...] = jnp.dot(u.astype(BF16), bmat_ref[...], preferred_element_type=F32)
    n_lane_blocks = SSM_LANES // LANES
    re_sl = [slice(lb * LANES, (lb + 1) * LANES) for lb in range(n_lane_blocks)]
    im_sl = [slice(SSM_LANES + lb * LANES, SSM_LANES + (lb + 1) * LANES)
             for lb in range(n_lane_blocks)]

    def group_rows(j):
        return pl.ds(pl.multiple_of(j * SUBLANES, SUBLANES), SUBLANES)

    def pass1(j, h):
        rows = group_rows(j)
        new = []
        for lb in range(n_lane_blocks):
            ar = coef_ref[0, :, re_sl[lb]]
            ai = coef_ref[1, :, re_sl[lb]]
            hr, hi = h[2 * lb], h[2 * lb + 1]
            nr = ar * hr - ai * hi + x_scr[rows, re_sl[lb]]
            ni = ar * hi + ai * hr + x_scr[rows, im_sl[lb]]
            x_scr[rows, re_sl[lb]] = nr
            x_scr[rows, im_sl[lb]] = ni
            new += [nr, ni]
        return tuple(new)

    zero = jnp.zeros((SUBLANES, LANES), F32)
    finals = lax.fori_loop(0, g, pass1, (zero,) * (2 * n_lane_blocks))

    row_id = lax.broadcasted_iota(I32, (SUBLANES, LANES), 0)
    starts = []
    for lb in range(n_lane_blocks):
        fr, fi = finals[2 * lb], finals[2 * lb + 1]
        for s, k in enumerate((1, 2, 4)):
            cr = coef_ref[2 + 2 * s, :, re_sl[lb]]
            ci = coef_ref[3 + 2 * s, :, re_sl[lb]]
            sr = pltpu.roll(fr, k, axis=0)
            si = pltpu.roll(fi, k, axis=0)
            fr, fi = fr + cr * sr - ci * si, fi + cr * si + ci * sr
        qr = coef_ref[8, :, re_sl[lb]]
        qi = coef_ref[9, :, re_sl[lb]]
        cbr = carry_ref[:, re_sl[lb]]
        cbi = carry_ref[:, im_sl[lb]]
        fr, fi = fr + qr * cbr - qi * cbi, fi + qr * cbi + qi * cbr
        starts.append(jnp.where(row_id == 0, cbr, pltpu.roll(fr, 1, axis=0)))
        starts.append(jnp.where(row_id == 0, cbi, pltpu.roll(fi, 1, axis=0)))
        carry_ref[:, re_sl[lb]] = jnp.broadcast_to(fr[SUBLANES - 1:, :], fr.shape)
        carry_ref[:, im_sl[lb]] = jnp.broadcast_to(fi[SUBLANES - 1:, :], fi.shape)

    def pass2(j, c):
        rows = group_rows(j)
        for lb in range(n_lane_blocks):
            pr = pw_ref[rows, re_sl[lb]]
            pi_ = pw_ref[rows, im_sl[lb]]
            sr, si = starts[2 * lb], starts[2 * lb + 1]
            x_scr[rows, re_sl[lb]] += pr * sr - pi_ * si
            x_scr[rows, im_sl[lb]] += pr * si + pi_ * sr
        return c

    lax.fori_loop(0, g, pass2, 0)

    y = jnp.dot(x_scr[...].astype(BF16), cmat_ref[...], preferred_element_type=F32) + d_ref[...] * u
    z = jax.nn.gelu(y, approximate=True)
    gate = jnp.dot(z.astype(BF16), gluw_ref[...], preferred_element_type=F32) + glub_ref[...]
    out = z * (1.0 / (1.0 + jnp.exp(-gate)))
    _group_norm_pairs([out[:, :LANES], out[:, LANES:]], og_ref, y_scr)
    for j in range(g):
        for h, sl in enumerate(halves):
            o_ref[h, pl.ds(j, SUBLANES, stride=g), :] = y_scr[j * SUBLANES:(j + 1) * SUBLANES, sl]


def _s5(ua, bsz, seq, bmat, coef, pw, cmat, dvec, gluw, glub, og):
    n = ua.shape[1]
    t = pw.shape[0]
    nt = seq // t
    row = lambda b, j: (0, b * nt + j, 0)
    c2 = lambda b, j: (0, 0)
    c3 = lambda b, j: (0, 0, 0)
    return pl.pallas_call(
        _s5_kernel,
        grid=(bsz, nt),
        in_specs=[
            pl.BlockSpec((2, t, LANES), row),
            pl.BlockSpec((GROUP_WIDTH, 2 * SSM_LANES), c2),
            pl.BlockSpec((10, SUBLANES, SSM_LANES), c3),
            pl.BlockSpec((t, 2 * SSM_LANES), c2),
            pl.BlockSpec((2 * SSM_LANES, GROUP_WIDTH), c2),
            pl.BlockSpec((1, GROUP_WIDTH), c2),
            pl.BlockSpec((GROUP_WIDTH, GROUP_WIDTH), c2),
            pl.BlockSpec((1, GROUP_WIDTH), c2),
            pl.BlockSpec((1, GROUP_WIDTH), c2),
        ],
        out_specs=pl.BlockSpec((2, t, LANES), row),
        out_shape=jax.ShapeDtypeStruct((2, n, LANES), F32),
        scratch_shapes=[pltpu.VMEM((t, GROUP_WIDTH), F32),
                        pltpu.VMEM((t, 2 * SSM_LANES), F32),
                        pltpu.VMEM((t, GROUP_WIDTH), F32),
                        pltpu.VMEM((SUBLANES, 2 * SSM_LANES), F32)],
        compiler_params=_cparams(("arbitrary", "arbitrary")),
        name="s5",
    )(ua, bmat, coef, pw, cmat, dvec, gluw, glub, og)


def _s5_constants(lam_re, lam_im, log_dt, b_re, b_im, c_re, c_im, tile):
    lr = lam_re.astype(F32)
    li = lam_im.astype(F32)
    dt = jnp.exp(log_dt.astype(F32))[:, None]
    mag = jnp.exp(lr * dt)
    a_re = mag * jnp.cos(li * dt)
    a_im = mag * jnp.sin(li * dt)
    den = lr * lr + li * li
    n_re = a_re - 1.0
    n_im = a_im
    k_re = (n_re * lr + n_im * li) / den
    k_im = (n_im * lr - n_re * li) / den
    br = b_re.astype(F32)
    bi = b_im.astype(F32)
    bb_re = k_re[..., None] * br - k_im[..., None] * bi
    bb_im = k_re[..., None] * bi + k_im[..., None] * br
    eye = jnp.eye(SSM_GROUPS, dtype=F32)
    bm_re = jnp.einsum('gph,gk->ghkp', bb_re, eye).reshape(GROUP_WIDTH, SSM_LANES)
    bm_im = jnp.einsum('gph,gk->ghkp', bb_im, eye).reshape(GROUP_WIDTH, SSM_LANES)
    bmat = jnp.concatenate([bm_re, bm_im], axis=1).astype(BF16)
    cm_re = jnp.einsum('ghp,gk->gpkh', c_re.astype(F32), eye).reshape(SSM_LANES, GROUP_WIDTH)
    cm_im = jnp.einsum('ghp,gk->gpkh', c_im.astype(F32), eye).reshape(SSM_LANES, GROUP_WIDTH)
    cmat = jnp.concatenate([cm_re, -cm_im], axis=0).astype(BF16)

    ar = a_re.reshape(1, SSM_LANES)
    ai = a_im.reshape(1, SSM_LANES)

    def cmul(x, y):
        return x[0] * y[0] - x[1] * y[1], x[0] * y[1] + x[1] * y[0]

    groups = tile // SUBLANES
    pows = [(ar, ai)]
    for _ in range(groups - 1):
        pows.append(cmul(pows[-1], (ar, ai)))
    pw = jnp.concatenate([jnp.repeat(jnp.concatenate([p[0] for p in pows], axis=0), SUBLANES, axis=0),
                          jnp.repeat(jnp.concatenate([p[1] for p in pows], axis=0), SUBLANES, axis=0)],
                         axis=1)
    chunk = pows[groups - 1]
    cpows = [chunk]
    for _ in range(SUBLANES - 1):
        cpows.append(cmul(cpows[-1], chunk))
    rows = jnp.arange(SUBLANES, dtype=I32)[:, None]
    ones = jnp.ones((SUBLANES, 1), F32)
    planes = [ones * ar, ones * ai]
    for k in (1, 2, 4):
        m = (rows >= k).astype(F32)
        planes += [m * cpows[k - 1][0], m * cpows[k - 1][1]]
    planes += [jnp.concatenate([p[0] for p in cpows], axis=0),
               jnp.concatenate([p[1] for p in cpows], axis=0)]
    coef = jnp.stack(planes, axis=0)
    return bmat, coef, pw, cmat


def _pool_kernel(v_ref, w_ref, g_ref, o_ref, ext_ref):
    j = pl.program_id(1)
    t = v_ref.shape[0]

    @pl.when(j == 0)
    def _():
        ext_ref[0:POOL_HALO, :] = jnp.zeros((POOL_HALO, GROUP_WIDTH), F32)

    v = v_ref[...]
    ext_ref[POOL_HALO:POOL_HALO + t, :] = v
    cur = ext_ref[...]
    width = 1
    wins = {}
    while width < POOL_WINDOWS[-1]:
        cur = cur + pltpu.roll(cur, width, axis=0)
        width *= 2
        wins[width] = cur[POOL_HALO:, :]
    lane = _lane_iota(v.shape)
    pos = (j * t + lax.broadcasted_iota(I32, v.shape, 0) + 1).astype(F32)
    mean = None
    for gi, w in enumerate(POOL_WINDOWS):
        m = wins[w] / jnp.minimum(pos, float(w))
        mean = m if mean is None else jnp.where(lane >= gi * HEAD_DIM, m, mean)
    mixed = jnp.dot((mean - v).astype(BF16), w_ref[...], preferred_element_type=F32)
    _group_norm_pairs([mixed[:, :LANES], mixed[:, LANES:]], g_ref, o_ref)
    ext_ref[0:POOL_HALO, :] = v[t - POOL_HALO:, :]


def _pool(ub, bsz, seq, w_blk, g):
    n = ub.shape[0]
    t = min(T_POOL, seq)
    nt = seq // t
    row = lambda b, j: (b * nt + j, 0)
    c2 = lambda b, j: (0, 0)
    return pl.pallas_call(
        _pool_kernel,
        grid=(bsz, nt),
        in_specs=[pl.BlockSpec((t, GROUP_WIDTH), row),
                  pl.BlockSpec((GROUP_WIDTH, GROUP_WIDTH), c2),
                  pl.BlockSpec((1, GROUP_WIDTH), c2)],
        out_specs=pl.BlockSpec((t, GROUP_WIDTH), row),
        out_shape=jax.ShapeDtypeStruct((n, GROUP_WIDTH), BF16),
        scratch_shapes=[pltpu.VMEM((t + POOL_HALO, GROUP_WIDTH), F32)],
        compiler_params=_cparams(("arbitrary", "arbitrary")),
        name="pool",
    )(ub, w_blk, g)


def _fox_kernel(qt_ref, kt_ref, qa_ref, ka_ref, va_ref, g_ref, o_ref, m_ref, acc_ref):
    p_id = pl.program_id(1)
    qi = qt_ref[p_id]
    ki = kt_ref[p_id]
    tq = qa_ref.shape[1]
    tk = ka_ref.shape[1]

    @pl.when(ki == 0)
    def _():
        m_ref[...] = jnp.full_like(m_ref, NEG_INF)
        acc_ref[...] = jnp.zeros_like(acc_ref)

    def step(on_diagonal):
        if on_diagonal:
            causal = (lax.broadcasted_iota(I32, (tq, tk), 0)
                      >= lax.broadcasted_iota(I32, (tq, tk), 1))
        def scores(hd):
            return lax.dot_general(qa_ref[hd], ka_ref[hd], (((1,), (1,)), ((), ())),
                                   preferred_element_type=F32)

        s_next = scores(0)
        for hd in range(N_HEADS):
            s = s_next
            if hd + 1 < N_HEADS:
                s_next = scores(hd + 1)
            if on_diagonal:
                s = jnp.where(causal, s, NEG_INF)
            m_prev = m_ref[hd]
            m_new = jnp.maximum(m_prev, jnp.broadcast_to(jnp.max(s, axis=-1, keepdims=True),
                                                         m_prev.shape))
            alpha = jnp.exp2(m_prev - m_new)
            p = jnp.exp2(s - jnp.concatenate([m_new] * (tk // LANES), axis=1))
            acc_ref[hd] = alpha * acc_ref[hd] + jnp.dot(p.astype(BF16), va_ref[hd],
                                                        preferred_element_type=F32)
            m_ref[hd] = m_new

    @pl.when(ki < qi)
    def _():
        step(False)

    @pl.when(ki == qi)
    def _():
        step(True)
        heads = []
        for hd in range(N_HEADS):
            acc = acc_ref[hd]
            heads.append(_divide_by_denominator(acc))
        _group_norm_pairs([_join_heads(heads[0], heads[1]), _join_heads(heads[2], heads[3])],
                          g_ref, o_ref)


def _fox(qa, ka, va, bsz, seq, g):
    n = qa.shape[1]
    t = min(T_ATT, seq)
    nq = seq // t
    pairs = [(q, k) for q in range(nq) for k in range(q + 1)]
    qt = jnp.asarray([p[0] for p in pairs], I32)
    kt = jnp.asarray([p[1] for p in pairs], I32)
    qmap = lambda b, p, qt, kt: (0, b * nq + qt[p], 0)
    kmap = lambda b, p, qt, kt: (0, b * nq + kt[p], 0)
    grid_spec = pltpu.PrefetchScalarGridSpec(
        num_scalar_prefetch=2,
        grid=(bsz, len(pairs)),
        in_specs=[pl.BlockSpec((N_HEADS, t, LANES), qmap),
                  pl.BlockSpec((N_HEADS, t, LANES), kmap),
                  pl.BlockSpec((N_HEADS, t, LANES), kmap),
                  pl.BlockSpec((1, GROUP_WIDTH), lambda b, p, qt, kt: (0, 0))],
        out_specs=pl.BlockSpec((t, GROUP_WIDTH), lambda b, p, qt, kt: (b * nq + qt[p], 0)),
        scratch_shapes=[pltpu.VMEM((N_HEADS, t, LANES), F32),
                        pltpu.VMEM((N_HEADS, t, LANES), F32)],
    )
    return pl.pallas_call(
        _fox_kernel,
        grid_spec=grid_spec,
        out_shape=jax.ShapeDtypeStruct((n, GROUP_WIDTH), BF16),
        compiler_params=_cparams(("arbitrary", "arbitrary")),
        name="fox",
    )(qt, kt, qa, ka, va, g)


def _memkv_kernel(mem_ref, g_ref, w_ref, kg_ref, mk_ref, mvt_ref):
    x = mem_ref[0]
    h = x * lax.rsqrt(jnp.mean(x * x, axis=-1, keepdims=True) + EPS) * g_ref[...]
    kv = jnp.dot(h.astype(BF16), w_ref[...], preferred_element_type=F32)
    m = x.shape[0]
    v_t = jnp.transpose(kv[:, GROUP_WIDTH:])
    tail = jnp.where(lax.broadcasted_iota(I32, (HEAD_DIM, m), 0) == 0, 1.0, 0.0)
    for hd in range(N_HEADS):
        pair, odd = hd // 2, hd % 2
        ks = _head_slab(kv[:, pair * LANES:(pair + 1) * LANES], odd)
        mk_ref[0, hd] = _head_norm(ks, kg_ref[...]).astype(BF16)
        mvt_ref[0, hd] = jnp.concatenate([v_t[hd * HEAD_DIM:(hd + 1) * HEAD_DIM], tail],
                                         axis=0).astype(BF16)


def _memkv(mem, g, w_kv, kg):
    bsz, m, d = mem.shape
    c2 = lambda b: (0, 0)
    out = jax.ShapeDtypeStruct((bsz, N_HEADS, m, LANES), BF16)
    out_t = jax.ShapeDtypeStruct((bsz, N_HEADS, LANES, m), BF16)
    return pl.pallas_call(
        _memkv_kernel,
        grid=(bsz,),
        in_specs=[pl.BlockSpec((1, m, d), lambda b: (b, 0, 0)),
                  pl.BlockSpec((1, d), c2),
                  pl.BlockSpec((d, 2 * GROUP_WIDTH), c2),
                  pl.BlockSpec((1, LANES), c2)],
        out_specs=(pl.BlockSpec((1, N_HEADS, m, LANES), lambda b: (b, 0, 0, 0)),
                   pl.BlockSpec((1, N_HEADS, LANES, m), lambda b: (b, 0, 0, 0))),
        out_shape=(out, out_t),
        compiler_params=_cparams(("arbitrary",)),
        name="memkv",
    )(mem, g, w_kv, kg)


def _memattn_kernel(qm_ref, mk_ref, mvt_ref, g_ref, o_ref):
    heads = []
    for hd in range(N_HEADS):
        s_t = lax.dot_general(mk_ref[0, hd], qm_ref[hd], (((1,), (1,)), ((), ())),
                              preferred_element_type=F32)
        p_t = jnp.exp2(s_t - jnp.max(s_t, axis=0, keepdims=True))
        acc_t = jnp.dot(mvt_ref[0, hd], p_t.astype(BF16), preferred_element_type=F32)
        heads.append(acc_t[:HEAD_DIM] * (1.0 / acc_t[HEAD_DIM:HEAD_DIM + 1]))
    o_t = jnp.concatenate(heads, axis=0)
    ss = jnp.sum(o_t * o_t, axis=0, keepdims=True)
    o_t = o_t * lax.rsqrt(ss * (1.0 / GROUP_WIDTH) + EPS)
    o_ref[...] = (jnp.transpose(o_t) * g_ref[...]).astype(o_ref.dtype)


def _memattn(qm, mk, mv, bsz, seq, g):
    n = qm.shape[1]
    m = mk.shape[2]
    t = min(T_ATT, seq)
    nt = seq // t
    return pl.pallas_call(
        _memattn_kernel,
        grid=(bsz, nt),
        in_specs=[pl.BlockSpec((N_HEADS, t, LANES), lambda b, j: (0, b * nt + j, 0)),
                  pl.BlockSpec((1, N_HEADS, m, LANES), lambda b, j: (b, 0, 0, 0)),
                  pl.BlockSpec((1, N_HEADS, LANES, m), lambda b, j: (b, 0, 0, 0)),
                  pl.BlockSpec((1, GROUP_WIDTH), lambda b, j: (0, 0))],
        out_specs=pl.BlockSpec((t, GROUP_WIDTH), lambda b, j: (b * nt + j, 0)),
        out_shape=jax.ShapeDtypeStruct((n, GROUP_WIDTH), BF16),
        compiler_params=_cparams(("arbitrary", "arbitrary")),
        name="memattn",
    )(qm, mk, mv, g)


def _outproj_kernel(x_ref, ya_ref, yb_ref, yc_ref, ym_ref, w_ref, fg_ref, rw_ref, rb_ref, us_ref,
                    x1_ref, hn_ref, idx_ref, gate_ref, rank_ref, cnt_ref, toff_ref, carry_ref):
    @pl.when(pl.program_id(0) == 0)
    def _():
        carry_ref[...] = jnp.zeros_like(carry_ref)

    merged = jnp.concatenate([ya_ref[0].astype(BF16), ya_ref[1].astype(BF16), yb_ref[...],
                              yc_ref[...], ym_ref[...]], axis=1)
    acc = x_ref[...] + jnp.dot(merged, w_ref[...], preferred_element_type=F32)
    x1_ref[...] = acc
    hn = acc * lax.rsqrt(jnp.mean(acc * acc, axis=-1, keepdims=True) + EPS) * fg_ref[...]
    _store_token_tiles(hn_ref, hn)
    n_exp = rb_ref.shape[0]
    tm = hn.shape[0]
    hn_hi = hn.astype(BF16)
    hn_lo = (hn - hn_hi.astype(F32)).astype(BF16)
    parts = jnp.dot(jnp.concatenate([hn_hi, hn_lo], axis=0), rw_ref[...],
                    preferred_element_type=F32)
    top = jnp.transpose(parts[:tm])
    bot = jnp.transpose(parts[tm:])
    logits = (top[:n_exp] + top[n_exp:2 * n_exp] + bot[:n_exp] + bot[n_exp:2 * n_exp]
              + rb_ref[...])
    e_iota = lax.broadcasted_iota(I32, (n_exp, tm), 0).astype(F32)
    work = logits
    vals, onehots = [], []
    for k in range(TOP_K):
        m = jnp.max(work, axis=0, keepdims=True)
        sel = jnp.min(jnp.where(work == m, e_iota, float(n_exp)), axis=0, keepdims=True)
        hot = e_iota == sel
        idx_ref[k:k + 1, :] = sel.astype(I32)
        vals.append(m)
        onehots.append(hot.astype(F32))
        work = jnp.where(hot, NEG_INF, work)
    exps = [jnp.exp(v - vals[0]) for v in vals]
    denom = exps[0] + exps[1] + exps[2] + exps[3]
    for k in range(TOP_K):
        gate_ref[k:k + 1, :] = exps[k] / denom
    stacked = jnp.concatenate(onehots, axis=0).astype(BF16)
    prefix = jnp.dot(stacked, us_ref[...], preferred_element_type=F32)
    base = carry_ref[...]
    base_sq = jnp.concatenate([base, jnp.zeros((LANES - n_exp, LANES), F32)], axis=0)
    toff_ref[...] = jnp.transpose(base_sq)[:SUBLANES, :]
    for k in range(TOP_K):
        hot = onehots[k]
        pk = prefix[k * n_exp:(k + 1) * n_exp, :]
        rank = jnp.sum(hot * (pk + base[:, 0:1]), axis=0, keepdims=True)
        rank_ref[k:k + 1, :] = rank.astype(I32)
        base = base + jnp.sum(hot, axis=1, keepdims=True)
    carry_ref[...] = base
    cnt_ref[...] = base


def _outproj(x2, ya, yb, yc, ym, w_out, fg, rw, rb, ustrict):
    n, d = x2.shape
    n_exp = rb.shape[0]
    tm = min(TM_OUT, n)
    row = lambda i: (i, 0)
    col = lambda i: (0, i)
    const = lambda i: (0, 0)
    return pl.pallas_call(
        _outproj_kernel,
        grid=(n // tm,),
        in_specs=[pl.BlockSpec((tm, d), row)]
        + [pl.BlockSpec((2, tm, LANES), lambda i: (0, i, 0))]
        + [pl.BlockSpec((tm, GROUP_WIDTH), row)] * 3
        + [pl.BlockSpec((d, d), const),
           pl.BlockSpec((1, d), const),
           pl.BlockSpec((d, LANES), const),
           pl.BlockSpec((n_exp, 1), const),
           pl.BlockSpec((tm, tm), const)],
        out_specs=(pl.BlockSpec((tm, d), row),
                   pl.BlockSpec((tm * ROW_CHUNKS, LANES), row),
                   pl.BlockSpec((TOP_K, tm), col),
                   pl.BlockSpec((TOP_K, tm), col),
                   pl.BlockSpec((TOP_K, tm), col),
                   pl.BlockSpec((n_exp, LANES), const),
                   pl.BlockSpec((SUBLANES, LANES), row)),
        out_shape=(jax.ShapeDtypeStruct((n, d), F32),
                   jax.ShapeDtypeStruct((n * ROW_CHUNKS, LANES), F32),
                   jax.ShapeDtypeStruct((TOP_K, n), I32),
                   jax.ShapeDtypeStruct((TOP_K, n), F32),
                   jax.ShapeDtypeStruct((TOP_K, n), I32),
                   jax.ShapeDtypeStruct((n_exp, LANES), F32),
                   jax.ShapeDtypeStruct((n // tm * SUBLANES, LANES), F32)),
        scratch_shapes=[pltpu.VMEM((n_exp, LANES), F32)],
        compiler_params=_cparams(("arbitrary",)),
        name="outproj",
    )(x2, ya, yb, yc, ym, w_out, fg, rw, rb, ustrict)


def _plan_kernel(cnt_ref, idx_ref, rank_ref, toff_ref, dest_ref, meta_ref, start_ref, win_ref,
                 *, n_exp, block_rows):
    def body(e, off):
        start_ref[e] = off
        return off + (cnt_ref[e] + block_rows - 1) // block_rows * block_rows

    total = lax.fori_loop(0, n_exp, body, jnp.int32(0))
    idx = idx_ref[...]
    dest = rank_ref[...]
    blk_start = lax.broadcasted_iota(I32, meta_ref.shape, 1) * block_rows
    blk_e = jnp.zeros(meta_ref.shape, I32)
    lane = _lane_iota((SUBLANES, LANES))
    start_l = jnp.zeros((SUBLANES, LANES), I32)
    cnt_l = jnp.zeros((SUBLANES, LANES), I32)
    for e in range(n_exp):
        dest = dest + jnp.where(idx == e, start_ref[e], 0)
        start_l = jnp.where(lane == e, start_ref[e], start_l)
        cnt_l = jnp.where(lane == e, cnt_ref[e], cnt_l)
        if e > 0:
            blk_e = blk_e + (blk_start >= start_ref[e]).astype(I32)
    dest_ref[...] = dest
    sub = lax.broadcasted_iota(I32, meta_ref.shape, 0)
    meta_ref[...] = jnp.where(sub == 0, blk_e, total // block_rows)

    toff = toff_ref[...].astype(I32)
    n_rows = toff.shape[0]
    nxt = jnp.concatenate([toff[SUBLANES:], cnt_l], axis=0) if n_rows > SUBLANES else cnt_l
    first = jnp.concatenate([start_l] * (n_rows // SUBLANES), axis=0) + toff
    aligned = jnp.bitwise_and(first, -BF16_ROWS)
    lanes = _lane_iota(toff.shape)
    span = jnp.where(lanes < n_exp, first - aligned + (nxt - toff), 0)
    n_win = jnp.right_shift(span + (WINDOW_ROWS - 1), WINDOW_SHIFT)
    win_ref[...] = jnp.where(lanes < n_exp, aligned, pltpu.roll(n_win, n_exp, axis=1))


def _plan(cnt, idx, rank, toff, n_blocks):
    n_exp = cnt.shape[0]
    nb_pad = (n_blocks + LANES - 1) // LANES * LANES
    return pl.pallas_call(
        functools.partial(_plan_kernel, n_exp=n_exp, block_rows=TM_EXPERT),
        in_specs=[pl.BlockSpec(memory_space=pltpu.SMEM),
                  pl.BlockSpec(memory_space=pltpu.VMEM),
                  pl.BlockSpec(memory_space=pltpu.VMEM),
                  pl.BlockSpec(memory_space=pltpu.VMEM)],
        out_specs=(pl.BlockSpec(memory_space=pltpu.VMEM),
                   pl.BlockSpec(memory_space=pltpu.VMEM),
                   pl.BlockSpec(memory_space=pltpu.SMEM),
                   pl.BlockSpec(memory_space=pltpu.VMEM)),
        out_shape=(jax.ShapeDtypeStruct(idx.shape, I32),
                   jax.ShapeDtypeStruct((SUBLANES, nb_pad), I32),
                   jax.ShapeDtypeStruct((n_exp,), I32),
                   jax.ShapeDtypeStruct(toff.shape, I32)),
        compiler_params=pltpu.CompilerParams(vmem_limit_bytes=VMEM_LIMIT),
        name="plan",
    )(cnt, idx, rank, toff)


def _tile_rows(row):
    return pl.ds(pl.multiple_of(row * ROW_CHUNKS, ROW_CHUNKS), ROW_CHUNKS)


def _dispatch_kernel(dest_ref, cnt_ref, start_ref, hn_ref, xs_ref, zero_ref, sem, zsem,
                     *, n_exp, block_rows):
    tm = dest_ref.shape[1]

    def pad_copy(e, r):
        return pltpu.make_async_copy(zero_ref.at[pl.ds(0, ROW_CHUNKS)],
                                     xs_ref.at[_tile_rows(start_ref[e] + r)], zsem)

    def pad_bounds(e):
        cnt = cnt_ref[e]
        return cnt, (cnt + block_rows - 1) // block_rows * block_rows

    @pl.when(pl.program_id(0) == 0)
    def _():
        zero_ref[...] = jnp.zeros_like(zero_ref)

        def pad_start(e, c):
            lo, hi = pad_bounds(e)
            return lax.fori_loop(lo, hi, lambda r, c: (pad_copy(e, r).start(), c)[1], c)

        def pad_wait(e, c):
            lo, hi = pad_bounds(e)
            return lax.fori_loop(lo, hi, lambda r, c: (pad_copy(e, r).wait(), c)[1], c)

        lax.fori_loop(0, n_exp, pad_start, 0)
        lax.fori_loop(0, n_exp, pad_wait, 0)

        last = n_exp - 1
        used = start_ref[last] + pad_bounds(last)[1]
        n_blocks = xs_ref.shape[0] // (ROW_CHUNKS * block_rows)

        def tail_copy(b):
            rows = pl.ds(pl.multiple_of(b * (block_rows * ROW_CHUNKS), block_rows * ROW_CHUNKS),
                         block_rows * ROW_CHUNKS)
            return pltpu.make_async_copy(zero_ref, xs_ref.at[rows], zsem)

        lax.fori_loop(used // block_rows, n_blocks, lambda b, c: (tail_copy(b).start(), c)[1], 0)
        lax.fori_loop(used // block_rows, n_blocks, lambda b, c: (tail_copy(b).wait(), c)[1], 0)

    def copy(k, t):
        return pltpu.make_async_copy(hn_ref.at[_tile_rows(t)],
                                     xs_ref.at[_tile_rows(dest_ref[k, t])], sem)

    def start(t, c):
        for k in range(TOP_K):
            copy(k, t).start(priority=k % 2)
        return c

    def wait(t, c):
        for k in range(TOP_K):
            copy(k, t).wait()
        return c

    lax.fori_loop(0, tm, start, 0)
    lax.fori_loop(0, tm, wait, 0)


def _dispatch(dest, cnt, starts, hn, cap):
    n = hn.shape[0] // ROW_CHUNKS
    tm = min(T_MOVE, n)
    return pl.pallas_call(
        functools.partial(_dispatch_kernel, n_exp=cnt.shape[0], block_rows=TM_EXPERT),
        grid=(n // tm,),
        in_specs=[pl.BlockSpec((TOP_K, tm), lambda i: (0, i), memory_space=pltpu.SMEM),
                  pl.BlockSpec(memory_space=pltpu.SMEM),
                  pl.BlockSpec(memory_space=pltpu.SMEM),
                  pl.BlockSpec((tm * ROW_CHUNKS, LANES), lambda i: (i, 0))],
        out_specs=pl.BlockSpec(memory_space=pl.ANY),
        out_shape=jax.ShapeDtypeStruct((cap * ROW_CHUNKS, LANES), hn.dtype),
        scratch_shapes=[pltpu.VMEM((TM_EXPERT * ROW_CHUNKS, LANES), hn.dtype),
                        pltpu.SemaphoreType.DMA, pltpu.SemaphoreType.DMA],
        compiler_params=_cparams(("arbitrary",)),
        name="dispatch",
    )(dest, cnt, starts, hn)


def _experts_kernel(meta_ref, xs_ref, wg_ref, bg_ref, wu_ref, bu_ref, wd_ref, bd_ref, ys_ref,
                    wg_s, wu_s, wd_s):
    i = pl.program_id(0)
    tm = xs_ref.shape[0] // ROW_CHUNKS
    in_use = i < meta_ref[1, 0]
    new_expert = jnp.logical_or(i == 0, meta_ref[0, i] != meta_ref[0, jnp.maximum(i - 1, 0)])

    @pl.when(jnp.logical_not(in_use))
    def _():
        ys_ref[...] = jnp.zeros_like(ys_ref)

    @pl.when(jnp.logical_and(in_use, new_expert))
    def _():
        wg_s[...] = wg_ref[0, 0].astype(BF16)
        wu_s[...] = wu_ref[0, 0].astype(BF16)
        wd_s[...] = wd_ref[0, 0].astype(BF16)

    @pl.when(in_use)
    def _():
        x = _load_token_tiles(xs_ref, tm).astype(BF16)
        gl = jnp.dot(x, wg_s[...], preferred_element_type=F32) + bg_ref[0, 0]
        up = jnp.dot(x, wu_s[...], preferred_element_type=F32) + bu_ref[0, 0]
        gl = jnp.minimum(gl, SWIGLU_LIMIT)
        up = jnp.clip(up, -SWIGLU_LIMIT, SWIGLU_LIMIT)
        act = gl * (1.0 / (1.0 + jnp.exp(-SWIGLU_ALPHA * gl)))
        hmid = ((up + 1.0) * act).astype(BF16)
        y = jnp.dot(hmid, wd_s[...], preferred_element_type=F32) + bd_ref[0, 0]
        ys_ref[...] = y.astype(ys_ref.dtype)


def _experts(meta, xs, layer, wg, bg, wu, bu, wd, bd):
    cap = xs.shape[0] // ROW_CHUNKS
    _, _, d, de = wg.shape
    tm = TM_EXPERT
    n_blocks = cap // tm

    def blk(i, meta):
        return jnp.minimum(i, meta[1, 0] - 1)

    rows = lambda i, meta: (blk(i, meta), 0)
    wmap = lambda i, meta: (layer, meta[0, blk(i, meta)], 0, 0)
    grid_spec = pltpu.PrefetchScalarGridSpec(
        num_scalar_prefetch=1,
        grid=(n_blocks,),
        in_specs=[pl.BlockSpec((tm * ROW_CHUNKS, LANES), rows),
                  pl.BlockSpec((1, 1, d, de), wmap),
                  pl.BlockSpec((1, 1, 1, de), wmap),
                  pl.BlockSpec((1, 1, d, de), wmap),
                  pl.BlockSpec((1, 1, 1, de), wmap),
                  pl.BlockSpec((1, 1, de, d), wmap),
                  pl.BlockSpec((1, 1, 1, d), wmap)],
        out_specs=pl.BlockSpec((tm, d), lambda i, meta: (i, 0)),
        scratch_shapes=[pltpu.VMEM((d, de), BF16), pltpu.VMEM((d, de), BF16),
                        pltpu.VMEM((de, d), BF16)],
    )
    return pl.pallas_call(
        _experts_kernel,
        grid_spec=grid_spec,
        out_shape=jax.ShapeDtypeStruct((cap, d), BF16),
        compiler_params=_cparams(("arbitrary",)),
        name="experts",
    )(meta, xs, wg, bg, wu, bu, wd, bd)


def _combine_kernel(win_ref, dest_ref, idx_ref, gate_ref, x1_ref, ys_ref, o_ref, buf_ref, one_ref,
                    sem, one_sem, *, n_exp):
    i = pl.program_id(0)
    n_tiles = pl.num_programs(0)
    tm = x1_ref.shape[0]
    last_window = ys_ref.shape[0] - WINDOW_ROWS

    def window_start(tile, e, rnd):
        return jnp.minimum(win_ref[tile, e] + rnd * WINDOW_ROWS, last_window)

    def ys_window(tile, e, rnd):
        return ys_ref.at[pl.ds(pl.multiple_of(window_start(tile, e, rnd), BF16_ROWS), WINDOW_ROWS)]

    def buf_window(slot, w):
        return buf_ref.at[slot, pl.ds(pl.multiple_of(w * WINDOW_ROWS, WINDOW_ROWS), WINDOW_ROWS)]

    def for_spills(tile, fn):
        def body(e, j):
            hit = jnp.logical_and(win_ref[tile, n_exp + e] > 1, j < SPILL_WINDOWS)

            @pl.when(hit)
            def _():
                fn(e, j)

            return j + hit.astype(I32)

        lax.fori_loop(0, n_exp, body, jnp.int32(0))

    def fetch(tile, slot):
        for e in range(n_exp):
            pltpu.make_async_copy(ys_window(tile, e, 0), buf_window(slot, e),
                                  sem.at[slot]).start(priority=e % 2)
        for_spills(tile, lambda e, j: pltpu.make_async_copy(
            ys_window(tile, e, 1), buf_window(slot, n_exp + j), sem.at[slot]).start())

    def drain(tile, slot):
        for e in range(n_exp):
            pltpu.make_async_copy(ys_window(tile, e, 0), buf_window(slot, e), sem.at[slot]).wait()
        for_spills(tile, lambda e, j: pltpu.make_async_copy(
            ys_window(tile, e, 1), buf_window(slot, n_exp + j), sem.at[slot]).wait())

    @pl.when(i == 0)
    def _():
        fetch(0, 0)

    @pl.when(i + 1 < n_tiles)
    def _():
        fetch(i + 1, (i + 1) % 2)

    idx = idx_ref[...]
    dest = dest_ref[...]
    gates = gate_ref[...]
    chunk = SEL_CHUNK_WINDOWS * WINDOW_ROWS
    lane = _lane_iota((tm, chunk)).astype(F32)

    first_tok = jnp.zeros(idx.shape, I32)
    clamped_tok = jnp.zeros(idx.shape, I32)
    for e in range(n_exp):
        first_tok = jnp.where(idx == e, win_ref[i, e], first_tok)
        clamped_tok = jnp.where(idx == e, window_start(i, e, 0), clamped_tok)
    local = dest - clamped_tok
    col = jnp.where(dest - first_tok < WINDOW_ROWS, idx * WINDOW_ROWS + local, -1)
    key = idx * KEY_STRIDE + (dest - first_tok)
    packed = jnp.concatenate([col.astype(F32), gates, key.astype(F32),
                              jnp.zeros((LANES - 3 * TOP_K, tm), F32)], axis=0)
    by_tok = jnp.transpose(packed)

    def spread(j, width):
        one = jnp.broadcast_to(by_tok[:, j:j + 1], (tm, LANES))
        return jnp.concatenate([one] * (width // LANES), axis=1) if width > LANES else one

    slot = i % 2
    drain(i, slot)
    cols = [spread(k, chunk) for k in range(TOP_K)]
    gts = [spread(TOP_K + k, chunk) for k in range(TOP_K)]
    total = x1_ref[...]
    for c in range(n_exp // SEL_CHUNK_WINDOWS):
        sel = jnp.zeros((tm, chunk), F32)
        for k in range(TOP_K):
            sel = jnp.where(lane == cols[k] - float(c * chunk), gts[k], sel)
        total = total + jnp.dot(sel.astype(BF16), buf_ref[slot, c * chunk:(c + 1) * chunk, :],
                                preferred_element_type=F32)
    o_ref[...] = total

    lane1 = _lane_iota((tm, LANES))
    keys1 = [spread(2 * TOP_K + k, LANES) for k in range(TOP_K)]
    gts1 = [spread(TOP_K + k, LANES) for k in range(TOP_K)]

    def add_window(e, rnd, rows):
        shift = win_ref[i, e] + rnd * WINDOW_ROWS - window_start(i, e, rnd)
        want = jnp.where(lane1 >= shift, lane1 + (e * KEY_STRIDE + rnd * WINDOW_ROWS - shift),
                         -1).astype(F32)
        sel = jnp.zeros((tm, LANES), F32)
        for k in range(TOP_K):
            sel = jnp.where(want == keys1[k], gts1[k], sel)
        o_ref[...] += jnp.dot(sel.astype(BF16), rows, preferred_element_type=F32)

    for_spills(i, lambda e, j: add_window(e, 1, buf_window(slot, n_exp + j)[...]))

    def on_demand(e, j):
        n_win = win_ref[i, n_exp + e]
        hit = n_win > 1
        ahead = jnp.logical_and(hit, j < SPILL_WINDOWS)

        def one(rnd, c):
            cp = pltpu.make_async_copy(ys_window(i, e, rnd), one_ref, one_sem)
            cp.start()
            cp.wait()
            add_window(e, rnd, one_ref[...])
            return c

        lax.fori_loop(jnp.where(ahead, 2, 1), n_win, one, 0)
        return j + hit.astype(I32)

    lax.fori_loop(0, n_exp, on_demand, jnp.int32(0))


def _combine(win, dest, idx, gates, x1, ys, n_exp):
    n, d = x1.shape
    tm = min(TM_OUT, n)
    vec = lambda i, win: (0, i)
    grid_spec = pltpu.PrefetchScalarGridSpec(
        num_scalar_prefetch=1,
        grid=(n // tm,),
        in_specs=[pl.BlockSpec((TOP_K, tm), vec),
                  pl.BlockSpec((TOP_K, tm), vec),
                  pl.BlockSpec((TOP_K, tm), vec),
                  pl.BlockSpec((tm, d), lambda i, win: (i, 0)),
                  pl.BlockSpec(memory_space=pl.ANY)],
        out_specs=pl.BlockSpec((tm, d), lambda i, win: (i, 0)),
        scratch_shapes=[pltpu.VMEM((2, (n_exp + SPILL_WINDOWS) * WINDOW_ROWS, d), ys.dtype),
                        pltpu.VMEM((WINDOW_ROWS, d), ys.dtype),
                        pltpu.SemaphoreType.DMA((2,)),
                        pltpu.SemaphoreType.DMA],
    )
    return pl.pallas_call(
        functools.partial(_combine_kernel, n_exp=n_exp),
        grid_spec=grid_spec,
        out_shape=jax.ShapeDtypeStruct((n, d), F32),
        compiler_params=_cparams(("arbitrary",)),
        name="combine",
    )(win, dest, idx, gates, x1, ys)


def _pad_lanes(v, width=LANES):
    v = v.astype(F32).reshape(1, -1)
    return jnp.pad(v, ((0, 0), (0, width - v.shape[1])))


def _layer(x2, mem, bsz, seq, p, layer, experts):
    n, d = x2.shape
    f32 = F32
    w_in = p['w_in']
    off_q = 2 * GROUP_WIDTH
    off_k, off_v, off_f = off_q + GROUP_WIDTH, off_q + 2 * GROUP_WIDTH, off_q + 3 * GROUP_WIDTH
    off_qm = off_f + N_HEADS
    w_f = w_in[:, off_f:off_qm]
    w_f_pad = jnp.pad(jnp.concatenate([w_f, w_f, w_f], axis=1), ((0, 0), (0, LANES - 3 * N_HEADS)))
    w_all = jnp.concatenate([w_in[:, :off_f], w_in[:, off_qm:], w_f_pad], axis=1).astype(BF16)
    fb = p['fox_forget_b'].astype(f32)
    fb_pad = _pad_lanes(jnp.concatenate([fb, fb, fb]))
    tm_in = min(TM_INPROJ, seq)
    tri = jnp.tril(jnp.ones((tm_in, tm_in), f32)).astype(BF16)

    sel = jnp.zeros((3, LANES, N_HEADS, LANES), f32)
    for part in range(3):
        for hd in range(N_HEADS):
            sel = sel.at[part, hd, hd, HEAD_DIM + part].set(1.0)
            sel = sel.at[part, hd, hd, HEAD_DIM + 3 + part].set(-1.0)
    sel = sel.reshape(3 * LANES, N_HEADS * LANES).astype(BF16)
    head_of = jnp.arange(GROUP_WIDTH) // HEAD_DIM
    ones_blk = (head_of[:, None] == head_of[None, :]).astype(BF16)
    ones2 = jnp.concatenate([ones_blk, ones_blk], axis=0)

    def head_gain(g):
        return jnp.tile(g.astype(f32).reshape(1, HEAD_DIM), (1, N_HEADS))

    ua, ub, qa, ka, va, qm = _inproj(
        x2, seq, p['mix_norm_g'].reshape(1, d).astype(f32), w_all,
        head_gain(p['fox_q_g']), head_gain(p['fox_k_g']), head_gain(p['mem_q_g']), fb_pad, tri,
        sel, ones2)

    bmat, coef, ssm_pw, cmat = _s5_constants(
        p['ssm_lambda_re'], p['ssm_lambda_im'], p['ssm_log_dt'], p['ssm_b_re'], p['ssm_b_im'],
        p['ssm_c_re'], p['ssm_c_im'], min(T_S5, seq))
    ya = _s5(ua, bsz, seq, bmat, coef, ssm_pw, cmat,
             p['ssm_d'].reshape(1, GROUP_WIDTH).astype(f32), p['ssm_glu_w'].astype(BF16),
             p['ssm_glu_b'].reshape(1, GROUP_WIDTH).astype(f32),
             p['ssm_out_g'].reshape(1, GROUP_WIDTH).astype(f32))

    pw = p['pool_w'].astype(f32)
    w_blk = jnp.zeros((GROUP_WIDTH, GROUP_WIDTH), f32)
    for gi in range(len(POOL_WINDOWS)):
        w_blk = w_blk.at[gi * HEAD_DIM:(gi + 1) * HEAD_DIM, gi * HEAD_DIM:(gi + 1) * HEAD_DIM].set(pw[gi])
    yb = _pool(ub, bsz, seq, w_blk.astype(BF16), p['pool_scale'].reshape(1, GROUP_WIDTH).astype(f32))

    yc = _fox(qa, ka, va, bsz, seq, p['fox_out_g'].reshape(1, GROUP_WIDTH).astype(f32))

    mk, mv = _memkv(mem, p['mem_norm_g'].reshape(1, d).astype(f32), p['mem_w_kv'].astype(BF16),
                    _pad_lanes(p['mem_k_g']))
    ym = _memattn(qm, mk, mv, bsz, seq, p['mem_out_g'].reshape(1, GROUP_WIDTH).astype(f32))

    n_exp = p['router_w'].shape[1]
    tm_out = min(TM_OUT, n)
    ustrict = jnp.triu(jnp.ones((tm_out, tm_out), f32), k=1).astype(BF16)
    rw = p['router_w'].astype(f32)
    rw_hi = rw.astype(BF16)
    rw_lo = (rw - rw_hi.astype(f32)).astype(BF16)
    rw_parts = jnp.pad(jnp.concatenate([rw_hi, rw_lo], axis=1), ((0, 0), (0, LANES - 2 * n_exp)))
    x1, hn, idx, gates, rank, cnt, toff = _outproj(
        x2, ya, yb, yc, ym, p['w_out'].astype(BF16), p['ffn_norm_g'].reshape(1, d).astype(f32),
        rw_parts, p['router_b'].reshape(n_exp, 1).astype(f32), ustrict)

    cap = n * TOP_K + n_exp * TM_EXPERT
    cnt_i = cnt[:, 0].astype(I32)
    dest, meta, starts, win = _plan(cnt_i, idx, rank, toff, cap // TM_EXPERT)
    xs = _dispatch(dest, cnt_i, starts, hn, cap)
    ys = _experts(meta, xs, layer, *experts)
    win_tab = win[::SUBLANES, :2 * n_exp]
    return _combine(win_tab, dest, idx, gates, x1, ys, n_exp)


_PARAM_NAMES = ('mix_norm_g', 'w_in', 'ssm_lambda_re', 'ssm_lambda_im', 'ssm_log_dt',
                'ssm_b_re', 'ssm_b_im', 'ssm_c_re', 'ssm_c_im', 'ssm_d', 'ssm_glu_w', 'ssm_glu_b',
                'ssm_out_g', 'pool_w', 'pool_scale', 'fox_forget_b', 'fox_q_g', 'fox_k_g',
                'fox_out_g', 'mem_norm_g', 'mem_w_kv', 'mem_q_g', 'mem_k_g', 'mem_out_g', 'w_out',
                'ffn_norm_g', 'router_w', 'router_b', 'exp_w_gate', 'exp_b_gate', 'exp_w_up',
                'exp_b_up', 'exp_w_down', 'exp_b_down')


def kernel(x, mem, mix_norm_g, w_in, ssm_lambda_re, ssm_lambda_im, ssm_log_dt, ssm_b_re, ssm_b_im,
           ssm_c_re, ssm_c_im, ssm_d, ssm_glu_w, ssm_glu_b, ssm_out_g, pool_w, pool_scale,
           fox_forget_b, fox_q_g, fox_k_g, fox_out_g, mem_norm_g, mem_w_kv, mem_q_g, mem_k_g,
           mem_out_g, w_out, ffn_norm_g, router_w, router_b, exp_w_gate, exp_b_gate, exp_w_up,
           exp_b_up, exp_w_down, exp_b_down):
    stacked = dict(zip(_PARAM_NAMES, (
        mix_norm_g, w_in, ssm_lambda_re, ssm_lambda_im, ssm_log_dt, ssm_b_re, ssm_b_im, ssm_c_re,
        ssm_c_im, ssm_d, ssm_glu_w, ssm_glu_b, ssm_out_g, pool_w, pool_scale, fox_forget_b,
        fox_q_g, fox_k_g, fox_out_g, mem_norm_g, mem_w_kv, mem_q_g, mem_k_g, mem_out_g, w_out,
        ffn_norm_g, router_w, router_b, exp_w_gate, exp_b_gate, exp_w_up, exp_b_up, exp_w_down,
        exp_b_down)))
    bsz, seq, d = x.shape
    depth = w_in.shape[0]
    x2 = x.reshape(bsz * seq, d).astype(F32)
    mem = mem.astype(F32)
    expert_names = ('exp_w_gate', 'exp_b_gate', 'exp_w_up', 'exp_b_up', 'exp_w_down', 'exp_b_down')
    experts = tuple(stacked[k].astype(F32) if stacked[k].ndim == 4
                    else stacked[k].astype(F32)[:, :, None, :] for k in expert_names)
    for layer in range(depth):
        x2 = _layer(x2, mem, bsz, seq,
                    {k: v[layer] for k, v in stacked.items() if k not in expert_names},
                    layer, experts)
    return x2.reshape(bsz, seq, d).astype(x.dtype)
```

```python
import functools
import math

import jax
import jax.numpy as jnp
from jax import lax
from jax.experimental import pallas as pl
from jax.experimental.pallas import tpu as pltpu

F32 = jnp.float32
BF16 = jnp.bfloat16
I32 = jnp.int32

EPS = 1e-6
HEAD_DIM = 64
N_HEADS = 4
GROUP_WIDTH = 256
LANES = 128
SUBLANES = 8
ROW_CHUNKS = 8
SSM_GROUPS = 16
SSM_CH = 16
SSM_STATE = 64
SSM_LANES = SSM_GROUPS * SSM_STATE
POOL_WINDOWS = (2, 4, 8, 16)
POOL_HALO = 16
TOP_K = 4
SWIGLU_LIMIT = 7.0
SWIGLU_ALPHA = 1.702
VMEM_LIMIT = 56 * 1024 * 1024

TM_INPROJ = 512
T_S5 = 512
T_POOL = 512
T_ATT = 512
T_ATT_Q = 1024
TM_OUT = 512
TM_EXPERT = 512
T_MOVE = 256
PAD_CHUNK = 64
BF16_ROWS = 16
WINDOW_SHIFT = 7
WINDOW_ROWS = 1 << WINDOW_SHIFT
SEL_CHUNK_WINDOWS = 8
SPILL_WINDOWS = 8
KEY_STRIDE = 1 << 16

NEG_INF = float("-inf")
LOG2E = 1.4426950408889634


def _cparams(sem):
    return pltpu.CompilerParams(dimension_semantics=sem, vmem_limit_bytes=VMEM_LIMIT)


def _lane_iota(shape):
    return lax.broadcasted_iota(I32, shape, len(shape) - 1)


def _split3(x):
    hi = x.astype(BF16).astype(F32)
    r = x - hi
    mid = r.astype(BF16).astype(F32)
    lo = r - mid
    return hi, mid, lo


def _head_slab(slab, odd):
    lane = _lane_iota(slab.shape)
    if odd:
        slab = pltpu.roll(slab, HEAD_DIM, axis=1)
    return jnp.where(lane < HEAD_DIM, slab, 0.0)


def _head_norm(xh, gain):
    ss = jnp.sum(xh * xh, axis=-1, keepdims=True)
    return xh * jnp.broadcast_to(lax.rsqrt(ss * (1.0 / HEAD_DIM) + EPS), xh.shape) * gain


def _divide_by_denominator(acc):
    inv = 1.0 / acc[:, HEAD_DIM:HEAD_DIM + 1]
    return acc * jnp.broadcast_to(inv, acc.shape)


def _join_heads(o_even, o_odd):
    lane = _lane_iota(o_even.shape)
    return jnp.where(lane < HEAD_DIM, o_even, pltpu.roll(o_odd, HEAD_DIM, axis=1))


def _store_token_tiles(ref, val):
    t = val.shape[0]
    for s in range(ROW_CHUNKS):
        ref[pl.ds(s, t, stride=ROW_CHUNKS), :] = val[:, s * LANES:(s + 1) * LANES]


def _load_token_tiles(ref, t):
    return jnp.concatenate([ref[pl.ds(s, t, stride=ROW_CHUNKS), :] for s in range(ROW_CHUNKS)],
                           axis=1)


def _group_norm_pairs(pairs, gain_ref, out_ref):
    ss = jnp.sum(sum(p * p for p in pairs), axis=-1, keepdims=True)
    scale = jnp.broadcast_to(lax.rsqrt(ss * (1.0 / GROUP_WIDTH) + EPS), pairs[0].shape)
    for i, p in enumerate(pairs):
        sl = slice(i * LANES, (i + 1) * LANES)
        out_ref[:, sl] = (p * scale * gain_ref[:, sl]).astype(out_ref.dtype)


COL_A, COL_B, COL_Q, COL_K, COL_V, COL_QM, COL_F = 0, 256, 512, 768, 1024, 1280, 1536
IN_COLS_PADDED = COL_F + LANES


def _inproj_kernel(x_ref, g_ref, w_ref, qg_ref, kg_ref, mqg_ref, fb_ref, tri_ref, sel_ref, ones_ref,
                   ua_ref, ub_ref, qa_ref, ka_ref, vat_ref, qm_ref, carry_ref, *, tiles_per_seq):
    i = pl.program_id(0)

    @pl.when(i % tiles_per_seq == 0)
    def _():
        carry_ref[...] = jnp.zeros_like(carry_ref)

    x = x_ref[...]
    h = x * lax.rsqrt(jnp.mean(x * x, axis=-1, keepdims=True) + EPS) * g_ref[...]
    proj = jnp.dot(h.astype(BF16), w_ref[...], preferred_element_type=F32)
    ua_ref[0] = proj[:, COL_A:COL_A + LANES]
    ua_ref[1] = proj[:, COL_A + LANES:COL_A + GROUP_WIDTH]
    ub_ref[...] = proj[:, COL_B:COL_B + GROUP_WIDTH]

    z = proj[:, COL_F:COL_F + LANES] + fb_ref[...]
    lane = _lane_iota(z.shape)
    logf = jnp.minimum(z, 0.0) - jnp.log(1.0 + jnp.exp(-jnp.abs(z)))
    hi, mid, lo = _split3(logf)
    packed = jnp.where(lane < 4, hi, jnp.where(lane < 8, mid, jnp.where(lane < 12, lo, 0.0)))
    cs = jnp.dot(tri_ref[...], packed.astype(BF16), preferred_element_type=F32)
    cum = cs + pltpu.roll(cs, LANES - 4, axis=1) + pltpu.roll(cs, LANES - 8, axis=1)
    cum = cum + carry_ref[...]
    carry_ref[...] = cum[cum.shape[0] - 1:, :]

    scale = HEAD_DIM ** -0.5 * LOG2E
    c_parts = jnp.concatenate(_split3(cum * LOG2E), axis=1).astype(BF16)
    bias = jnp.dot(c_parts, sel_ref[...], preferred_element_type=F32)

    def heads_normed(col, gain_ref):
        t = proj[:, col:col + GROUP_WIDTH]
        sq = t * t
        sq_hi = sq.astype(BF16)
        sq_lo = (sq - sq_hi.astype(F32)).astype(BF16)
        ss = jnp.dot(jnp.concatenate([sq_hi, sq_lo], axis=1), ones_ref[...],
                     preferred_element_type=F32)
        return t * lax.rsqrt(ss * (1.0 / HEAD_DIM) + EPS) * gain_ref[...]

    qn_all = heads_normed(COL_Q, qg_ref) * scale
    kn_all = heads_normed(COL_K, kg_ref)
    mn_all = heads_normed(COL_QM, mqg_ref) * scale
    q_bias = (lane >= 64) & (lane < 67)
    k_bias = (lane >= 67) & (lane < 70)
    for hd in range(N_HEADS):
        pair, odd = hd // 2, hd % 2
        pair_sl = slice(pair * LANES, (pair + 1) * LANES)
        b = bias[:, hd * LANES:(hd + 1) * LANES]
        qn = _head_slab(qn_all[:, pair_sl], odd)
        qa_ref[hd] = jnp.where(q_bias, b, jnp.where(k_bias, 1.0, qn)).astype(BF16)
        kn = _head_slab(kn_all[:, pair_sl], odd)
        ka_ref[hd] = jnp.where(k_bias, b, jnp.where(q_bias, 1.0, kn)).astype(BF16)
        qm_ref[hd] = _head_slab(mn_all[:, pair_sl], odd).astype(BF16)
    tm = proj.shape[0]
    v_t = jnp.transpose(proj[:, COL_V:COL_V + GROUP_WIDTH])
    tail = jnp.where(lax.broadcasted_iota(I32, (HEAD_DIM, tm), 0) == 0, 1.0, 0.0)
    for hd in range(N_HEADS):
        vat_ref[hd] = jnp.concatenate([v_t[hd * HEAD_DIM:(hd + 1) * HEAD_DIM], tail],
                                      axis=0).astype(BF16)


def _inproj(x2, seq, mix_g, w_all, qg, kg, mqg, fb, tri, sel, ones):
    n, d = x2.shape
    tm = min(TM_INPROJ, seq)
    grid = (n // tm,)
    const = lambda i: (0, 0)
    row = lambda i: (i, 0)
    hrow = lambda i: (0, i, 0)
    out_shape = (
        jax.ShapeDtypeStruct((2, n, LANES), F32),
        jax.ShapeDtypeStruct((n, GROUP_WIDTH), F32),
        jax.ShapeDtypeStruct((N_HEADS, n, LANES), BF16),
        jax.ShapeDtypeStruct((N_HEADS, n, LANES), BF16),
        jax.ShapeDtypeStruct((N_HEADS, LANES, n), BF16),
        jax.ShapeDtypeStruct((N_HEADS, n, LANES), BF16),
    )
    return pl.pallas_call(
        functools.partial(_inproj_kernel, tiles_per_seq=seq // tm),
        grid=grid,
        in_specs=[
            pl.BlockSpec((tm, d), row),
            pl.BlockSpec((1, d), const),
            pl.BlockSpec((d, IN_COLS_PADDED), const),
            pl.BlockSpec((1, GROUP_WIDTH), const),
            pl.BlockSpec((1, GROUP_WIDTH), const),
            pl.BlockSpec((1, GROUP_WIDTH), const),
            pl.BlockSpec((1, LANES), const),
            pl.BlockSpec((tm, tm), const),
            pl.BlockSpec((3 * LANES, N_HEADS * LANES), const),
            pl.BlockSpec((2 * GROUP_WIDTH, GROUP_WIDTH), const),
        ],
        out_specs=(
            pl.BlockSpec((2, tm, LANES), hrow),
            pl.BlockSpec((tm, GROUP_WIDTH), row),
            pl.BlockSpec((N_HEADS, tm, LANES), hrow),
            pl.BlockSpec((N_HEADS, tm, LANES), hrow),
            pl.BlockSpec((N_HEADS, LANES, tm), lambda i: (0, 0, i)),
            pl.BlockSpec((N_HEADS, tm, LANES), hrow),
        ),
        out_shape=out_shape,
        scratch_shapes=[pltpu.VMEM((1, LANES), F32)],
        compiler_params=_cparams(("arbitrary",)),
        name="inproj",
    )(x2, mix_g, w_all, qg, kg, mqg, fb, tri, sel, ones)


def _s5_kernel(u_ref, bmat_ref, coef_ref, pw_ref, cmat_ref, d_ref, gluw_ref, glub_ref, og_ref,
               o_ref, u_scr, x_scr, y_scr, carry_ref):
    @pl.when(pl.program_id(1) == 0)
    def _():
        carry_ref[...] = jnp.zeros_like(carry_ref)

    t = u_ref.shape[1]
    g = t // SUBLANES
    halves = [slice(h * LANES, (h + 1) * LANES) for h in range(GROUP_WIDTH // LANES)]
    for j in range(g):
        for h, sl in enumerate(halves):
            u_scr[j * SUBLANES:(j + 1) * SUBLANES, sl] = u_ref[h, pl.ds(j, SUBLANES, stride=g), :]
    u = u_scr[...]
    x_scr[...] = jnp.dot(u.astype(BF16), bmat_ref[...], preferred_element_type=F32)
    n_lane_blocks = SSM_LANES // LANES
    re_sl = [slice(lb * LANES, (lb + 1) * LANES) for lb in range(n_lane_blocks)]
    im_sl = [slice(SSM_LANES + lb * LANES, SSM_LANES + (lb + 1) * LANES)
             for lb in range(n_lane_blocks)]

    def group_rows(j):
        return pl.ds(pl.multiple_of(j * SUBLANES, SUBLANES), SUBLANES)

    def pass1(j, h):
        rows = group_rows(j)
        new = []
        for lb in range(n_lane_blocks):
            ar = coef_ref[0, :, re_sl[lb]]
            ai = coef_ref[1, :, re_sl[lb]]
            hr, hi = h[2 * lb], h[2 * lb + 1]
            nr = ar * hr - ai * hi + x_scr[rows, re_sl[lb]]
            ni = ar * hi + ai * hr + x_scr[rows, im_sl[lb]]
            x_scr[rows, re_sl[lb]] = nr
            x_scr[rows, im_sl[lb]] = ni
            new += [nr, ni]
        return tuple(new)

    zero = jnp.zeros((SUBLANES, LANES), F32)
    finals = lax.fori_loop(0, g, pass1, (zero,) * (2 * n_lane_blocks))

    row_id = lax.broadcasted_iota(I32, (SUBLANES, LANES), 0)
    starts = []
    for lb in range(n_lane_blocks):
        fr, fi = finals[2 * lb], finals[2 * lb + 1]
        for s, k in enumerate((1, 2, 4)):
            cr = coef_ref[2 + 2 * s, :, re_sl[lb]]
            ci = coef_ref[3 + 2 * s, :, re_sl[lb]]
            sr = pltpu.roll(fr, k, axis=0)
            si = pltpu.roll(fi, k, axis=0)
            fr, fi = fr + cr * sr - ci * si, fi + cr * si + ci * sr
        qr = coef_ref[8, :, re_sl[lb]]
        qi = coef_ref[9, :, re_sl[lb]]
        cbr = carry_ref[:, re_sl[lb]]
        cbi = carry_ref[:, im_sl[lb]]
        fr, fi = fr + qr * cbr - qi * cbi, fi + qr * cbi + qi * cbr
        starts.append(jnp.where(row_id == 0, cbr, pltpu.roll(fr, 1, axis=0)))
        starts.append(jnp.where(row_id == 0, cbi, pltpu.roll(fi, 1, axis=0)))
        carry_ref[:, re_sl[lb]] = jnp.broadcast_to(fr[SUBLANES - 1:, :], fr.shape)
        carry_ref[:, im_sl[lb]] = jnp.broadcast_to(fi[SUBLANES - 1:, :], fi.shape)

    def pass2(j, c):
        rows = group_rows(j)
        for lb in range(n_lane_blocks):
            pr = pw_ref[rows, re_sl[lb]]
            pi_ = pw_ref[rows, im_sl[lb]]
            sr, si = starts[2 * lb], starts[2 * lb + 1]
            x_scr[rows, re_sl[lb]] += pr * sr - pi_ * si
            x_scr[rows, im_sl[lb]] += pr * si + pi_ * sr
        return c

    lax.fori_loop(0, g, pass2, 0)

    y = jnp.dot(x_scr[...].astype(BF16), cmat_ref[...], preferred_element_type=F32) + d_ref[...] * u
    z = jax.nn.gelu(y, approximate=True)
    gate = jnp.dot(z.astype(BF16), gluw_ref[...], preferred_element_type=F32) + glub_ref[...]
    out = z * (1.0 / (1.0 + jnp.exp(-gate)))
    _group_norm_pairs([out[:, :LANES], out[:, LANES:]], og_ref, y_scr)
    for j in range(g):
        for h, sl in enumerate(halves):
            o_ref[h, pl.ds(j, SUBLANES, stride=g), :] = y_scr[j * SUBLANES:(j + 1) * SUBLANES, sl]


def _s5(ua, bsz, seq, bmat, coef, pw, cmat, dvec, gluw, glub, og):
    n = ua.shape[1]
    t = pw.shape[0]
    nt = seq // t
    row = lambda b, j: (0, b * nt + j, 0)
    c2 = lambda b, j: (0, 0)
    c3 = lambda b, j: (0, 0, 0)
    return pl.pallas_call(
        _s5_kernel,
        grid=(bsz, nt),
        in_specs=[
            pl.BlockSpec((2, t, LANES), row),
            pl.BlockSpec((GROUP_WIDTH, 2 * SSM_LANES), c2),
            pl.BlockSpec((10, SUBLANES, SSM_LANES), c3),
            pl.BlockSpec((t, 2 * SSM_LANES), c2),
            pl.BlockSpec((2 * SSM_LANES, GROUP_WIDTH), c2),
            pl.BlockSpec((1, GROUP_WIDTH), c2),
            pl.BlockSpec((GROUP_WIDTH, GROUP_WIDTH), c2),
            pl.BlockSpec((1, GROUP_WIDTH), c2),
            pl.BlockSpec((1, GROUP_WIDTH), c2),
        ],
        out_specs=pl.BlockSpec((2, t, LANES), row),
        out_shape=jax.ShapeDtypeStruct((2, n, LANES), F32),
        scratch_shapes=[pltpu.VMEM((t, GROUP_WIDTH), F32),
                        pltpu.VMEM((t, 2 * SSM_LANES), F32),
                        pltpu.VMEM((t, GROUP_WIDTH), F32),
                        pltpu.VMEM((SUBLANES, 2 * SSM_LANES), F32)],
        compiler_params=_cparams(("arbitrary", "arbitrary")),
        name="s5",
    )(ua, bmat, coef, pw, cmat, dvec, gluw, glub, og)


def _s5_constants(lam_re, lam_im, log_dt, b_re, b_im, c_re, c_im, tile):
    lr = lam_re.astype(F32)
    li = lam_im.astype(F32)
    dt = jnp.exp(log_dt.astype(F32))[:, None]
    mag = jnp.exp(lr * dt)
    a_re = mag * jnp.cos(li * dt)
    a_im = mag * jnp.sin(li * dt)
    den = lr * lr + li * li
    n_re = a_re - 1.0
    n_im = a_im
    k_re = (n_re * lr + n_im * li) / den
    k_im = (n_im * lr - n_re * li) / den
    br = b_re.astype(F32)
    bi = b_im.astype(F32)
    bb_re = k_re[..., None] * br - k_im[..., None] * bi
    bb_im = k_re[..., None] * bi + k_im[..., None] * br
    eye = jnp.eye(SSM_GROUPS, dtype=F32)
    bm_re = jnp.einsum('gph,gk->ghkp', bb_re, eye).reshape(GROUP_WIDTH, SSM_LANES)
    bm_im = jnp.einsum('gph,gk->ghkp', bb_im, eye).reshape(GROUP_WIDTH, SSM_LANES)
    bmat = jnp.concatenate([bm_re, bm_im], axis=1).astype(BF16)
    cm_re = jnp.einsum('ghp,gk->gpkh', c_re.astype(F32), eye).reshape(SSM_LANES, GROUP_WIDTH)
    cm_im = jnp.einsum('ghp,gk->gpkh', c_im.astype(F32), eye).reshape(SSM_LANES, GROUP_WIDTH)
    cmat = jnp.concatenate([cm_re, -cm_im], axis=0).astype(BF16)

    ar = a_re.reshape(1, SSM_LANES)
    ai = a_im.reshape(1, SSM_LANES)

    def cmul(x, y):
        return x[0] * y[0] - x[1] * y[1], x[0] * y[1] + x[1] * y[0]

    groups = tile // SUBLANES
    pows = [(ar, ai)]
    for _ in range(groups - 1):
        pows.append(cmul(pows[-1], (ar, ai)))
    pw = jnp.concatenate([jnp.repeat(jnp.concatenate([p[0] for p in pows], axis=0), SUBLANES, axis=0),
                          jnp.repeat(jnp.concatenate([p[1] for p in pows], axis=0), SUBLANES, axis=0)],
                         axis=1)
    chunk = pows[groups - 1]
    cpows = [chunk]
    for _ in range(SUBLANES - 1):
        cpows.append(cmul(cpows[-1], chunk))
    rows = jnp.arange(SUBLANES, dtype=I32)[:, None]
    ones = jnp.ones((SUBLANES, 1), F32)
    planes = [ones * ar, ones * ai]
    for k in (1, 2, 4):
        m = (rows >= k).astype(F32)
        planes += [m * cpows[k - 1][0], m * cpows[k - 1][1]]
    planes += [jnp.concatenate([p[0] for p in cpows], axis=0),
               jnp.concatenate([p[1] for p in cpows], axis=0)]
    coef = jnp.stack(planes, axis=0)
    return bmat, coef, pw, cmat


def _pool_kernel(v_ref, w_ref, g_ref, o_ref, ext_ref):
    j = pl.program_id(1)
    t = v_ref.shape[0]

    @pl.when(j == 0)
    def _():
        ext_ref[0:POOL_HALO, :] = jnp.zeros((POOL_HALO, GROUP_WIDTH), F32)

    v = v_ref[...]
    ext_ref[POOL_HALO:POOL_HALO + t, :] = v
    cur = ext_ref[...]
    width = 1
    wins = {}
    while width < POOL_WINDOWS[-1]:
        cur = cur + pltpu.roll(cur, width, axis=0)
        width *= 2
        wins[width] = cur[POOL_HALO:, :]
    lane = _lane_iota(v.shape)
    pos = (j * t + lax.broadcasted_iota(I32, v.shape, 0) + 1).astype(F32)
    mean = None
    for gi, w in enumerate(POOL_WINDOWS):
        m = wins[w] / jnp.minimum(pos, float(w))
        mean = m if mean is None else jnp.where(lane >= gi * HEAD_DIM, m, mean)
    mixed = jnp.dot((mean - v).astype(BF16), w_ref[...], preferred_element_type=F32)
    _group_norm_pairs([mixed[:, :LANES], mixed[:, LANES:]], g_ref, o_ref)
    ext_ref[0:POOL_HALO, :] = v[t - POOL_HALO:, :]


def _pool(ub, bsz, seq, w_blk, g):
    n = ub.shape[0]
    t = min(T_POOL, seq)
    nt = seq // t
    row = lambda b, j: (b * nt + j, 0)
    c2 = lambda b, j: (0, 0)
    return pl.pallas_call(
        _pool_kernel,
        grid=(bsz, nt),
        in_specs=[pl.BlockSpec((t, GROUP_WIDTH), row),
                  pl.BlockSpec((GROUP_WIDTH, GROUP_WIDTH), c2),
                  pl.BlockSpec((1, GROUP_WIDTH), c2)],
        out_specs=pl.BlockSpec((t, GROUP_WIDTH), row),
        out_shape=jax.ShapeDtypeStruct((n, GROUP_WIDTH), BF16),
        scratch_shapes=[pltpu.VMEM((t + POOL_HALO, GROUP_WIDTH), F32)],
        compiler_params=_cparams(("arbitrary", "arbitrary")),
        name="pool",
    )(ub, w_blk, g)


def _fox_kernel(qt_ref, kt_ref, qa_ref, ka_ref, vat_ref, g_ref, o_ref, m_ref, acc_ref):
    p_id = pl.program_id(1)
    qi = qt_ref[p_id]
    ki = kt_ref[p_id]
    tq = qa_ref.shape[1]
    tk = ka_ref.shape[1]

    @pl.when(ki == 0)
    def _():
        m_ref[...] = jnp.full_like(m_ref, NEG_INF)
        acc_ref[...] = jnp.zeros_like(acc_ref)

    def step(on_diagonal):
        if on_diagonal:
            causal = (lax.broadcasted_iota(I32, (tk, tq), 0) + (ki * tk - qi * tq)
                      <= lax.broadcasted_iota(I32, (tk, tq), 1))

        def scores(hd):
            return lax.dot_general(ka_ref[hd], qa_ref[hd], (((1,), (1,)), ((), ())),
                                   preferred_element_type=F32)

        s_next = scores(0)
        for hd in range(N_HEADS):
            s_t = s_next
            if hd + 1 < N_HEADS:
                s_next = scores(hd + 1)
            if on_diagonal:
                s_t = jnp.where(causal, s_t, NEG_INF)
            m_prev = m_ref[hd]
            m_new = jnp.maximum(m_prev, jnp.max(s_t, axis=0, keepdims=True))
            alpha = jnp.exp2(m_prev - m_new)
            p_t = jnp.exp2(s_t - m_new)
            acc_ref[hd] = alpha * acc_ref[hd] + jnp.dot(vat_ref[hd], p_t.astype(BF16),
                                                        preferred_element_type=F32)
            m_ref[hd] = m_new

    ratio = tq // tk
    first_masked = qi * ratio

    @pl.when(ki < first_masked)
    def _():
        step(False)

    @pl.when(ki >= first_masked)
    def _():
        step(True)

    @pl.when(ki == first_masked + ratio - 1)
    def _():
        heads = []
        for hd in range(N_HEADS):
            acc_t = acc_ref[hd]
            heads.append(acc_t[:HEAD_DIM] * (1.0 / acc_t[HEAD_DIM:HEAD_DIM + 1]))
        o_t = jnp.concatenate(heads, axis=0)
        ss = jnp.sum(o_t * o_t, axis=0, keepdims=True)
        o_t = o_t * lax.rsqrt(ss * (1.0 / GROUP_WIDTH) + EPS)
        o_ref[...] = (jnp.transpose(o_t) * g_ref[...]).astype(o_ref.dtype)


def _fox(qa, ka, vat, bsz, seq, g):
    n = qa.shape[1]
    tk = min(T_ATT, seq)
    tq = min(T_ATT_Q, seq)
    nq, nk = seq // tq, seq // tk
    ratio = tq // tk
    pairs = [(q, k) for q in range(nq) for k in range(ratio * (q + 1))]
    qt = jnp.asarray([p[0] for p in pairs], I32)
    kt = jnp.asarray([p[1] for p in pairs], I32)
    qmap = lambda b, p, qt, kt: (0, b * nq + qt[p], 0)
    kmap = lambda b, p, qt, kt: (0, b * nk + kt[p], 0)
    grid_spec = pltpu.PrefetchScalarGridSpec(
        num_scalar_prefetch=2,
        grid=(bsz, len(pairs)),
        in_specs=[pl.BlockSpec((N_HEADS, tq, LANES), qmap),
                  pl.BlockSpec((N_HEADS, tk, LANES), kmap),
                  pl.BlockSpec((N_HEADS, LANES, tk), lambda b, p, qt, kt: (0, 0, b * nk + kt[p])),
                  pl.BlockSpec((1, GROUP_WIDTH), lambda b, p, qt, kt: (0, 0))],
        out_specs=pl.BlockSpec((tq, GROUP_WIDTH), lambda b, p, qt, kt: (b * nq + qt[p], 0)),
        scratch_shapes=[pltpu.VMEM((N_HEADS, 1, tq), F32),
                        pltpu.VMEM((N_HEADS, LANES, tq), F32)],
    )
    return pl.pallas_call(
        _fox_kernel,
        grid_spec=grid_spec,
        out_shape=jax.ShapeDtypeStruct((n, GROUP_WIDTH), BF16),
        compiler_params=_cparams(("arbitrary", "arbitrary")),
        name="fox",
    )(qt, kt, qa, ka, vat, g)


def _memkv_kernel(mem_ref, g_ref, w_ref, kg_ref, mk_ref, mvt_ref):
    x = mem_ref[0]
    h = x * lax.rsqrt(jnp.mean(x * x, axis=-1, keepdims=True) + EPS) * g_ref[...]
    kv = jnp.dot(h.astype(BF16), w_ref[...], preferred_element_type=F32)
    m = x.shape[0]
    v_t = jnp.transpose(kv[:, GROUP_WIDTH:])
    tail = jnp.where(lax.broadcasted_iota(I32, (HEAD_DIM, m), 0) == 0, 1.0, 0.0)
    for hd in range(N_HEADS):
        pair, odd = hd // 2, hd % 2
        ks = _head_slab(kv[:, pair * LANES:(pair + 1) * LANES], odd)
        mk_ref[0, hd] = _head_norm(ks, kg_ref[...]).astype(BF16)
        mvt_ref[0, hd] = jnp.concatenate([v_t[hd * HEAD_DIM:(hd + 1) * HEAD_DIM], tail],
                                         axis=0).astype(BF16)


def _memkv(mem, g, w_kv, kg):
    bsz, m, d = mem.shape
    c2 = lambda b: (0, 0)
    out = jax.ShapeDtypeStruct((bsz, N_HEADS, m, LANES), BF16)
    out_t = jax.ShapeDtypeStruct((bsz, N_HEADS, LANES, m), BF16)
    return pl.pallas_call(
        _memkv_kernel,
        grid=(bsz,),
        in_specs=[pl.BlockSpec((1, m, d), lambda b: (b, 0, 0)),
                  pl.BlockSpec((1, d), c2),
                  pl.BlockSpec((d, 2 * GROUP_WIDTH), c2),
                  pl.BlockSpec((1, LANES), c2)],
        out_specs=(pl.BlockSpec((1, N_HEADS, m, LANES), lambda b: (b, 0, 0, 0)),
                   pl.BlockSpec((1, N_HEADS, LANES, m), lambda b: (b, 0, 0, 0))),
        out_shape=(out, out_t),
        compiler_params=_cparams(("arbitrary",)),
        name="memkv",
    )(mem, g, w_kv, kg)


def _memattn_kernel(qm_ref, mk_ref, mvt_ref, g_ref, o_ref):
    heads = []
    for hd in range(N_HEADS):
        s_t = lax.dot_general(mk_ref[0, hd], qm_ref[hd], (((1,), (1,)), ((), ())),
                              preferred_element_type=F32)
        p_t = jnp.exp2(s_t - jnp.max(s_t, axis=0, keepdims=True))
        acc_t = jnp.dot(mvt_ref[0, hd], p_t.astype(BF16), preferred_element_type=F32)
        heads.append(acc_t[:HEAD_DIM] * (1.0 / acc_t[HEAD_DIM:HEAD_DIM + 1]))
    o_t = jnp.concatenate(heads, axis=0)
    ss = jnp.sum(o_t * o_t, axis=0, keepdims=True)
    o_t = o_t * lax.rsqrt(ss * (1.0 / GROUP_WIDTH) + EPS)
    o_ref[...] = (jnp.transpose(o_t) * g_ref[...]).astype(o_ref.dtype)


def _memattn(qm, mk, mv, bsz, seq, g):
    n = qm.shape[1]
    m = mk.shape[2]
    t = min(T_ATT, seq)
    nt = seq // t
    return pl.pallas_call(
        _memattn_kernel,
        grid=(bsz, nt),
        in_specs=[pl.BlockSpec((N_HEADS, t, LANES), lambda b, j: (0, b * nt + j, 0)),
                  pl.BlockSpec((1, N_HEADS, m, LANES), lambda b, j: (b, 0, 0, 0)),
                  pl.BlockSpec((1, N_HEADS, LANES, m), lambda b, j: (b, 0, 0, 0)),
                  pl.BlockSpec((1, GROUP_WIDTH), lambda b, j: (0, 0))],
        out_specs=pl.BlockSpec((t, GROUP_WIDTH), lambda b, j: (b * nt + j, 0)),
        out_shape=jax.ShapeDtypeStruct((n, GROUP_WIDTH), BF16),
        compiler_params=_cparams(("arbitrary", "arbitrary")),
        name="memattn",
    )(qm, mk, mv, g)


def _outproj_kernel(x_ref, ya_ref, yb_ref, yc_ref, ym_ref, w_ref, fg_ref, rw_ref, rb_ref, us_ref,
                    x1_ref, hn_ref, idx_ref, gate_ref, rank_ref, cnt_ref, toff_ref, carry_ref):
    @pl.when(pl.program_id(0) == 0)
    def _():
        carry_ref[...] = jnp.zeros_like(carry_ref)

    merged = jnp.concatenate([ya_ref[0].astype(BF16), ya_ref[1].astype(BF16), yb_ref[...],
                              yc_ref[...], ym_ref[...]], axis=1)
    acc = x_ref[...] + jnp.dot(merged, w_ref[...], preferred_element_type=F32)
    x1_ref[...] = acc
    hn = acc * lax.rsqrt(jnp.mean(acc * acc, axis=-1, keepdims=True) + EPS) * fg_ref[...]
    _store_token_tiles(hn_ref, hn)
    n_exp = rb_ref.shape[0]
    tm = hn.shape[0]
    hn_hi = hn.astype(BF16)
    hn_lo = (hn - hn_hi.astype(F32)).astype(BF16)
    parts = jnp.dot(jnp.concatenate([hn_hi, hn_lo], axis=0), rw_ref[...],
                    preferred_element_type=F32)
    top = jnp.transpose(parts[:tm])
    bot = jnp.transpose(parts[tm:])
    logits = (top[:n_exp] + top[n_exp:2 * n_exp] + bot[:n_exp] + bot[n_exp:2 * n_exp]
              + rb_ref[...])
    e_iota = lax.broadcasted_iota(I32, (n_exp, tm), 0).astype(F32)
    work = logits
    vals, onehots = [], []
    for k in range(TOP_K):
        m = jnp.max(work, axis=0, keepdims=True)
        sel = jnp.min(jnp.where(work == m, e_iota, float(n_exp)), axis=0, keepdims=True)
        hot = e_iota == sel
        idx_ref[k:k + 1, :] = sel.astype(I32)
        vals.append(m)
        onehots.append(hot.astype(F32))
        work = jnp.where(hot, NEG_INF, work)
    exps = [jnp.exp(v - vals[0]) for v in vals]
    denom = exps[0] + exps[1] + exps[2] + exps[3]
    for k in range(TOP_K):
        gate_ref[k:k + 1, :] = exps[k] / denom
    stacked = jnp.concatenate(onehots, axis=0).astype(BF16)
    prefix = jnp.dot(stacked, us_ref[...], preferred_element_type=F32)
    base = carry_ref[...]
    base_sq = jnp.concatenate([base, jnp.zeros((LANES - n_exp, LANES), F32)], axis=0)
    toff_ref[...] = jnp.transpose(base_sq)[:SUBLANES, :]
    for k in range(TOP_K):
        hot = onehots[k]
        pk = prefix[k * n_exp:(k + 1) * n_exp, :]
        rank = jnp.sum(hot * (pk + base[:, 0:1]), axis=0, keepdims=True)
        rank_ref[k:k + 1, :] = rank.astype(I32)
        base = base + jnp.sum(hot, axis=1, keepdims=True)
    carry_ref[...] = base
    cnt_ref[...] = base


def _outproj(x2, ya, yb, yc, ym, w_out, fg, rw, rb, ustrict):
    n, d = x2.shape
    n_exp = rb.shape[0]
    tm = min(TM_OUT, n)
    row = lambda i: (i, 0)
    col = lambda i: (0, i)
    const = lambda i: (0, 0)
    return pl.pallas_call(
        _outproj_kernel,
        grid=(n // tm,),
        in_specs=[pl.BlockSpec((tm, d), row)]
        + [pl.BlockSpec((2, tm, LANES), lambda i: (0, i, 0))]
        + [pl.BlockSpec((tm, GROUP_WIDTH), row)] * 3
        + [pl.BlockSpec((d, d), const),
           pl.BlockSpec((1, d), const),
           pl.BlockSpec((d, LANES), const),
           pl.BlockSpec((n_exp, 1), const),
           pl.BlockSpec((tm, tm), const)],
        out_specs=(pl.BlockSpec((tm, d), row),
                   pl.BlockSpec((tm * ROW_CHUNKS, LANES), row),
                   pl.BlockSpec((TOP_K, tm), col),
                   pl.BlockSpec((TOP_K, tm), col),
                   pl.BlockSpec((TOP_K, tm), col),
                   pl.BlockSpec((n_exp, LANES), const),
                   pl.BlockSpec((SUBLANES, LANES), row)),
        out_shape=(jax.ShapeDtypeStruct((n, d), F32),
                   jax.ShapeDtypeStruct((n * ROW_CHUNKS, LANES), F32),
                   jax.ShapeDtypeStruct((TOP_K, n), I32),
                   jax.ShapeDtypeStruct((TOP_K, n), F32),
                   jax.ShapeDtypeStruct((TOP_K, n), I32),
                   jax.ShapeDtypeStruct((n_exp, LANES), F32),
                   jax.ShapeDtypeStruct((n // tm * SUBLANES, LANES), F32)),
        scratch_shapes=[pltpu.VMEM((n_exp, LANES), F32)],
        compiler_params=_cparams(("arbitrary",)),
        name="outproj",
    )(x2, ya, yb, yc, ym, w_out, fg, rw, rb, ustrict)


def _plan_kernel(cnt_ref, idx_ref, rank_ref, toff_ref, dest_ref, meta_ref, start_ref, win_ref,
                 *, n_exp, block_rows):
    def body(e, off):
        start_ref[e] = off
        return off + (cnt_ref[e] + block_rows - 1) // block_rows * block_rows

    total = lax.fori_loop(0, n_exp, body, jnp.int32(0))
    idx = idx_ref[...]
    dest = rank_ref[...]
    blk_start = lax.broadcasted_iota(I32, meta_ref.shape, 1) * block_rows
    blk_e = jnp.zeros(meta_ref.shape, I32)
    lane = _lane_iota((SUBLANES, LANES))
    start_l = jnp.zeros((SUBLANES, LANES), I32)
    cnt_l = jnp.zeros((SUBLANES, LANES), I32)
    for e in range(n_exp):
        dest = dest + jnp.where(idx == e, start_ref[e], 0)
        start_l = jnp.where(lane == e, start_ref[e], start_l)
        cnt_l = jnp.where(lane == e, cnt_ref[e], cnt_l)
        if e > 0:
            blk_e = blk_e + (blk_start >= start_ref[e]).astype(I32)
    dest_ref[...] = dest
    sub = lax.broadcasted_iota(I32, meta_ref.shape, 0)
    meta_ref[...] = jnp.where(sub == 0, blk_e, total // block_rows)

    toff = toff_ref[...].astype(I32)
    n_rows = toff.shape[0]
    nxt = jnp.concatenate([toff[SUBLANES:], cnt_l], axis=0) if n_rows > SUBLANES else cnt_l
    first = jnp.concatenate([start_l] * (n_rows // SUBLANES), axis=0) + toff
    aligned = jnp.bitwise_and(first, -BF16_ROWS)
    lanes = _lane_iota(toff.shape)
    span = jnp.where(lanes < n_exp, first - aligned + (nxt - toff), 0)
    n_win = jnp.right_shift(span + (WINDOW_ROWS - 1), WINDOW_SHIFT)
    win_ref[...] = jnp.where(lanes < n_exp, aligned, pltpu.roll(n_win, n_exp, axis=1))


def _plan(cnt, idx, rank, toff, n_blocks):
    n_exp = cnt.shape[0]
    nb_pad = (n_blocks + LANES - 1) // LANES * LANES
    return pl.pallas_call(
        functools.partial(_plan_kernel, n_exp=n_exp, block_rows=TM_EXPERT),
        in_specs=[pl.BlockSpec(memory_space=pltpu.SMEM),
                  pl.BlockSpec(memory_space=pltpu.VMEM),
                  pl.BlockSpec(memory_space=pltpu.VMEM),
                  pl.BlockSpec(memory_space=pltpu.VMEM)],
        out_specs=(pl.BlockSpec(memory_space=pltpu.VMEM),
                   pl.BlockSpec(memory_space=pltpu.VMEM),
                   pl.BlockSpec(memory_space=pltpu.SMEM),
                   pl.BlockSpec(memory_space=pltpu.VMEM)),
        out_shape=(jax.ShapeDtypeStruct(idx.shape, I32),
                   jax.ShapeDtypeStruct((SUBLANES, nb_pad), I32),
                   jax.ShapeDtypeStruct((n_exp,), I32),
                   jax.ShapeDtypeStruct(toff.shape, I32)),
        compiler_params=pltpu.CompilerParams(vmem_limit_bytes=VMEM_LIMIT),
        name="plan",
    )(cnt, idx, rank, toff)


def _tile_rows(row):
    return pl.ds(pl.multiple_of(row * ROW_CHUNKS, ROW_CHUNKS), ROW_CHUNKS)


def _dispatch_kernel(dest_ref, cnt_ref, start_ref, hn_ref, xs_ref, zero_ref, sem, zsem,
                     *, n_exp, block_rows):
    tm = dest_ref.shape[1]

    def pad_copy(e, r):
        return pltpu.make_async_copy(zero_ref.at[pl.ds(0, ROW_CHUNKS)],
                                     xs_ref.at[_tile_rows(start_ref[e] + r)], zsem)

    def pad_chunk_copy(e, c):
        rows = pl.ds(pl.multiple_of((start_ref[e] + c * PAD_CHUNK) * ROW_CHUNKS, PAD_CHUNK * ROW_CHUNKS),
                     PAD_CHUNK * ROW_CHUNKS)
        return pltpu.make_async_copy(zero_ref.at[pl.ds(0, PAD_CHUNK * ROW_CHUNKS)], xs_ref.at[rows],
                                     zsem)

    def pad_bounds(e):
        cnt = cnt_ref[e]
        hi = (cnt + block_rows - 1) // block_rows * block_rows
        mid = jnp.minimum((cnt + PAD_CHUNK - 1) // PAD_CHUNK * PAD_CHUNK, hi)
        return cnt, mid, hi

    @pl.when(pl.program_id(0) == 0)
    def _():
        zero_ref[...] = jnp.zeros_like(zero_ref)

        def pad_start(e, c):
            lo, mid, hi = pad_bounds(e)
            lax.fori_loop(lo, mid, lambda r, c: (pad_copy(e, r).start(), c)[1], c)
            return lax.fori_loop(mid // PAD_CHUNK, hi // PAD_CHUNK,
                                 lambda j, c: (pad_chunk_copy(e, j).start(), c)[1], c)

        def pad_wait(e, c):
            lo, mid, hi = pad_bounds(e)
            lax.fori_loop(lo, mid, lambda r, c: (pad_copy(e, r).wait(), c)[1], c)
            return lax.fori_loop(mid // PAD_CHUNK, hi // PAD_CHUNK,
                                 lambda j, c: (pad_chunk_copy(e, j).wait(), c)[1], c)

        lax.fori_loop(0, n_exp, pad_start, 0)
        lax.fori_loop(0, n_exp, pad_wait, 0)

        last = n_exp - 1
        used = start_ref[last] + pad_bounds(last)[2]
        n_blocks = xs_ref.shape[0] // (ROW_CHUNKS * block_rows)

        def tail_copy(b):
            rows = pl.ds(pl.multiple_of(b * (block_rows * ROW_CHUNKS), block_rows * ROW_CHUNKS),
                         block_rows * ROW_CHUNKS)
            return pltpu.make_async_copy(zero_ref, xs_ref.at[rows], zsem)

        lax.fori_loop(used // block_rows, n_blocks, lambda b, c: (tail_copy(b).start(), c)[1], 0)
        lax.fori_loop(used // block_rows, n_blocks, lambda b, c: (tail_copy(b).wait(), c)[1], 0)

    def copy(k, t):
        return pltpu.make_async_copy(hn_ref.at[_tile_rows(t)],
                                     xs_ref.at[_tile_rows(dest_ref[k, t])], sem)

    def start(t, c):
        for k in range(TOP_K):
            copy(k, t).start(priority=k % 2)
        return c

    def wait(t, c):
        for k in range(TOP_K):
            copy(k, t).wait()
        return c

    lax.fori_loop(0, tm, start, 0)
    lax.fori_loop(0, tm, wait, 0)


def _dispatch(dest, cnt, starts, hn, cap):
    n = hn.shape[0] // ROW_CHUNKS
    tm = min(T_MOVE, n)
    return pl.pallas_call(
        functools.partial(_dispatch_kernel, n_exp=cnt.shape[0], block_rows=TM_EXPERT),
        grid=(n // tm,),
        in_specs=[pl.BlockSpec((TOP_K, tm), lambda i: (0, i), memory_space=pltpu.SMEM),
                  pl.BlockSpec(memory_space=pltpu.SMEM),
                  pl.BlockSpec(memory_space=pltpu.SMEM),
                  pl.BlockSpec((tm * ROW_CHUNKS, LANES), lambda i: (i, 0))],
        out_specs=pl.BlockSpec(memory_space=pl.ANY),
        out_shape=jax.ShapeDtypeStruct((cap * ROW_CHUNKS, LANES), hn.dtype),
        scratch_shapes=[pltpu.VMEM((TM_EXPERT * ROW_CHUNKS, LANES), hn.dtype),
                        pltpu.SemaphoreType.DMA, pltpu.SemaphoreType.DMA],
        compiler_params=_cparams(("arbitrary",)),
        name="dispatch",
    )(dest, cnt, starts, hn)


def _experts_kernel(meta_ref, xs_ref, wg_ref, bg_ref, wu_ref, bu_ref, wd_ref, bd_ref, ys_ref,
                    wg_s, wu_s, wd_s):
    i = pl.program_id(0)
    tm = xs_ref.shape[0] // ROW_CHUNKS
    in_use = i < meta_ref[1, 0]
    new_expert = jnp.logical_or(i == 0, meta_ref[0, i] != meta_ref[0, jnp.maximum(i - 1, 0)])

    @pl.when(jnp.logical_not(in_use))
    def _():
        ys_ref[...] = jnp.zeros_like(ys_ref)

    @pl.when(jnp.logical_and(in_use, new_expert))
    def _():
        wg_s[...] = wg_ref[0, 0].astype(BF16)
        wu_s[...] = wu_ref[0, 0].astype(BF16)
        wd_s[...] = wd_ref[0, 0].astype(BF16)

    @pl.when(in_use)
    def _():
        x = _load_token_tiles(xs_ref, tm).astype(BF16)
        gl = jnp.dot(x, wg_s[...], preferred_element_type=F32) + bg_ref[0, 0]
        up = jnp.dot(x, wu_s[...], preferred_element_type=F32) + bu_ref[0, 0]
        gl = jnp.minimum(gl, SWIGLU_LIMIT)
        up = jnp.clip(up, -SWIGLU_LIMIT, SWIGLU_LIMIT)
        act = gl * (1.0 / (1.0 + jnp.exp(-SWIGLU_ALPHA * gl)))
        hmid = ((up + 1.0) * act).astype(BF16)
        y = jnp.dot(hmid, wd_s[...], preferred_element_type=F32) + bd_ref[0, 0]
        ys_ref[...] = y.astype(ys_ref.dtype)


def _experts(meta, xs, layer, wg, bg, wu, bu, wd, bd):
    cap = xs.shape[0] // ROW_CHUNKS
    _, _, d, de = wg.shape
    tm = TM_EXPERT
    n_blocks = cap // tm

    def blk(i, meta):
        return jnp.minimum(i, meta[1, 0] - 1)

    rows = lambda i, meta: (blk(i, meta), 0)
    wmap = lambda i, meta: (layer, meta[0, blk(i, meta)], 0, 0)
    grid_spec = pltpu.PrefetchScalarGridSpec(
        num_scalar_prefetch=1,
        grid=(n_blocks,),
        in_specs=[pl.BlockSpec((tm * ROW_CHUNKS, LANES), rows),
                  pl.BlockSpec((1, 1, d, de), wmap),
                  pl.BlockSpec((1, 1, 1, de), wmap),
                  pl.BlockSpec((1, 1, d, de), wmap),
                  pl.BlockSpec((1, 1, 1, de), wmap),
                  pl.BlockSpec((1, 1, de, d), wmap),
                  pl.BlockSpec((1, 1, 1, d), wmap)],
        out_specs=pl.BlockSpec((tm, d), lambda i, meta: (i, 0)),
        scratch_shapes=[pltpu.VMEM((d, de), BF16), pltpu.VMEM((d, de), BF16),
                        pltpu.VMEM((de, d), BF16)],
    )
    return pl.pallas_call(
        _experts_kernel,
        grid_spec=grid_spec,
        out_shape=jax.ShapeDtypeStruct((cap, d), BF16),
        compiler_params=_cparams(("arbitrary",)),
        name="experts",
    )(meta, xs, wg, bg, wu, bu, wd, bd)


def _combine_kernel(win_ref, dest_ref, idx_ref, gate_ref, x1_ref, ys_ref, o_ref, buf_ref, one_ref,
                    sem, one_sem, *, n_exp):
    i = pl.program_id(0)
    n_tiles = pl.num_programs(0)
    tm = x1_ref.shape[0]
    last_window = ys_ref.shape[0] - WINDOW_ROWS

    def window_start(tile, e, rnd):
        return jnp.minimum(win_ref[tile, e] + rnd * WINDOW_ROWS, last_window)

    def ys_window(tile, e, rnd):
        return ys_ref.at[pl.ds(pl.multiple_of(window_start(tile, e, rnd), BF16_ROWS), WINDOW_ROWS)]

    def buf_window(slot, w):
        return buf_ref.at[slot, pl.ds(pl.multiple_of(w * WINDOW_ROWS, WINDOW_ROWS), WINDOW_ROWS)]

    def for_spills(tile, fn):
        def body(e, j):
            hit = jnp.logical_and(win_ref[tile, n_exp + e] > 1, j < SPILL_WINDOWS)

            @pl.when(hit)
            def _():
                fn(e, j)

            return j + hit.astype(I32)

        lax.fori_loop(0, n_exp, body, jnp.int32(0))

    def fetch(tile, slot):
        for e in range(n_exp):
            pltpu.make_async_copy(ys_window(tile, e, 0), buf_window(slot, e),
                                  sem.at[slot]).start(priority=e % 2)
        for_spills(tile, lambda e, j: pltpu.make_async_copy(
            ys_window(tile, e, 1), buf_window(slot, n_exp + j), sem.at[slot]).start())

    def drain(tile, slot):
        for e in range(n_exp):
            pltpu.make_async_copy(ys_window(tile, e, 0), buf_window(slot, e), sem.at[slot]).wait()
        for_spills(tile, lambda e, j: pltpu.make_async_copy(
            ys_window(tile, e, 1), buf_window(slot, n_exp + j), sem.at[slot]).wait())

    @pl.when(i == 0)
    def _():
        fetch(0, 0)

    @pl.when(i + 1 < n_tiles)
    def _():
        fetch(i + 1, (i + 1) % 2)

    idx = idx_ref[...]
    dest = dest_ref[...]
    gates = gate_ref[...]
    chunk = SEL_CHUNK_WINDOWS * WINDOW_ROWS
    lane = _lane_iota((tm, chunk)).astype(F32)

    first_tok = jnp.zeros(idx.shape, I32)
    clamped_tok = jnp.zeros(idx.shape, I32)
    for e in range(n_exp):
        first_tok = jnp.where(idx == e, win_ref[i, e], first_tok)
        clamped_tok = jnp.where(idx == e, window_start(i, e, 0), clamped_tok)
    local = dest - clamped_tok
    col = jnp.where(dest - first_tok < WINDOW_ROWS, idx * WINDOW_ROWS + local, -1)
    key = idx * KEY_STRIDE + (dest - first_tok)
    packed = jnp.concatenate([col.astype(F32), gates, key.astype(F32),
                              jnp.zeros((LANES - 3 * TOP_K, tm), F32)], axis=0)
    by_tok = jnp.transpose(packed)

    def spread(j, width):
        one = jnp.broadcast_to(by_tok[:, j:j + 1], (tm, LANES))
        return jnp.concatenate([one] * (width // LANES), axis=1) if width > LANES else one

    slot = i % 2
    drain(i, slot)
    cols = [spread(k, chunk) for k in range(TOP_K)]
    gts = [spread(TOP_K + k, chunk) for k in range(TOP_K)]
    total = x1_ref[...]
    for c in range(n_exp // SEL_CHUNK_WINDOWS):
        sel = jnp.zeros((tm, chunk), F32)
        for k in range(TOP_K):
            sel = jnp.where(lane == cols[k] - float(c * chunk), gts[k], sel)
        total = total + jnp.dot(sel.astype(BF16), buf_ref[slot, c * chunk:(c + 1) * chunk, :],
                                preferred_element_type=F32)
    o_ref[...] = total

    lane1 = _lane_iota((tm, LANES))
    keys1 = [spread(2 * TOP_K + k, LANES) for k in range(TOP_K)]
    gts1 = [spread(TOP_K + k, LANES) for k in range(TOP_K)]

    def add_window(e, rnd, rows):
        shift = win_ref[i, e] + rnd * WINDOW_ROWS - window_start(i, e, rnd)
        want = jnp.where(lane1 >= shift, lane1 + (e * KEY_STRIDE + rnd * WINDOW_ROWS - shift),
                         -1).astype(F32)
        sel = jnp.zeros((tm, LANES), F32)
        for k in range(TOP_K):
            sel = jnp.where(want == keys1[k], gts1[k], sel)
        o_ref[...] += jnp.dot(sel.astype(BF16), rows, preferred_element_type=F32)

    for_spills(i, lambda e, j: add_window(e, 1, buf_window(slot, n_exp + j)[...]))

    def on_demand(e, j):
        n_win = win_ref[i, n_exp + e]
        hit = n_win > 1
        ahead = jnp.logical_and(hit, j < SPILL_WINDOWS)

        def one(rnd, c):
            cp = pltpu.make_async_copy(ys_window(i, e, rnd), one_ref, one_sem)
            cp.start()
            cp.wait()
            add_window(e, rnd, one_ref[...])
            return c

        lax.fori_loop(jnp.where(ahead, 2, 1), n_win, one, 0)
        return j + hit.astype(I32)

    lax.fori_loop(0, n_exp, on_demand, jnp.int32(0))


def _combine(win, dest, idx, gates, x1, ys, n_exp):
    n, d = x1.shape
    tm = min(TM_OUT, n)
    vec = lambda i, win: (0, i)
    grid_spec = pltpu.PrefetchScalarGridSpec(
        num_scalar_prefetch=1,
        grid=(n // tm,),
        in_specs=[pl.BlockSpec((TOP_K, tm), vec),
                  pl.BlockSpec((TOP_K, tm), vec),
                  pl.BlockSpec((TOP_K, tm), vec),
                  pl.BlockSpec((tm, d), lambda i, win: (i, 0)),
                  pl.BlockSpec(memory_space=pl.ANY)],
        out_specs=pl.BlockSpec((tm, d), lambda i, win: (i, 0)),
        scratch_shapes=[pltpu.VMEM((2, (n_exp + SPILL_WINDOWS) * WINDOW_ROWS, d), ys.dtype),
                        pltpu.VMEM((WINDOW_ROWS, d), ys.dtype),
                        pltpu.SemaphoreType.DMA((2,)),
                        pltpu.SemaphoreType.DMA],
    )
    return pl.pallas_call(
        functools.partial(_combine_kernel, n_exp=n_exp),
        grid_spec=grid_spec,
        out_shape=jax.ShapeDtypeStruct((n, d), F32),
        compiler_params=_cparams(("arbitrary",)),
        name="combine",
    )(win, dest, idx, gates, x1, ys)


def _pad_lanes(v, width=LANES):
    v = v.astype(F32).reshape(1, -1)
    return jnp.pad(v, ((0, 0), (0, width - v.shape[1])))


def _layer(x2, mem, bsz, seq, p, layer, experts):
    n, d = x2.shape
    f32 = F32
    w_in = p['w_in']
    off_q = 2 * GROUP_WIDTH
    off_k, off_v, off_f = off_q + GROUP_WIDTH, off_q + 2 * GROUP_WIDTH, off_q + 3 * GROUP_WIDTH
    off_qm = off_f + N_HEADS
    w_f = w_in[:, off_f:off_qm]
    w_f_pad = jnp.pad(jnp.concatenate([w_f, w_f, w_f], axis=1), ((0, 0), (0, LANES - 3 * N_HEADS)))
    w_all = jnp.concatenate([w_in[:, :off_f], w_in[:, off_qm:], w_f_pad], axis=1).astype(BF16)
    fb = p['fox_forget_b'].astype(f32)
    fb_pad = _pad_lanes(jnp.concatenate([fb, fb, fb]))
    tm_in = min(TM_INPROJ, seq)
    tri = jnp.tril(jnp.ones((tm_in, tm_in), f32)).astype(BF16)

    sel = jnp.zeros((3, LANES, N_HEADS, LANES), f32)
    for part in range(3):
        for hd in range(N_HEADS):
            sel = sel.at[part, hd, hd, HEAD_DIM + part].set(1.0)
            sel = sel.at[part, hd, hd, HEAD_DIM + 3 + part].set(-1.0)
    sel = sel.reshape(3 * LANES, N_HEADS * LANES).astype(BF16)
    head_of = jnp.arange(GROUP_WIDTH) // HEAD_DIM
    ones_blk = (head_of[:, None] == head_of[None, :]).astype(BF16)
    ones2 = jnp.concatenate([ones_blk, ones_blk], axis=0)

    def head_gain(g):
        return jnp.tile(g.astype(f32).reshape(1, HEAD_DIM), (1, N_HEADS))

    ua, ub, qa, ka, va, qm = _inproj(
        x2, seq, p['mix_norm_g'].reshape(1, d).astype(f32), w_all,
        head_gain(p['fox_q_g']), head_gain(p['fox_k_g']), head_gain(p['mem_q_g']), fb_pad, tri,
        sel, ones2)

    bmat, coef, ssm_pw, cmat = _s5_constants(
        p['ssm_lambda_re'], p['ssm_lambda_im'], p['ssm_log_dt'], p['ssm_b_re'], p['ssm_b_im'],
        p['ssm_c_re'], p['ssm_c_im'], min(T_S5, seq))
    ya = _s5(ua, bsz, seq, bmat, coef, ssm_pw, cmat,
             p['ssm_d'].reshape(1, GROUP_WIDTH).astype(f32), p['ssm_glu_w'].astype(BF16),
             p['ssm_glu_b'].reshape(1, GROUP_WIDTH).astype(f32),
             p['ssm_out_g'].reshape(1, GROUP_WIDTH).astype(f32))

    pw = p['pool_w'].astype(f32)
    w_blk = jnp.zeros((GROUP_WIDTH, GROUP_WIDTH), f32)
    for gi in range(len(POOL_WINDOWS)):
        w_blk = w_blk.at[gi * HEAD_DIM:(gi + 1) * HEAD_DIM, gi * HEAD_DIM:(gi + 1) * HEAD_DIM].set(pw[gi])
    yb = _pool(ub, bsz, seq, w_blk.astype(BF16), p['pool_scale'].reshape(1, GROUP_WIDTH).astype(f32))

    yc = _fox(qa, ka, va, bsz, seq, p['fox_out_g'].reshape(1, GROUP_WIDTH).astype(f32))

    mk, mv = _memkv(mem, p['mem_norm_g'].reshape(1, d).astype(f32), p['mem_w_kv'].astype(BF16),
                    _pad_lanes(p['mem_k_g']))
    ym = _memattn(qm, mk, mv, bsz, seq, p['mem_out_g'].reshape(1, GROUP_WIDTH).astype(f32))

    n_exp = p['router_w'].shape[1]
    tm_out = min(TM_OUT, n)
    ustrict = jnp.triu(jnp.ones((tm_out, tm_out), f32), k=1).astype(BF16)
    rw = p['router_w'].astype(f32)
    rw_hi = rw.astype(BF16)
    rw_lo = (rw - rw_hi.astype(f32)).astype(BF16)
    rw_parts = jnp.pad(jnp.concatenate([rw_hi, rw_lo], axis=1), ((0, 0), (0, LANES - 2 * n_exp)))
    x1, hn, idx, gates, rank, cnt, toff = _outproj(
        x2, ya, yb, yc, ym, p['w_out'].astype(BF16), p['ffn_norm_g'].reshape(1, d).astype(f32),
        rw_parts, p['router_b'].reshape(n_exp, 1).astype(f32), ustrict)

    cap = n * TOP_K + n_exp * TM_EXPERT
    cnt_i = cnt[:, 0].astype(I32)
    dest, meta, starts, win = _plan(cnt_i, idx, rank, toff, cap // TM_EXPERT)
    xs = _dispatch(dest, cnt_i, starts, hn, cap)
    ys = _experts(meta, xs, layer, *experts)
    win_tab = win[::SUBLANES, :2 * n_exp]
    return _combine(win_tab, dest, idx, gates, x1, ys, n_exp)


_PARAM_NAMES = ('mix_norm_g', 'w_in', 'ssm_lambda_re', 'ssm_lambda_im', 'ssm_log_dt',
                'ssm_b_re', 'ssm_b_im', 'ssm_c_re', 'ssm_c_im', 'ssm_d', 'ssm_glu_w', 'ssm_glu_b',
                'ssm_out_g', 'pool_w', 'pool_scale', 'fox_forget_b', 'fox_q_g', 'fox_k_g',
                'fox_out_g', 'mem_norm_g', 'mem_w_kv', 'mem_q_g', 'mem_k_g', 'mem_out_g', 'w_out',
                'ffn_norm_g', 'router_w', 'router_b', 'exp_w_gate', 'exp_b_gate', 'exp_w_up',
                'exp_b_up', 'exp_w_down', 'exp_b_down')


def kernel(x, mem, mix_norm_g, w_in, ssm_lambda_re, ssm_lambda_im, ssm_log_dt, ssm_b_re, ssm_b_im,
           ssm_c_re, ssm_c_im, ssm_d, ssm_glu_w, ssm_glu_b, ssm_out_g, pool_w, pool_scale,
           fox_forget_b, fox_q_g, fox_k_g, fox_out_g, mem_norm_g, mem_w_kv, mem_q_g, mem_k_g,
           mem_out_g, w_out, ffn_norm_g, router_w, router_b, exp_w_gate, exp_b_gate, exp_w_up,
           exp_b_up, exp_w_down, exp_b_down):
    stacked = dict(zip(_PARAM_NAMES, (
        mix_norm_g, w_in, ssm_lambda_re, ssm_lambda_im, ssm_log_dt, ssm_b_re, ssm_b_im, ssm_c_re,
        ssm_c_im, ssm_d, ssm_glu_w, ssm_glu_b, ssm_out_g, pool_w, pool_scale, fox_forget_b,
        fox_q_g, fox_k_g, fox_out_g, mem_norm_g, mem_w_kv, mem_q_g, mem_k_g, mem_out_g, w_out,
        ffn_norm_g, router_w, router_b, exp_w_gate, exp_b_gate, exp_w_up, exp_b_up, exp_w_down,
        exp_b_down)))
    bsz, seq, d = x.shape
    depth = w_in.shape[0]
    x2 = x.reshape(bsz * seq, d).astype(F32)
    mem = mem.astype(F32)
    expert_names = ('exp_w_gate', 'exp_b_gate', 'exp_w_up', 'exp_b_up', 'exp_w_down', 'exp_b_down')
    experts = tuple(stacked[k].astype(F32) if stacked[k].ndim == 4
                    else stacked[k].astype(F32)[:, :, None, :] for k in expert_names)
    for layer in range(depth):
        x2 = _layer(x2, mem, bsz, seq,
                    {k: v[layer] for k, v in stacked.items() if k not in expert_names},
                    layer, experts)
    return x2.reshape(bsz, seq, d).astype(x.dtype)
```

```python
import functools
import math

import jax
import jax.numpy as jnp
from jax import lax
from jax.experimental import pallas as pl
from jax.experimental.pallas import tpu as pltpu

F32 = jnp.float32
BF16 = jnp.bfloat16
I32 = jnp.int32

EPS = 1e-6
HEAD_DIM = 64
N_HEADS = 4
GROUP_WIDTH = 256
LANES = 128
SUBLANES = 8
ROW_CHUNKS = 8
SSM_GROUPS = 16
SSM_CH = 16
SSM_STATE = 64
SSM_LANES = SSM_GROUPS * SSM_STATE
POOL_WINDOWS = (2, 4, 8, 16)
POOL_HALO = 16
TOP_K = 4
SWIGLU_LIMIT = 7.0
SWIGLU_ALPHA = 1.702
VMEM_LIMIT = 56 * 1024 * 1024

TM_INPROJ = 512
T_S5 = 512
T_POOL = 512
T_ATT = 512
T_ATT_Q = 1024
TM_OUT = 512
TM_EXPERT = 512
T_MOVE = 256
PAD_CHUNK = 64
BF16_ROWS = 16
WINDOW_SHIFT = 7
WINDOW_ROWS = 1 << WINDOW_SHIFT
SEL_CHUNK_WINDOWS = 8
SPILL_WINDOWS = 16
KEY_STRIDE = 1 << 16

NEG_INF = float("-inf")
LOG2E = 1.4426950408889634


def _cparams(sem):
    return pltpu.CompilerParams(dimension_semantics=sem, vmem_limit_bytes=VMEM_LIMIT)


def _lane_iota(shape):
    return lax.broadcasted_iota(I32, shape, len(shape) - 1)


def _split3(x):
    hi = x.astype(BF16).astype(F32)
    r = x - hi
    mid = r.astype(BF16).astype(F32)
    lo = r - mid
    return hi, mid, lo


def _head_slab(slab, odd):
    lane = _lane_iota(slab.shape)
    if odd:
        slab = pltpu.roll(slab, HEAD_DIM, axis=1)
    return jnp.where(lane < HEAD_DIM, slab, 0.0)


def _head_norm(xh, gain):
    ss = jnp.sum(xh * xh, axis=-1, keepdims=True)
    return xh * jnp.broadcast_to(lax.rsqrt(ss * (1.0 / HEAD_DIM) + EPS), xh.shape) * gain


def _divide_by_denominator(acc):
    inv = 1.0 / acc[:, HEAD_DIM:HEAD_DIM + 1]
    return acc * jnp.broadcast_to(inv, acc.shape)


def _join_heads(o_even, o_odd):
    lane = _lane_iota(o_even.shape)
    return jnp.where(lane < HEAD_DIM, o_even, pltpu.roll(o_odd, HEAD_DIM, axis=1))


def _store_token_tiles(ref, val):
    t = val.shape[0]
    for s in range(ROW_CHUNKS):
        ref[pl.ds(s, t, stride=ROW_CHUNKS), :] = val[:, s * LANES:(s + 1) * LANES]


def _load_token_tiles(ref, t):
    return jnp.concatenate([ref[pl.ds(s, t, stride=ROW_CHUNKS), :] for s in range(ROW_CHUNKS)],
                           axis=1)


def _group_norm_pairs(pairs, gain_ref, out_ref):
    ss = jnp.sum(sum(p * p for p in pairs), axis=-1, keepdims=True)
    scale = jnp.broadcast_to(lax.rsqrt(ss * (1.0 / GROUP_WIDTH) + EPS), pairs[0].shape)
    for i, p in enumerate(pairs):
        sl = slice(i * LANES, (i + 1) * LANES)
        out_ref[:, sl] = (p * scale * gain_ref[:, sl]).astype(out_ref.dtype)


COL_A, COL_B, COL_Q, COL_K, COL_V, COL_QM, COL_F = 0, 256, 512, 768, 1024, 1280, 1536
IN_COLS_PADDED = COL_F + LANES


def _inproj_kernel(x_ref, g_ref, w_ref, qg_ref, kg_ref, mqg_ref, fb_ref, tri_ref, sel_ref, ones_ref,
                   ua_ref, ub_ref, qa_ref, ka_ref, vat_ref, qm_ref, carry_ref, *, tiles_per_seq):
    i = pl.program_id(0)

    @pl.when(i % tiles_per_seq == 0)
    def _():
        carry_ref[...] = jnp.zeros_like(carry_ref)

    x = x_ref[...]
    h = x * lax.rsqrt(jnp.mean(x * x, axis=-1, keepdims=True) + EPS) * g_ref[...]
    proj = jnp.dot(h.astype(BF16), w_ref[...], preferred_element_type=F32)
    ua_ref[0] = proj[:, COL_A:COL_A + LANES]
    ua_ref[1] = proj[:, COL_A + LANES:COL_A + GROUP_WIDTH]
    ub_ref[...] = proj[:, COL_B:COL_B + GROUP_WIDTH]

    z = proj[:, COL_F:COL_F + LANES] + fb_ref[...]
    lane = _lane_iota(z.shape)
    logf = jnp.minimum(z, 0.0) - jnp.log(1.0 + jnp.exp(-jnp.abs(z)))
    hi, mid, lo = _split3(logf)
    packed = jnp.where(lane < 4, hi, jnp.where(lane < 8, mid, jnp.where(lane < 12, lo, 0.0)))
    cs = jnp.dot(tri_ref[...], packed.astype(BF16), preferred_element_type=F32)
    cum = cs + pltpu.roll(cs, LANES - 4, axis=1) + pltpu.roll(cs, LANES - 8, axis=1)
    cum = cum + carry_ref[...]
    carry_ref[...] = cum[cum.shape[0] - 1:, :]

    scale = HEAD_DIM ** -0.5 * LOG2E
    c_parts = jnp.concatenate(_split3(cum * LOG2E), axis=1).astype(BF16)
    bias = jnp.dot(c_parts, sel_ref[...], preferred_element_type=F32)

    def heads_normed(col, gain_ref):
        t = proj[:, col:col + GROUP_WIDTH]
        sq = t * t
        sq_hi = sq.astype(BF16)
        sq_lo = (sq - sq_hi.astype(F32)).astype(BF16)
        ss = jnp.dot(jnp.concatenate([sq_hi, sq_lo], axis=1), ones_ref[...],
                     preferred_element_type=F32)
        return t * lax.rsqrt(ss * (1.0 / HEAD_DIM) + EPS) * gain_ref[...]

    qn_all = heads_normed(COL_Q, qg_ref) * scale
    kn_all = heads_normed(COL_K, kg_ref)
    mn_all = heads_normed(COL_QM, mqg_ref) * scale
    q_bias = (lane >= 64) & (lane < 67)
    k_bias = (lane >= 67) & (lane < 70)
    for hd in range(N_HEADS):
        pair, odd = hd // 2, hd % 2
        pair_sl = slice(pair * LANES, (pair + 1) * LANES)
        b = bias[:, hd * LANES:(hd + 1) * LANES]
        qn = _head_slab(qn_all[:, pair_sl], odd)
        qa_ref[hd] = jnp.where(q_bias, b, jnp.where(k_bias, 1.0, qn)).astype(BF16)
        kn = _head_slab(kn_all[:, pair_sl], odd)
        ka_ref[hd] = jnp.where(k_bias, b, jnp.where(q_bias, 1.0, kn)).astype(BF16)
        qm_ref[hd] = _head_slab(mn_all[:, pair_sl], odd).astype(BF16)
    tm = proj.shape[0]
    v_t = jnp.transpose(proj[:, COL_V:COL_V + GROUP_WIDTH])
    tail = jnp.where(lax.broadcasted_iota(I32, (HEAD_DIM, tm), 0) == 0, 1.0, 0.0)
    for hd in range(N_HEADS):
        vat_ref[hd] = jnp.concatenate([v_t[hd * HEAD_DIM:(hd + 1) * HEAD_DIM], tail],
                                      axis=0).astype(BF16)


def _inproj(x2, seq, mix_g, w_all, qg, kg, mqg, fb, tri, sel, ones):
    n, d = x2.shape
    tm = min(TM_INPROJ, seq)
    grid = (n // tm,)
    const = lambda i: (0, 0)
    row = lambda i: (i, 0)
    hrow = lambda i: (0, i, 0)
    out_shape = (
        jax.ShapeDtypeStruct((2, n, LANES), F32),
        jax.ShapeDtypeStruct((n, GROUP_WIDTH), F32),
        jax.ShapeDtypeStruct((N_HEADS, n, LANES), BF16),
        jax.ShapeDtypeStruct((N_HEADS, n, LANES), BF16),
        jax.ShapeDtypeStruct((N_HEADS, LANES, n), BF16),
        jax.ShapeDtypeStruct((N_HEADS, n, LANES), BF16),
    )
    return pl.pallas_call(
        functools.partial(_inproj_kernel, tiles_per_seq=seq // tm),
        grid=grid,
        in_specs=[
            pl.BlockSpec((tm, d), row),
            pl.BlockSpec((1, d), const),
            pl.BlockSpec((d, IN_COLS_PADDED), const),
            pl.BlockSpec((1, GROUP_WIDTH), const),
            pl.BlockSpec((1, GROUP_WIDTH), const),
            pl.BlockSpec((1, GROUP_WIDTH), const),
            pl.BlockSpec((1, LANES), const),
            pl.BlockSpec((tm, tm), const),
            pl.BlockSpec((3 * LANES, N_HEADS * LANES), const),
            pl.BlockSpec((2 * GROUP_WIDTH, GROUP_WIDTH), const),
        ],
        out_specs=(
            pl.BlockSpec((2, tm, LANES), hrow),
            pl.BlockSpec((tm, GROUP_WIDTH), row),
            pl.BlockSpec((N_HEADS, tm, LANES), hrow),
            pl.BlockSpec((N_HEADS, tm, LANES), hrow),
            pl.BlockSpec((N_HEADS, LANES, tm), lambda i: (0, 0, i)),
            pl.BlockSpec((N_HEADS, tm, LANES), hrow),
        ),
        out_shape=out_shape,
        scratch_shapes=[pltpu.VMEM((1, LANES), F32)],
        compiler_params=_cparams(("arbitrary",)),
        name="inproj",
    )(x2, mix_g, w_all, qg, kg, mqg, fb, tri, sel, ones)


def _s5_kernel(u_ref, bmat_ref, coef_ref, pw_ref, cmat_ref, d_ref, gluw_ref, glub_ref, og_ref,
               o_ref, u_scr, x_scr, y_scr, carry_ref):
    @pl.when(pl.program_id(1) == 0)
    def _():
        carry_ref[...] = jnp.zeros_like(carry_ref)

    t = u_ref.shape[1]
    g = t // SUBLANES
    halves = [slice(h * LANES, (h + 1) * LANES) for h in range(GROUP_WIDTH // LANES)]
    for j in range(g):
        for h, sl in enumerate(halves):
            u_scr[j * SUBLANES:(j + 1) * SUBLANES, sl] = u_ref[h, pl.ds(j, SUBLANES, stride=g), :]
    u = u_scr[...]
    x_scr[...] = jnp.dot(u.astype(BF16), bmat_ref[...], preferred_element_type=F32)
    n_lane_blocks = SSM_LANES // LANES
    re_sl = [slice(lb * LANES, (lb + 1) * LANES) for lb in range(n_lane_blocks)]
    im_sl = [slice(SSM_LANES + lb * LANES, SSM_LANES + (lb + 1) * LANES)
             for lb in range(n_lane_blocks)]

    def group_rows(j):
        return pl.ds(pl.multiple_of(j * SUBLANES, SUBLANES), SUBLANES)

    def pass1(j, h):
        rows = group_rows(j)
        new = []
        for lb in range(n_lane_blocks):
            ar = coef_ref[0, :, re_sl[lb]]
            ai = coef_ref[1, :, re_sl[lb]]
            hr, hi = h[2 * lb], h[2 * lb + 1]
            nr = ar * hr - ai * hi + x_scr[rows, re_sl[lb]]
            ni = ar * hi + ai * hr + x_scr[rows, im_sl[lb]]
            x_scr[rows, re_sl[lb]] = nr
            x_scr[rows, im_sl[lb]] = ni
            new += [nr, ni]
        return tuple(new)

    zero = jnp.zeros((SUBLANES, LANES), F32)
    finals = lax.fori_loop(0, g, pass1, (zero,) * (2 * n_lane_blocks))

    row_id = lax.broadcasted_iota(I32, (SUBLANES, LANES), 0)
    starts = []
    for lb in range(n_lane_blocks):
        fr, fi = finals[2 * lb], finals[2 * lb + 1]
        for s, k in enumerate((1, 2, 4)):
            cr = coef_ref[2 + 2 * s, :, re_sl[lb]]
            ci = coef_ref[3 + 2 * s, :, re_sl[lb]]
            sr = pltpu.roll(fr, k, axis=0)
            si = pltpu.roll(fi, k, axis=0)
            fr, fi = fr + cr * sr - ci * si, fi + cr * si + ci * sr
        qr = coef_ref[8, :, re_sl[lb]]
        qi = coef_ref[9, :, re_sl[lb]]
        cbr = carry_ref[:, re_sl[lb]]
        cbi = carry_ref[:, im_sl[lb]]
        fr, fi = fr + qr * cbr - qi * cbi, fi + qr * cbi + qi * cbr
        starts.append(jnp.where(row_id == 0, cbr, pltpu.roll(fr, 1, axis=0)))
        starts.append(jnp.where(row_id == 0, cbi, pltpu.roll(fi, 1, axis=0)))
        carry_ref[:, re_sl[lb]] = jnp.broadcast_to(fr[SUBLANES - 1:, :], fr.shape)
        carry_ref[:, im_sl[lb]] = jnp.broadcast_to(fi[SUBLANES - 1:, :], fi.shape)

    def pass2(j, c):
        rows = group_rows(j)
        for lb in range(n_lane_blocks):
            pr = pw_ref[rows, re_sl[lb]]
            pi_ = pw_ref[rows, im_sl[lb]]
            sr, si = starts[2 * lb], starts[2 * lb + 1]
            x_scr[rows, re_sl[lb]] += pr * sr - pi_ * si
            x_scr[rows, im_sl[lb]] += pr * si + pi_ * sr
        return c

    lax.fori_loop(0, g, pass2, 0)

    y = jnp.dot(x_scr[...].astype(BF16), cmat_ref[...], preferred_element_type=F32) + d_ref[...] * u
    z = jax.nn.gelu(y, approximate=True)
    gate = jnp.dot(z.astype(BF16), gluw_ref[...], preferred_element_type=F32) + glub_ref[...]
    out = z * (1.0 / (1.0 + jnp.exp(-gate)))
    _group_norm_pairs([out[:, :LANES], out[:, LANES:]], og_ref, y_scr)
    for j in range(g):
        for h, sl in enumerate(halves):
            o_ref[h, pl.ds(j, SUBLANES, stride=g), :] = y_scr[j * SUBLANES:(j + 1) * SUBLANES, sl]


def _s5(ua, bsz, seq, bmat, coef, pw, cmat, dvec, gluw, glub, og):
    n = ua.shape[1]
    t = pw.shape[0]
    nt = seq // t
    row = lambda b, j: (0, b * nt + j, 0)
    c2 = lambda b, j: (0, 0)
    c3 = lambda b, j: (0, 0, 0)
    return pl.pallas_call(
        _s5_kernel,
        grid=(bsz, nt),
        in_specs=[
            pl.BlockSpec((2, t, LANES), row),
            pl.BlockSpec((GROUP_WIDTH, 2 * SSM_LANES), c2),
            pl.BlockSpec((10, SUBLANES, SSM_LANES), c3),
            pl.BlockSpec((t, 2 * SSM_LANES), c2),
            pl.BlockSpec((2 * SSM_LANES, GROUP_WIDTH), c2),
            pl.BlockSpec((1, GROUP_WIDTH), c2),
            pl.BlockSpec((GROUP_WIDTH, GROUP_WIDTH), c2),
            pl.BlockSpec((1, GROUP_WIDTH), c2),
            pl.BlockSpec((1, GROUP_WIDTH), c2),
        ],
        out_specs=pl.BlockSpec((2, t, LANES), row),
        out_shape=jax.ShapeDtypeStruct((2, n, LANES), F32),
        scratch_shapes=[pltpu.VMEM((t, GROUP_WIDTH), F32),
                        pltpu.VMEM((t, 2 * SSM_LANES), F32),
                        pltpu.VMEM((t, GROUP_WIDTH), F32),
                        pltpu.VMEM((SUBLANES, 2 * SSM_LANES), F32)],
        compiler_params=_cparams(("arbitrary", "arbitrary")),
        name="s5",
    )(ua, bmat, coef, pw, cmat, dvec, gluw, glub, og)


def _s5_constants(lam_re, lam_im, log_dt, b_re, b_im, c_re, c_im, tile):
    lr = lam_re.astype(F32)
    li = lam_im.astype(F32)
    dt = jnp.exp(log_dt.astype(F32))[:, None]
    mag = jnp.exp(lr * dt)
    a_re = mag * jnp.cos(li * dt)
    a_im = mag * jnp.sin(li * dt)
    den = lr * lr + li * li
    n_re = a_re - 1.0
    n_im = a_im
    k_re = (n_re * lr + n_im * li) / den
    k_im = (n_im * lr - n_re * li) / den
    br = b_re.astype(F32)
    bi = b_im.astype(F32)
    bb_re = k_re[..., None] * br - k_im[..., None] * bi
    bb_im = k_re[..., None] * bi + k_im[..., None] * br
    eye = jnp.eye(SSM_GROUPS, dtype=F32)
    bm_re = jnp.einsum('gph,gk->ghkp', bb_re, eye).reshape(GROUP_WIDTH, SSM_LANES)
    bm_im = jnp.einsum('gph,gk->ghkp', bb_im, eye).reshape(GROUP_WIDTH, SSM_LANES)
    bmat = jnp.concatenate([bm_re, bm_im], axis=1).astype(BF16)
    cm_re = jnp.einsum('ghp,gk->gpkh', c_re.astype(F32), eye).reshape(SSM_LANES, GROUP_WIDTH)
    cm_im = jnp.einsum('ghp,gk->gpkh', c_im.astype(F32), eye).reshape(SSM_LANES, GROUP_WIDTH)
    cmat = jnp.concatenate([cm_re, -cm_im], axis=0).astype(BF16)

    ar = a_re.reshape(1, SSM_LANES)
    ai = a_im.reshape(1, SSM_LANES)

    def cmul(x, y):
        return x[0] * y[0] - x[1] * y[1], x[0] * y[1] + x[1] * y[0]

    groups = tile // SUBLANES
    pows = [(ar, ai)]
    for _ in range(groups - 1):
        pows.append(cmul(pows[-1], (ar, ai)))
    pw = jnp.concatenate([jnp.repeat(jnp.concatenate([p[0] for p in pows], axis=0), SUBLANES, axis=0),
                          jnp.repeat(jnp.concatenate([p[1] for p in pows], axis=0), SUBLANES, axis=0)],
                         axis=1)
    chunk = pows[groups - 1]
    cpows = [chunk]
    for _ in range(SUBLANES - 1):
        cpows.append(cmul(cpows[-1], chunk))
    rows = jnp.arange(SUBLANES, dtype=I32)[:, None]
    ones = jnp.ones((SUBLANES, 1), F32)
    planes = [ones * ar, ones * ai]
    for k in (1, 2, 4):
        m = (rows >= k).astype(F32)
        planes += [m * cpows[k - 1][0], m * cpows[k - 1][1]]
    planes += [jnp.concatenate([p[0] for p in cpows], axis=0),
               jnp.concatenate([p[1] for p in cpows], axis=0)]
    coef = jnp.stack(planes, axis=0)
    return bmat, coef, pw, cmat


def _pool_kernel(v_ref, w_ref, g_ref, o_ref, ext_ref):
    j = pl.program_id(1)
    t = v_ref.shape[0]

    @pl.when(j == 0)
    def _():
        ext_ref[0:POOL_HALO, :] = jnp.zeros((POOL_HALO, GROUP_WIDTH), F32)

    v = v_ref[...]
    ext_ref[POOL_HALO:POOL_HALO + t, :] = v
    cur = ext_ref[...]
    width = 1
    wins = {}
    while width < POOL_WINDOWS[-1]:
        cur = cur + pltpu.roll(cur, width, axis=0)
        width *= 2
        wins[width] = cur[POOL_HALO:, :]
    lane = _lane_iota(v.shape)
    pos = (j * t + lax.broadcasted_iota(I32, v.shape, 0) + 1).astype(F32)
    mean = None
    for gi, w in enumerate(POOL_WINDOWS):
        m = wins[w] / jnp.minimum(pos, float(w))
        mean = m if mean is None else jnp.where(lane >= gi * HEAD_DIM, m, mean)
    mixed = jnp.dot((mean - v).astype(BF16), w_ref[...], preferred_element_type=F32)
    _group_norm_pairs([mixed[:, :LANES], mixed[:, LANES:]], g_ref, o_ref)
    ext_ref[0:POOL_HALO, :] = v[t - POOL_HALO:, :]


def _pool(ub, bsz, seq, w_blk, g):
    n = ub.shape[0]
    t = min(T_POOL, seq)
    nt = seq // t
    row = lambda b, j: (b * nt + j, 0)
    c2 = lambda b, j: (0, 0)
    return pl.pallas_call(
        _pool_kernel,
        grid=(bsz, nt),
        in_specs=[pl.BlockSpec((t, GROUP_WIDTH), row),
                  pl.BlockSpec((GROUP_WIDTH, GROUP_WIDTH), c2),
                  pl.BlockSpec((1, GROUP_WIDTH), c2)],
        out_specs=pl.BlockSpec((t, GROUP_WIDTH), row),
        out_shape=jax.ShapeDtypeStruct((n, GROUP_WIDTH), BF16),
        scratch_shapes=[pltpu.VMEM((t + POOL_HALO, GROUP_WIDTH), F32)],
        compiler_params=_cparams(("arbitrary", "arbitrary")),
        name="pool",
    )(ub, w_blk, g)


def _fox_kernel(qt_ref, kt_ref, qa_ref, ka_ref, vat_ref, g_ref, o_ref, m_ref, acc_ref):
    p_id = pl.program_id(1)
    qi = qt_ref[p_id]
    ki = kt_ref[p_id]
    tq = qa_ref.shape[1]
    tk = ka_ref.shape[1]

    @pl.when(ki == 0)
    def _():
        m_ref[...] = jnp.full_like(m_ref, NEG_INF)
        acc_ref[...] = jnp.zeros_like(acc_ref)

    def step(on_diagonal):
        if on_diagonal:
            causal = (lax.broadcasted_iota(I32, (tk, tq), 0) + (ki * tk - qi * tq)
                      <= lax.broadcasted_iota(I32, (tk, tq), 1))

        def scores(hd):
            return lax.dot_general(ka_ref[hd], qa_ref[hd], (((1,), (1,)), ((), ())),
                                   preferred_element_type=F32)

        s_next = scores(0)
        for hd in range(N_HEADS):
            s_t = s_next
            if hd + 1 < N_HEADS:
                s_next = scores(hd + 1)
            if on_diagonal:
                s_t = jnp.where(causal, s_t, NEG_INF)
            m_prev = m_ref[hd]
            m_new = jnp.maximum(m_prev, jnp.max(s_t, axis=0, keepdims=True))
            alpha = jnp.exp2(m_prev - m_new)
            p_t = jnp.exp2(s_t - m_new)
            acc_ref[hd] = alpha * acc_ref[hd] + jnp.dot(vat_ref[hd], p_t.astype(BF16),
                                                        preferred_element_type=F32)
            m_ref[hd] = m_new

    ratio = tq // tk
    first_masked = qi * ratio

    @pl.when(ki < first_masked)
    def _():
        step(False)

    @pl.when(ki >= first_masked)
    def _():
        step(True)

    @pl.when(ki == first_masked + ratio - 1)
    def _():
        heads = []
        for hd in range(N_HEADS):
            acc_t = acc_ref[hd]
            heads.append(acc_t[:HEAD_DIM] * (1.0 / acc_t[HEAD_DIM:HEAD_DIM + 1]))
        o_t = jnp.concatenate(heads, axis=0)
        ss = jnp.sum(o_t * o_t, axis=0, keepdims=True)
        o_t = o_t * lax.rsqrt(ss * (1.0 / GROUP_WIDTH) + EPS)
        o_ref[...] = (jnp.transpose(o_t) * g_ref[...]).astype(o_ref.dtype)


def _fox(qa, ka, vat, bsz, seq, g):
    n = qa.shape[1]
    tk = min(T_ATT, seq)
    tq = min(T_ATT_Q, seq)
    nq, nk = seq // tq, seq // tk
    ratio = tq // tk
    pairs = [(q, k) for q in range(nq) for k in range(ratio * (q + 1))]
    qt = jnp.asarray([p[0] for p in pairs], I32)
    kt = jnp.asarray([p[1] for p in pairs], I32)
    qmap = lambda b, p, qt, kt: (0, b * nq + qt[p], 0)
    kmap = lambda b, p, qt, kt: (0, b * nk + kt[p], 0)
    grid_spec = pltpu.PrefetchScalarGridSpec(
        num_scalar_prefetch=2,
        grid=(bsz, len(pairs)),
        in_specs=[pl.BlockSpec((N_HEADS, tq, LANES), qmap),
                  pl.BlockSpec((N_HEADS, tk, LANES), kmap),
                  pl.BlockSpec((N_HEADS, LANES, tk), lambda b, p, qt, kt: (0, 0, b * nk + kt[p])),
                  pl.BlockSpec((1, GROUP_WIDTH), lambda b, p, qt, kt: (0, 0))],
        out_specs=pl.BlockSpec((tq, GROUP_WIDTH), lambda b, p, qt, kt: (b * nq + qt[p], 0)),
        scratch_shapes=[pltpu.VMEM((N_HEADS, 1, tq), F32),
                        pltpu.VMEM((N_HEADS, LANES, tq), F32)],
    )
    return pl.pallas_call(
        _fox_kernel,
        grid_spec=grid_spec,
        out_shape=jax.ShapeDtypeStruct((n, GROUP_WIDTH), BF16),
        compiler_params=_cparams(("arbitrary", "arbitrary")),
        name="fox",
    )(qt, kt, qa, ka, vat, g)


def _memkv_kernel(mem_ref, g_ref, w_ref, kg_ref, mk_ref, mvt_ref):
    x = mem_ref[0]
    h = x * lax.rsqrt(jnp.mean(x * x, axis=-1, keepdims=True) + EPS) * g_ref[...]
    kv = jnp.dot(h.astype(BF16), w_ref[...], preferred_element_type=F32)
    m = x.shape[0]
    v_t = jnp.transpose(kv[:, GROUP_WIDTH:])
    tail = jnp.where(lax.broadcasted_iota(I32, (HEAD_DIM, m), 0) == 0, 1.0, 0.0)
    for hd in range(N_HEADS):
        pair, odd = hd // 2, hd % 2
        ks = _head_slab(kv[:, pair * LANES:(pair + 1) * LANES], odd)
        mk_ref[0, hd] = _head_norm(ks, kg_ref[...]).astype(BF16)
        mvt_ref[0, hd] = jnp.concatenate([v_t[hd * HEAD_DIM:(hd + 1) * HEAD_DIM], tail],
                                         axis=0).astype(BF16)


def _memkv(mem, g, w_kv, kg):
    bsz, m, d = mem.shape
    c2 = lambda b: (0, 0)
    out = jax.ShapeDtypeStruct((bsz, N_HEADS, m, LANES), BF16)
    out_t = jax.ShapeDtypeStruct((bsz, N_HEADS, LANES, m), BF16)
    return pl.pallas_call(
        _memkv_kernel,
        grid=(bsz,),
        in_specs=[pl.BlockSpec((1, m, d), lambda b: (b, 0, 0)),
                  pl.BlockSpec((1, d), c2),
                  pl.BlockSpec((d, 2 * GROUP_WIDTH), c2),
                  pl.BlockSpec((1, LANES), c2)],
        out_specs=(pl.BlockSpec((1, N_HEADS, m, LANES), lambda b: (b, 0, 0, 0)),
                   pl.BlockSpec((1, N_HEADS, LANES, m), lambda b: (b, 0, 0, 0))),
        out_shape=(out, out_t),
        compiler_params=_cparams(("arbitrary",)),
        name="memkv",
    )(mem, g, w_kv, kg)


def _memattn_kernel(qm_ref, mk_ref, mvt_ref, g_ref, o_ref):
    heads = []
    for hd in range(N_HEADS):
        s_t = lax.dot_general(mk_ref[0, hd], qm_ref[hd], (((1,), (1,)), ((), ())),
                              preferred_element_type=F32)
        p_t = jnp.exp2(s_t - jnp.max(s_t, axis=0, keepdims=True))
        acc_t = jnp.dot(mvt_ref[0, hd], p_t.astype(BF16), preferred_element_type=F32)
        heads.append(acc_t[:HEAD_DIM] * (1.0 / acc_t[HEAD_DIM:HEAD_DIM + 1]))
    o_t = jnp.concatenate(heads, axis=0)
    ss = jnp.sum(o_t * o_t, axis=0, keepdims=True)
    o_t = o_t * lax.rsqrt(ss * (1.0 / GROUP_WIDTH) + EPS)
    o_ref[...] = (jnp.transpose(o_t) * g_ref[...]).astype(o_ref.dtype)


def _memattn(qm, mk, mv, bsz, seq, g):
    n = qm.shape[1]
    m = mk.shape[2]
    t = min(T_ATT, seq)
    nt = seq // t
    return pl.pallas_call(
        _memattn_kernel,
        grid=(bsz, nt),
        in_specs=[pl.BlockSpec((N_HEADS, t, LANES), lambda b, j: (0, b * nt + j, 0)),
                  pl.BlockSpec((1, N_HEADS, m, LANES), lambda b, j: (b, 0, 0, 0)),
                  pl.BlockSpec((1, N_HEADS, LANES, m), lambda b, j: (b, 0, 0, 0)),
                  pl.BlockSpec((1, GROUP_WIDTH), lambda b, j: (0, 0))],
        out_specs=pl.BlockSpec((t, GROUP_WIDTH), lambda b, j: (b * nt + j, 0)),
        out_shape=jax.ShapeDtypeStruct((n, GROUP_WIDTH), BF16),
        compiler_params=_cparams(("arbitrary", "arbitrary")),
        name="memattn",
    )(qm, mk, mv, g)


def _outproj_kernel(x_ref, ya_ref, yb_ref, yc_ref, ym_ref, w_ref, fg_ref, rw_ref, rb_ref, us_ref,
                    x1_ref, hn_ref, idx_ref, gate_ref, rank_ref, cnt_ref, toff_ref, carry_ref):
    @pl.when(pl.program_id(0) == 0)
    def _():
        carry_ref[...] = jnp.zeros_like(carry_ref)

    merged = jnp.concatenate([ya_ref[0].astype(BF16), ya_ref[1].astype(BF16), yb_ref[...],
                              yc_ref[...], ym_ref[...]], axis=1)
    acc = x_ref[...] + jnp.dot(merged, w_ref[...], preferred_element_type=F32)
    x1_ref[...] = acc
    hn = acc * lax.rsqrt(jnp.mean(acc * acc, axis=-1, keepdims=True) + EPS) * fg_ref[...]
    _store_token_tiles(hn_ref, hn)
    n_exp = rb_ref.shape[0]
    tm = hn.shape[0]
    hn_hi = hn.astype(BF16)
    hn_lo = (hn - hn_hi.astype(F32)).astype(BF16)
    parts = jnp.dot(jnp.concatenate([hn_hi, hn_lo], axis=0), rw_ref[...],
                    preferred_element_type=F32)
    top = jnp.transpose(parts[:tm])
    bot = jnp.transpose(parts[tm:])
    logits = (top[:n_exp] + top[n_exp:2 * n_exp] + bot[:n_exp] + bot[n_exp:2 * n_exp]
              + rb_ref[...])
    e_iota = lax.broadcasted_iota(I32, (n_exp, tm), 0).astype(F32)
    work = logits
    vals, onehots = [], []
    for k in range(TOP_K):
        m = jnp.max(work, axis=0, keepdims=True)
        sel = jnp.min(jnp.where(work == m, e_iota, float(n_exp)), axis=0, keepdims=True)
        hot = e_iota == sel
        idx_ref[k:k + 1, :] = sel.astype(I32)
        vals.append(m)
        onehots.append(hot.astype(F32))
        work = jnp.where(hot, NEG_INF, work)
    exps = [jnp.exp(v - vals[0]) for v in vals]
    denom = exps[0] + exps[1] + exps[2] + exps[3]
    for k in range(TOP_K):
        gate_ref[k:k + 1, :] = exps[k] / denom
    stacked = jnp.concatenate(onehots, axis=0).astype(BF16)
    prefix = jnp.dot(stacked, us_ref[...], preferred_element_type=F32)
    base = carry_ref[...]
    base_sq = jnp.concatenate([base, jnp.zeros((LANES - n_exp, LANES), F32)], axis=0)
    toff_ref[...] = jnp.transpose(base_sq)[:SUBLANES, :]
    for k in range(TOP_K):
        hot = onehots[k]
        pk = prefix[k * n_exp:(k + 1) * n_exp, :]
        rank = jnp.sum(hot * (pk + base[:, 0:1]), axis=0, keepdims=True)
        rank_ref[k:k + 1, :] = rank.astype(I32)
        base = base + jnp.sum(hot, axis=1, keepdims=True)
    carry_ref[...] = base
    cnt_ref[...] = base


def _outproj(x2, ya, yb, yc, ym, w_out, fg, rw, rb, ustrict):
    n, d = x2.shape
    n_exp = rb.shape[0]
    tm = min(TM_OUT, n)
    row = lambda i: (i, 0)
    col = lambda i: (0, i)
    const = lambda i: (0, 0)
    return pl.pallas_call(
        _outproj_kernel,
        grid=(n // tm,),
        in_specs=[pl.BlockSpec((tm, d), row)]
        + [pl.BlockSpec((2, tm, LANES), lambda i: (0, i, 0))]
        + [pl.BlockSpec((tm, GROUP_WIDTH), row)] * 3
        + [pl.BlockSpec((d, d), const),
           pl.BlockSpec((1, d), const),
           pl.BlockSpec((d, LANES), const),
           pl.BlockSpec((n_exp, 1), const),
           pl.BlockSpec((tm, tm), const)],
        out_specs=(pl.BlockSpec((tm, d), row),
                   pl.BlockSpec((tm * ROW_CHUNKS, LANES), row),
                   pl.BlockSpec((TOP_K, tm), col),
                   pl.BlockSpec((TOP_K, tm), col),
                   pl.BlockSpec((TOP_K, tm), col),
                   pl.BlockSpec((n_exp, LANES), const),
                   pl.BlockSpec((SUBLANES, LANES), row)),
        out_shape=(jax.ShapeDtypeStruct((n, d), F32),
                   jax.ShapeDtypeStruct((n * ROW_CHUNKS, LANES), F32),
                   jax.ShapeDtypeStruct((TOP_K, n), I32),
                   jax.ShapeDtypeStruct((TOP_K, n), F32),
                   jax.ShapeDtypeStruct((TOP_K, n), I32),
                   jax.ShapeDtypeStruct((n_exp, LANES), F32),
                   jax.ShapeDtypeStruct((n // tm * SUBLANES, LANES), F32)),
        scratch_shapes=[pltpu.VMEM((n_exp, LANES), F32)],
        compiler_params=_cparams(("arbitrary",)),
        name="outproj",
    )(x2, ya, yb, yc, ym, w_out, fg, rw, rb, ustrict)


def _plan_kernel(cnt_ref, idx_ref, rank_ref, toff_ref, dest_ref, meta_ref, start_ref, win_ref,
                 *, n_exp, block_rows):
    def body(e, off):
        start_ref[e] = off
        return off + (cnt_ref[e] + block_rows - 1) // block_rows * block_rows

    total = lax.fori_loop(0, n_exp, body, jnp.int32(0))
    idx = idx_ref[...]
    dest = rank_ref[...]
    blk_start = lax.broadcasted_iota(I32, meta_ref.shape, 1) * block_rows
    blk_e = jnp.zeros(meta_ref.shape, I32)
    lane = _lane_iota((SUBLANES, LANES))
    start_l = jnp.zeros((SUBLANES, LANES), I32)
    cnt_l = jnp.zeros((SUBLANES, LANES), I32)
    for e in range(n_exp):
        dest = dest + jnp.where(idx == e, start_ref[e], 0)
        start_l = jnp.where(lane == e, start_ref[e], start_l)
        cnt_l = jnp.where(lane == e, cnt_ref[e], cnt_l)
        if e > 0:
            blk_e = blk_e + (blk_start >= start_ref[e]).astype(I32)
    dest_ref[...] = dest
    sub = lax.broadcasted_iota(I32, meta_ref.shape, 0)
    meta_ref[...] = jnp.where(sub == 0, blk_e, total // block_rows)

    toff = toff_ref[...].astype(I32)
    n_rows = toff.shape[0]
    nxt = jnp.concatenate([toff[SUBLANES:], cnt_l], axis=0) if n_rows > SUBLANES else cnt_l
    first = jnp.concatenate([start_l] * (n_rows // SUBLANES), axis=0) + toff
    aligned = jnp.bitwise_and(first, -BF16_ROWS)
    lanes = _lane_iota(toff.shape)
    span = jnp.where(lanes < n_exp, first - aligned + (nxt - toff), 0)
    n_win = jnp.right_shift(span + (WINDOW_ROWS - 1), WINDOW_SHIFT)
    win_ref[...] = jnp.where(lanes < n_exp, aligned, pltpu.roll(n_win, n_exp, axis=1))


def _plan(cnt, idx, rank, toff, n_blocks):
    n_exp = cnt.shape[0]
    nb_pad = (n_blocks + LANES - 1) // LANES * LANES
    return pl.pallas_call(
        functools.partial(_plan_kernel, n_exp=n_exp, block_rows=TM_EXPERT),
        in_specs=[pl.BlockSpec(memory_space=pltpu.SMEM),
                  pl.BlockSpec(memory_space=pltpu.VMEM),
                  pl.BlockSpec(memory_space=pltpu.VMEM),
                  pl.BlockSpec(memory_space=pltpu.VMEM)],
        out_specs=(pl.BlockSpec(memory_space=pltpu.VMEM),
                   pl.BlockSpec(memory_space=pltpu.VMEM),
                   pl.BlockSpec(memory_space=pltpu.SMEM),
                   pl.BlockSpec(memory_space=pltpu.VMEM)),
        out_shape=(jax.ShapeDtypeStruct(idx.shape, I32),
                   jax.ShapeDtypeStruct((SUBLANES, nb_pad), I32),
                   jax.ShapeDtypeStruct((n_exp,), I32),
                   jax.ShapeDtypeStruct(toff.shape, I32)),
        compiler_params=pltpu.CompilerParams(vmem_limit_bytes=VMEM_LIMIT),
        name="plan",
    )(cnt, idx, rank, toff)


def _tile_rows(row):
    return pl.ds(pl.multiple_of(row * ROW_CHUNKS, ROW_CHUNKS), ROW_CHUNKS)


def _dispatch_kernel(dest_ref, cnt_ref, start_ref, hn_ref, xs_ref, zero_ref, sem, zsem,
                     *, n_exp, block_rows):
    tm = dest_ref.shape[1]

    def pad_copy(e, r):
        return pltpu.make_async_copy(zero_ref.at[pl.ds(0, ROW_CHUNKS)],
                                     xs_ref.at[_tile_rows(start_ref[e] + r)], zsem)

    def pad_chunk_copy(e, c):
        rows = pl.ds(pl.multiple_of((start_ref[e] + c * PAD_CHUNK) * ROW_CHUNKS, PAD_CHUNK * ROW_CHUNKS),
                     PAD_CHUNK * ROW_CHUNKS)
        return pltpu.make_async_copy(zero_ref.at[pl.ds(0, PAD_CHUNK * ROW_CHUNKS)], xs_ref.at[rows],
                                     zsem)

    def pad_bounds(e):
        cnt = cnt_ref[e]
        hi = (cnt + block_rows - 1) // block_rows * block_rows
        mid = jnp.minimum((cnt + PAD_CHUNK - 1) // PAD_CHUNK * PAD_CHUNK, hi)
        return cnt, mid, hi

    @pl.when(pl.program_id(0) == 0)
    def _():
        zero_ref[...] = jnp.zeros_like(zero_ref)

        def pad_start(e, c):
            lo, mid, hi = pad_bounds(e)
            lax.fori_loop(lo, mid, lambda r, c: (pad_copy(e, r).start(), c)[1], c)
            return lax.fori_loop(mid // PAD_CHUNK, hi // PAD_CHUNK,
                                 lambda j, c: (pad_chunk_copy(e, j).start(), c)[1], c)

        def pad_wait(e, c):
            lo, mid, hi = pad_bounds(e)
            lax.fori_loop(lo, mid, lambda r, c: (pad_copy(e, r).wait(), c)[1], c)
            return lax.fori_loop(mid // PAD_CHUNK, hi // PAD_CHUNK,
                                 lambda j, c: (pad_chunk_copy(e, j).wait(), c)[1], c)

        lax.fori_loop(0, n_exp, pad_start, 0)
        lax.fori_loop(0, n_exp, pad_wait, 0)

        last = n_exp - 1
        used = start_ref[last] + pad_bounds(last)[2]
        n_blocks = xs_ref.shape[0] // (ROW_CHUNKS * block_rows)

        def tail_copy(b):
            rows = pl.ds(pl.multiple_of(b * (block_rows * ROW_CHUNKS), block_rows * ROW_CHUNKS),
                         block_rows * ROW_CHUNKS)
            return pltpu.make_async_copy(zero_ref, xs_ref.at[rows], zsem)

        lax.fori_loop(used // block_rows, n_blocks, lambda b, c: (tail_copy(b).start(), c)[1], 0)
        lax.fori_loop(used // block_rows, n_blocks, lambda b, c: (tail_copy(b).wait(), c)[1], 0)

    def copy(k, t):
        return pltpu.make_async_copy(hn_ref.at[_tile_rows(t)],
                                     xs_ref.at[_tile_rows(dest_ref[k, t])], sem)

    def start(t, c):
        for k in range(TOP_K):
            copy(k, t).start(priority=k % 2)
        return c

    def wait(t, c):
        for k in range(TOP_K):
            copy(k, t).wait()
        return c

    lax.fori_loop(0, tm, start, 0)
    lax.fori_loop(0, tm, wait, 0)


def _dispatch(dest, cnt, starts, hn, cap):
    n = hn.shape[0] // ROW_CHUNKS
    tm = min(T_MOVE, n)
    return pl.pallas_call(
        functools.partial(_dispatch_kernel, n_exp=cnt.shape[0], block_rows=TM_EXPERT),
        grid=(n // tm,),
        in_specs=[pl.BlockSpec((TOP_K, tm), lambda i: (0, i), memory_space=pltpu.SMEM),
                  pl.BlockSpec(memory_space=pltpu.SMEM),
                  pl.BlockSpec(memory_space=pltpu.SMEM),
                  pl.BlockSpec((tm * ROW_CHUNKS, LANES), lambda i: (i, 0))],
        out_specs=pl.BlockSpec(memory_space=pl.ANY),
        out_shape=jax.ShapeDtypeStruct((cap * ROW_CHUNKS, LANES), hn.dtype),
        scratch_shapes=[pltpu.VMEM((TM_EXPERT * ROW_CHUNKS, LANES), hn.dtype),
                        pltpu.SemaphoreType.DMA, pltpu.SemaphoreType.DMA],
        compiler_params=_cparams(("arbitrary",)),
        name="dispatch",
    )(dest, cnt, starts, hn)


def _experts_kernel(meta_ref, xs_ref, wg_ref, bg_ref, wu_ref, bu_ref, wd_ref, bd_ref, ys_ref,
                    wg_s, wu_s, wd_s):
    i = pl.program_id(0)
    tm = xs_ref.shape[0] // ROW_CHUNKS
    in_use = i < meta_ref[1, 0]
    new_expert = jnp.logical_or(i == 0, meta_ref[0, i] != meta_ref[0, jnp.maximum(i - 1, 0)])

    @pl.when(jnp.logical_not(in_use))
    def _():
        ys_ref[...] = jnp.zeros_like(ys_ref)

    @pl.when(jnp.logical_and(in_use, new_expert))
    def _():
        wg_s[...] = wg_ref[0, 0].astype(BF16)
        wu_s[...] = wu_ref[0, 0].astype(BF16)
        wd_s[...] = wd_ref[0, 0].astype(BF16)

    @pl.when(in_use)
    def _():
        x = _load_token_tiles(xs_ref, tm).astype(BF16)
        gl = jnp.dot(x, wg_s[...], preferred_element_type=F32) + bg_ref[0, 0]
        up = jnp.dot(x, wu_s[...], preferred_element_type=F32) + bu_ref[0, 0]
        gl = jnp.minimum(gl, SWIGLU_LIMIT)
        up = jnp.clip(up, -SWIGLU_LIMIT, SWIGLU_LIMIT)
        act = gl * (1.0 / (1.0 + jnp.exp(-SWIGLU_ALPHA * gl)))
        hmid = ((up + 1.0) * act).astype(BF16)
        y = jnp.dot(hmid, wd_s[...], preferred_element_type=F32) + bd_ref[0, 0]
        ys_ref[...] = y.astype(ys_ref.dtype)


def _experts(meta, xs, layer, wg, bg, wu, bu, wd, bd):
    cap = xs.shape[0] // ROW_CHUNKS
    _, _, d, de = wg.shape
    tm = TM_EXPERT
    n_blocks = cap // tm

    def blk(i, meta):
        return jnp.minimum(i, meta[1, 0] - 1)

    rows = lambda i, meta: (blk(i, meta), 0)
    wmap = lambda i, meta: (layer, meta[0, blk(i, meta)], 0, 0)
    grid_spec = pltpu.PrefetchScalarGridSpec(
        num_scalar_prefetch=1,
        grid=(n_blocks,),
        in_specs=[pl.BlockSpec((tm * ROW_CHUNKS, LANES), rows),
                  pl.BlockSpec((1, 1, d, de), wmap),
                  pl.BlockSpec((1, 1, 1, de), wmap),
                  pl.BlockSpec((1, 1, d, de), wmap),
                  pl.BlockSpec((1, 1, 1, de), wmap),
                  pl.BlockSpec((1, 1, de, d), wmap),
                  pl.BlockSpec((1, 1, 1, d), wmap)],
        out_specs=pl.BlockSpec((tm, d), lambda i, meta: (i, 0)),
        scratch_shapes=[pltpu.VMEM((d, de), BF16), pltpu.VMEM((d, de), BF16),
                        pltpu.VMEM((de, d), BF16)],
    )
    return pl.pallas_call(
        _experts_kernel,
        grid_spec=grid_spec,
        out_shape=jax.ShapeDtypeStruct((cap, d), BF16),
        compiler_params=_cparams(("arbitrary",)),
        name="experts",
    )(meta, xs, wg, bg, wu, bu, wd, bd)


def _combine_kernel(win_ref, dest_ref, idx_ref, gate_ref, x1_ref, ys_ref, o_ref, buf_ref, one_ref,
                    sem, one_sem, *, n_exp):
    i = pl.program_id(0)
    n_tiles = pl.num_programs(0)
    tm = x1_ref.shape[0]
    last_window = ys_ref.shape[0] - WINDOW_ROWS

    def window_start(tile, e, rnd):
        return jnp.minimum(win_ref[tile, e] + rnd * WINDOW_ROWS, last_window)

    def ys_window(tile, e, rnd):
        return ys_ref.at[pl.ds(pl.multiple_of(window_start(tile, e, rnd), BF16_ROWS), WINDOW_ROWS)]

    def buf_window(slot, w):
        return buf_ref.at[slot, pl.ds(pl.multiple_of(w * WINDOW_ROWS, WINDOW_ROWS), WINDOW_ROWS)]

    def for_spills(tile, fn):
        def body(e, j):
            hit = jnp.logical_and(win_ref[tile, n_exp + e] > 1, j < SPILL_WINDOWS)

            @pl.when(hit)
            def _():
                fn(e, j)

            return j + hit.astype(I32)

        lax.fori_loop(0, n_exp, body, jnp.int32(0))

    def fetch(tile, slot):
        for e in range(n_exp):
            pltpu.make_async_copy(ys_window(tile, e, 0), buf_window(slot, e),
                                  sem.at[slot]).start(priority=e % 2)
        for_spills(tile, lambda e, j: pltpu.make_async_copy(
            ys_window(tile, e, 1), buf_window(slot, n_exp + j), sem.at[slot]).start())

    def drain(tile, slot):
        for e in range(n_exp):
            pltpu.make_async_copy(ys_window(tile, e, 0), buf_window(slot, e), sem.at[slot]).wait()
        for_spills(tile, lambda e, j: pltpu.make_async_copy(
            ys_window(tile, e, 1), buf_window(slot, n_exp + j), sem.at[slot]).wait())

    @pl.when(i == 0)
    def _():
        fetch(0, 0)

    @pl.when(i + 1 < n_tiles)
    def _():
        fetch(i + 1, (i + 1) % 2)

    idx = idx_ref[...]
    dest = dest_ref[...]
    gates = gate_ref[...]
    chunk = SEL_CHUNK_WINDOWS * WINDOW_ROWS
    lane = _lane_iota((tm, chunk)).astype(F32)

    first_tok = jnp.zeros(idx.shape, I32)
    clamped_tok = jnp.zeros(idx.shape, I32)
    for e in range(n_exp):
        first_tok = jnp.where(idx == e, win_ref[i, e], first_tok)
        clamped_tok = jnp.where(idx == e, window_start(i, e, 0), clamped_tok)
    local = dest - clamped_tok
    col = jnp.where(dest - first_tok < WINDOW_ROWS, idx * WINDOW_ROWS + local, -1)
    key = idx * KEY_STRIDE + (dest - first_tok)
    packed = jnp.concatenate([col.astype(F32), gates, key.astype(F32),
                              jnp.zeros((LANES - 3 * TOP_K, tm), F32)], axis=0)
    by_tok = jnp.transpose(packed)

    def spread(j, width):
        one = jnp.broadcast_to(by_tok[:, j:j + 1], (tm, LANES))
        return jnp.concatenate([one] * (width // LANES), axis=1) if width > LANES else one

    slot = i % 2
    drain(i, slot)
    cols = [spread(k, chunk) for k in range(TOP_K)]
    gts = [spread(TOP_K + k, chunk) for k in range(TOP_K)]
    total = x1_ref[...]
    for c in range(n_exp // SEL_CHUNK_WINDOWS):
        sel = jnp.zeros((tm, chunk), F32)
        for k in range(TOP_K):
            sel = jnp.where(lane == cols[k] - float(c * chunk), gts[k], sel)
        total = total + jnp.dot(sel.astype(BF16), buf_ref[slot, c * chunk:(c + 1) * chunk, :],
                                preferred_element_type=F32)
    o_ref[...] = total

    lane1 = _lane_iota((tm, LANES))
    keys1 = [spread(2 * TOP_K + k, LANES) for k in range(TOP_K)]
    gts1 = [spread(TOP_K + k, LANES) for k in range(TOP_K)]

    def add_window(e, rnd, rows):
        shift = win_ref[i, e] + rnd * WINDOW_ROWS - window_start(i, e, rnd)
        want = jnp.where(lane1 >= shift, lane1 + (e * KEY_STRIDE + rnd * WINDOW_ROWS - shift),
                         -1).astype(F32)
        sel = jnp.zeros((tm, LANES), F32)
        for k in range(TOP_K):
            sel = jnp.where(want == keys1[k], gts1[k], sel)
        o_ref[...] += jnp.dot(sel.astype(BF16), rows, preferred_element_type=F32)

    for_spills(i, lambda e, j: add_window(e, 1, buf_window(slot, n_exp + j)[...]))

    def on_demand(e, j):
        n_win = win_ref[i, n_exp + e]
        hit = n_win > 1
        ahead = jnp.logical_and(hit, j < SPILL_WINDOWS)

        def one(rnd, c):
            cp = pltpu.make_async_copy(ys_window(i, e, rnd), one_ref, one_sem)
            cp.start()
            cp.wait()
            add_window(e, rnd, one_ref[...])
            return c

        lax.fori_loop(jnp.where(ahead, 2, 1), n_win, one, 0)
        return j + hit.astype(I32)

    lax.fori_loop(0, n_exp, on_demand, jnp.int32(0))


def _combine(win, dest, idx, gates, x1, ys, n_exp):
    n, d = x1.shape
    tm = min(TM_OUT, n)
    vec = lambda i, win: (0, i)
    grid_spec = pltpu.PrefetchScalarGridSpec(
        num_scalar_prefetch=1,
        grid=(n // tm,),
        in_specs=[pl.BlockSpec((TOP_K, tm), vec),
                  pl.BlockSpec((TOP_K, tm), vec),
                  pl.BlockSpec((TOP_K, tm), vec),
                  pl.BlockSpec((tm, d), lambda i, win: (i, 0)),
                  pl.BlockSpec(memory_space=pl.ANY)],
        out_specs=pl.BlockSpec((tm, d), lambda i, win: (i, 0)),
        scratch_shapes=[pltpu.VMEM((2, (n_exp + SPILL_WINDOWS) * WINDOW_ROWS, d), ys.dtype),
                        pltpu.VMEM((WINDOW_ROWS, d), ys.dtype),
                        pltpu.SemaphoreType.DMA((2,)),
                        pltpu.SemaphoreType.DMA],
    )
    return pl.pallas_call(
        functools.partial(_combine_kernel, n_exp=n_exp),
        grid_spec=grid_spec,
        out_shape=jax.ShapeDtypeStruct((n, d), F32),
        compiler_params=_cparams(("arbitrary",)),
        name="combine",
    )(win, dest, idx, gates, x1, ys)


def _pad_lanes(v, width=LANES):
    v = v.astype(F32).reshape(1, -1)
    return jnp.pad(v, ((0, 0), (0, width - v.shape[1])))


def _layer(x2, mem, bsz, seq, p, layer, experts):
    n, d = x2.shape
    f32 = F32
    w_in = p['w_in']
    off_q = 2 * GROUP_WIDTH
    off_k, off_v, off_f = off_q + GROUP_WIDTH, off_q + 2 * GROUP_WIDTH, off_q + 3 * GROUP_WIDTH
    off_qm = off_f + N_HEADS
    w_f = w_in[:, off_f:off_qm]
    w_f_pad = jnp.pad(jnp.concatenate([w_f, w_f, w_f], axis=1), ((0, 0), (0, LANES - 3 * N_HEADS)))
    w_all = jnp.concatenate([w_in[:, :off_f], w_in[:, off_qm:], w_f_pad], axis=1).astype(BF16)
    fb = p['fox_forget_b'].astype(f32)
    fb_pad = _pad_lanes(jnp.concatenate([fb, fb, fb]))
    tm_in = min(TM_INPROJ, seq)
    tri = jnp.tril(jnp.ones((tm_in, tm_in), f32)).astype(BF16)

    sel = jnp.zeros((3, LANES, N_HEADS, LANES), f32)
    for part in range(3):
        for hd in range(N_HEADS):
            sel = sel.at[part, hd, hd, HEAD_DIM + part].set(1.0)
            sel = sel.at[part, hd, hd, HEAD_DIM + 3 + part].set(-1.0)
    sel = sel.reshape(3 * LANES, N_HEADS * LANES).astype(BF16)
    head_of = jnp.arange(GROUP_WIDTH) // HEAD_DIM
    ones_blk = (head_of[:, None] == head_of[None, :]).astype(BF16)
    ones2 = jnp.concatenate([ones_blk, ones_blk], axis=0)

    def head_gain(g):
        return jnp.tile(g.astype(f32).reshape(1, HEAD_DIM), (1, N_HEADS))

    ua, ub, qa, ka, va, qm = _inproj(
        x2, seq, p['mix_norm_g'].reshape(1, d).astype(f32), w_all,
        head_gain(p['fox_q_g']), head_gain(p['fox_k_g']), head_gain(p['mem_q_g']), fb_pad, tri,
        sel, ones2)

    bmat, coef, ssm_pw, cmat = _s5_constants(
        p['ssm_lambda_re'], p['ssm_lambda_im'], p['ssm_log_dt'], p['ssm_b_re'], p['ssm_b_im'],
        p['ssm_c_re'], p['ssm_c_im'], min(T_S5, seq))
    ya = _s5(ua, bsz, seq, bmat, coef, ssm_pw, cmat,
             p['ssm_d'].reshape(1, GROUP_WIDTH).astype(f32), p['ssm_glu_w'].astype(BF16),
             p['ssm_glu_b'].reshape(1, GROUP_WIDTH).astype(f32),
             p['ssm_out_g'].reshape(1, GROUP_WIDTH).astype(f32))

    pw = p['pool_w'].astype(f32)
    w_blk = jnp.zeros((GROUP_WIDTH, GROUP_WIDTH), f32)
    for gi in range(len(POOL_WINDOWS)):
        w_blk = w_blk.at[gi * HEAD_DIM:(gi + 1) * HEAD_DIM, gi * HEAD_DIM:(gi + 1) * HEAD_DIM].set(pw[gi])
    yb = _pool(ub, bsz, seq, w_blk.astype(BF16), p['pool_scale'].reshape(1, GROUP_WIDTH).astype(f32))

    yc = _fox(qa, ka, va, bsz, seq, p['fox_out_g'].reshape(1, GROUP_WIDTH).astype(f32))

    mk, mv = _memkv(mem, p['mem_norm_g'].reshape(1, d).astype(f32), p['mem_w_kv'].astype(BF16),
                    _pad_lanes(p['mem_k_g']))
    ym = _memattn(qm, mk, mv, bsz, seq, p['mem_out_g'].reshape(1, GROUP_WIDTH).astype(f32))

    n_exp = p['router_w'].shape[1]
    tm_out = min(TM_OUT, n)
    ustrict = jnp.triu(jnp.ones((tm_out, tm_out), f32), k=1).astype(BF16)
    rw = p['router_w'].astype(f32)
    rw_hi = rw.astype(BF16)
    rw_lo = (rw - rw_hi.astype(f32)).astype(BF16)
    rw_parts = jnp.pad(jnp.concatenate([rw_hi, rw_lo], axis=1), ((0, 0), (0, LANES - 2 * n_exp)))
    x1, hn, idx, gates, rank, cnt, toff = _outproj(
        x2, ya, yb, yc, ym, p['w_out'].astype(BF16), p['ffn_norm_g'].reshape(1, d).astype(f32),
        rw_parts, p['router_b'].reshape(n_exp, 1).astype(f32), ustrict)

    cap = n * TOP_K + n_exp * TM_EXPERT
    cnt_i = cnt[:, 0].astype(I32)
    dest, meta, starts, win = _plan(cnt_i, idx, rank, toff, cap // TM_EXPERT)
    xs = _dispatch(dest, cnt_i, starts, hn, cap)
    ys = _experts(meta, xs, layer, *experts)
    win_tab = win[::SUBLANES, :2 * n_exp]
    return _combine(win_tab, dest, idx, gates, x1, ys, n_exp)


_PARAM_NAMES = ('mix_norm_g', 'w_in', 'ssm_lambda_re', 'ssm_lambda_im', 'ssm_log_dt',
                'ssm_b_re', 'ssm_b_im', 'ssm_c_re', 'ssm_c_im', 'ssm_d', 'ssm_glu_w', 'ssm_glu_b',
                'ssm_out_g', 'pool_w', 'pool_scale', 'fox_forget_b', 'fox_q_g', 'fox_k_g',
                'fox_out_g', 'mem_norm_g', 'mem_w_kv', 'mem_q_g', 'mem_k_g', 'mem_out_g', 'w_out',
                'ffn_norm_g', 'router_w', 'router_b', 'exp_w_gate', 'exp_b_gate', 'exp_w_up',
                'exp_b_up', 'exp_w_down', 'exp_b_down')


def kernel(x, mem, mix_norm_g, w_in, ssm_lambda_re, ssm_lambda_im, ssm_log_dt, ssm_b_re, ssm_b_im,
           ssm_c_re, ssm_c_im, ssm_d, ssm_glu_w, ssm_glu_b, ssm_out_g, pool_w, pool_scale,
           fox_forget_b, fox_q_g, fox_k_g, fox_out_g, mem_norm_g, mem_w_kv, mem_q_g, mem_k_g,
           mem_out_g, w_out, ffn_norm_g, router_w, router_b, exp_w_gate, exp_b_gate, exp_w_up,
           exp_b_up, exp_w_down, exp_b_down):
    stacked = dict(zip(_PARAM_NAMES, (
        mix_norm_g, w_in, ssm_lambda_re, ssm_lambda_im, ssm_log_dt, ssm_b_re, ssm_b_im, ssm_c_re,
        ssm_c_im, ssm_d, ssm_glu_w, ssm_glu_b, ssm_out_g, pool_w, pool_scale, fox_forget_b,
        fox_q_g, fox_k_g, fox_out_g, mem_norm_g, mem_w_kv, mem_q_g, mem_k_g, mem_out_g, w_out,
        ffn_norm_g, router_w, router_b, exp_w_gate, exp_b_gate, exp_w_up, exp_b_up, exp_w_down,
        exp_b_down)))
    bsz, seq, d = x.shape
    depth = w_in.shape[0]
    x2 = x.reshape(bsz * seq, d).astype(F32)
    mem = mem.astype(F32)
    expert_names = ('exp_w_gate', 'exp_b_gate', 'exp_w_up', 'exp_b_up', 'exp_w_down', 'exp_b_down')
    experts = tuple(stacked[k].astype(F32) if stacked[k].ndim == 4
                    else stacked[k].astype(F32)[:, :, None, :] for k in expert_names)
    for layer in range(depth):
        x2 = _layer(x2, mem, bsz, seq,
                    {k: v[layer] for k, v in stacked.items() if k not in expert_names},
                    layer, experts)
    return x2.reshape(bsz, seq, d).astype(x.dtype)
```

```python
import functools
import math

import jax
import jax.numpy as jnp
from jax import lax
from jax.experimental import pallas as pl
from jax.experimental.pallas import tpu as pltpu

F32 = jnp.float32
BF16 = jnp.bfloat16
I32 = jnp.int32

EPS = 1e-6
HEAD_DIM = 64
N_HEADS = 4
GROUP_WIDTH = 256
LANES = 128
SUBLANES = 8
ROW_CHUNKS = 8
SSM_GROUPS = 16
SSM_CH = 16
SSM_STATE = 64
SSM_LANES = SSM_GROUPS * SSM_STATE
POOL_WINDOWS = (2, 4, 8, 16)
POOL_HALO = 16
TOP_K = 4
SWIGLU_LIMIT = 7.0
SWIGLU_ALPHA = 1.702
VMEM_LIMIT = 56 * 1024 * 1024

TM_INPROJ = 512
T_S5 = 512
T_POOL = 512
T_ATT = 512
T_ATT_Q = 1024
TM_OUT = 512
TM_EXPERT = 512
T_MOVE = 256
PAD_CHUNK = 64
BF16_ROWS = 16
WINDOW_SHIFT = 7
WINDOW_ROWS = 1 << WINDOW_SHIFT
SEL_CHUNK_WINDOWS = 8
SPILL_WINDOWS = 8
KEY_STRIDE = 1 << 16

NEG_INF = float("-inf")
LOG2E = 1.4426950408889634


def _cparams(sem):
    return pltpu.CompilerParams(dimension_semantics=sem, vmem_limit_bytes=VMEM_LIMIT)


def _lane_iota(shape):
    return lax.broadcasted_iota(I32, shape, len(shape) - 1)


def _split3(x):
    hi = x.astype(BF16).astype(F32)
    r = x - hi
    mid = r.astype(BF16).astype(F32)
    lo = r - mid
    return hi, mid, lo


def _head_slab(slab, odd):
    lane = _lane_iota(slab.shape)
    if odd:
        slab = pltpu.roll(slab, HEAD_DIM, axis=1)
    return jnp.where(lane < HEAD_DIM, slab, 0.0)


def _head_norm(xh, gain):
    ss = jnp.sum(xh * xh, axis=-1, keepdims=True)
    return xh * jnp.broadcast_to(lax.rsqrt(ss * (1.0 / HEAD_DIM) + EPS), xh.shape) * gain


def _divide_by_denominator(acc):
    inv = 1.0 / acc[:, HEAD_DIM:HEAD_DIM + 1]
    return acc * jnp.broadcast_to(inv, acc.shape)


def _join_heads(o_even, o_odd):
    lane = _lane_iota(o_even.shape)
    return jnp.where(lane < HEAD_DIM, o_even, pltpu.roll(o_odd, HEAD_DIM, axis=1))


def _store_token_tiles(ref, val):
    t = val.shape[0]
    for s in range(ROW_CHUNKS):
        ref[pl.ds(s, t, stride=ROW_CHUNKS), :] = val[:, s * LANES:(s + 1) * LANES]


def _load_token_tiles(ref, t):
    return jnp.concatenate([ref[pl.ds(s, t, stride=ROW_CHUNKS), :] for s in range(ROW_CHUNKS)],
                           axis=1)


def _group_norm_pairs(pairs, gain_ref, out_ref):
    ss = jnp.sum(sum(p * p for p in pairs), axis=-1, keepdims=True)
    scale = jnp.broadcast_to(lax.rsqrt(ss * (1.0 / GROUP_WIDTH) + EPS), pairs[0].shape)
    for i, p in enumerate(pairs):
        sl = slice(i * LANES, (i + 1) * LANES)
        out_ref[:, sl] = (p * scale * gain_ref[:, sl]).astype(out_ref.dtype)


COL_A, COL_B, COL_Q, COL_K, COL_V, COL_QM, COL_F = 0, 256, 512, 768, 1024, 1280, 1536
IN_COLS_PADDED = COL_F + LANES


def _inproj_kernel(x_ref, g_ref, w_ref, qg_ref, kg_ref, mqg_ref, fb_ref, tri_ref, sel_ref, ones_ref,
                   ua_ref, ub_ref, qa_ref, ka_ref, vat_ref, qm_ref, carry_ref, *, tiles_per_seq):
    i = pl.program_id(0)

    @pl.when(i % tiles_per_seq == 0)
    def _():
        carry_ref[...] = jnp.zeros_like(carry_ref)

    x = x_ref[...]
    h = x * lax.rsqrt(jnp.mean(x * x, axis=-1, keepdims=True) + EPS) * g_ref[...]
    proj = jnp.dot(h.astype(BF16), w_ref[...], preferred_element_type=F32)
    ua_ref[0] = proj[:, COL_A:COL_A + LANES]
    ua_ref[1] = proj[:, COL_A + LANES:COL_A + GROUP_WIDTH]
    ub_ref[...] = proj[:, COL_B:COL_B + GROUP_WIDTH]

    z = proj[:, COL_F:COL_F + LANES] + fb_ref[...]
    lane = _lane_iota(z.shape)
    logf = jnp.minimum(z, 0.0) - jnp.log(1.0 + jnp.exp(-jnp.abs(z)))
    hi, mid, lo = _split3(logf)
    packed = jnp.where(lane < 4, hi, jnp.where(lane < 8, mid, jnp.where(lane < 12, lo, 0.0)))
    cs = jnp.dot(tri_ref[...], packed.astype(BF16), preferred_element_type=F32)
    cum = cs + pltpu.roll(cs, LANES - 4, axis=1) + pltpu.roll(cs, LANES - 8, axis=1)
    cum = cum + carry_ref[...]
    carry_ref[...] = cum[cum.shape[0] - 1:, :]

    scale = HEAD_DIM ** -0.5 * LOG2E
    c_parts = jnp.concatenate(_split3(cum * LOG2E), axis=1).astype(BF16)
    bias = jnp.dot(c_parts, sel_ref[...], preferred_element_type=F32)

    def heads_normed(col, gain_ref):
        t = proj[:, col:col + GROUP_WIDTH]
        sq = t * t
        sq_hi = sq.astype(BF16)
        sq_lo = (sq - sq_hi.astype(F32)).astype(BF16)
        ss = jnp.dot(jnp.concatenate([sq_hi, sq_lo], axis=1), ones_ref[...],
                     preferred_element_type=F32)
        return t * lax.rsqrt(ss * (1.0 / HEAD_DIM) + EPS) * gain_ref[...]

    qn_all = heads_normed(COL_Q, qg_ref) * scale
    kn_all = heads_normed(COL_K, kg_ref)
    mn_all = heads_normed(COL_QM, mqg_ref) * scale
    q_bias = (lane >= 64) & (lane < 67)
    k_bias = (lane >= 67) & (lane < 70)
    for hd in range(N_HEADS):
        pair, odd = hd // 2, hd % 2
        pair_sl = slice(pair * LANES, (pair + 1) * LANES)
        b = bias[:, hd * LANES:(hd + 1) * LANES]
        qn = _head_slab(qn_all[:, pair_sl], odd)
        qa_ref[hd] = jnp.where(q_bias, b, jnp.where(k_bias, 1.0, qn)).astype(BF16)
        kn = _head_slab(kn_all[:, pair_sl], odd)
        ka_ref[hd] = jnp.where(k_bias, b, jnp.where(q_bias, 1.0, kn)).astype(BF16)
        qm_ref[hd] = _head_slab(mn_all[:, pair_sl], odd).astype(BF16)
    tm = proj.shape[0]
    v_t = jnp.transpose(proj[:, COL_V:COL_V + GROUP_WIDTH])
    tail = jnp.where(lax.broadcasted_iota(I32, (HEAD_DIM, tm), 0) == 0, 1.0, 0.0)
    for hd in range(N_HEADS):
        vat_ref[hd] = jnp.concatenate([v_t[hd * HEAD_DIM:(hd + 1) * HEAD_DIM], tail],
                                      axis=0).astype(BF16)


def _inproj(x2, seq, mix_g, w_all, qg, kg, mqg, fb, tri, sel, ones):
    n, d = x2.shape
    tm = min(TM_INPROJ, seq)
    grid = (n // tm,)
    const = lambda i: (0, 0)
    row = lambda i: (i, 0)
    hrow = lambda i: (0, i, 0)
    out_shape = (
        jax.ShapeDtypeStruct((2, n, LANES), F32),
        jax.ShapeDtypeStruct((n, GROUP_WIDTH), F32),
        jax.ShapeDtypeStruct((N_HEADS, n, LANES), BF16),
        jax.ShapeDtypeStruct((N_HEADS, n, LANES), BF16),
        jax.ShapeDtypeStruct((N_HEADS, LANES, n), BF16),
        jax.ShapeDtypeStruct((N_HEADS, n, LANES), BF16),
    )
    return pl.pallas_call(
        functools.partial(_inproj_kernel, tiles_per_seq=seq // tm),
        grid=grid,
        in_specs=[
            pl.BlockSpec((tm, d), row),
            pl.BlockSpec((1, d), const),
            pl.BlockSpec((d, IN_COLS_PADDED), const),
            pl.BlockSpec((1, GROUP_WIDTH), const),
            pl.BlockSpec((1, GROUP_WIDTH), const),
            pl.BlockSpec((1, GROUP_WIDTH), const),
            pl.BlockSpec((1, LANES), const),
            pl.BlockSpec((tm, tm), const),
            pl.BlockSpec((3 * LANES, N_HEADS * LANES), const),
            pl.BlockSpec((2 * GROUP_WIDTH, GROUP_WIDTH), const),
        ],
        out_specs=(
            pl.BlockSpec((2, tm, LANES), hrow),
            pl.BlockSpec((tm, GROUP_WIDTH), row),
            pl.BlockSpec((N_HEADS, tm, LANES), hrow),
            pl.BlockSpec((N_HEADS, tm, LANES), hrow),
            pl.BlockSpec((N_HEADS, LANES, tm), lambda i: (0, 0, i)),
            pl.BlockSpec((N_HEADS, tm, LANES), hrow),
        ),
        out_shape=out_shape,
        scratch_shapes=[pltpu.VMEM((1, LANES), F32)],
        compiler_params=_cparams(("arbitrary",)),
        name="inproj",
    )(x2, mix_g, w_all, qg, kg, mqg, fb, tri, sel, ones)


def _s5_kernel(u_ref, bmat_ref, coef_ref, pw_ref, cmat_ref, d_ref, gluw_ref, glub_ref, og_ref,
               o_ref, u_scr, x_scr, y_scr, carry_ref):
    @pl.when(pl.program_id(1) == 0)
    def _():
        carry_ref[...] = jnp.zeros_like(carry_ref)

    t = u_ref.shape[1]
    g = t // SUBLANES
    halves = [slice(h * LANES, (h + 1) * LANES) for h in range(GROUP_WIDTH // LANES)]
    for j in range(g):
        for h, sl in enumerate(halves):
            u_scr[j * SUBLANES:(j + 1) * SUBLANES, sl] = u_ref[h, pl.ds(j, SUBLANES, stride=g), :]
    u = u_scr[...]
    x_scr[...] = jnp.dot(u.astype(BF16), bmat_ref[...], preferred_element_type=F32)
    n_lane_blocks = SSM_LANES // LANES
    re_sl = [slice(lb * LANES, (lb + 1) * LANES) for lb in range(n_lane_blocks)]
    im_sl = [slice(SSM_LANES + lb * LANES, SSM_LANES + (lb + 1) * LANES)
             for lb in range(n_lane_blocks)]

    def group_rows(j):
        return pl.ds(pl.multiple_of(j * SUBLANES, SUBLANES), SUBLANES)

    def pass1(j, h):
        rows = group_rows(j)
        new = []
        for lb in range(n_lane_blocks):
            ar = coef_ref[0, :, re_sl[lb]]
            ai = coef_ref[1, :, re_sl[lb]]
            hr, hi = h[2 * lb], h[2 * lb + 1]
            nr = ar * hr - ai * hi + x_scr[rows, re_sl[lb]]
            ni = ar * hi + ai * hr + x_scr[rows, im_sl[lb]]
            x_scr[rows, re_sl[lb]] = nr
            x_scr[rows, im_sl[lb]] = ni
            new += [nr, ni]
        return tuple(new)

    zero = jnp.zeros((SUBLANES, LANES), F32)
    finals = lax.fori_loop(0, g, pass1, (zero,) * (2 * n_lane_blocks))

    row_id = lax.broadcasted_iota(I32, (SUBLANES, LANES), 0)
    starts = []
    for lb in range(n_lane_blocks):
        fr, fi = finals[2 * lb], finals[2 * lb + 1]
        for s, k in enumerate((1, 2, 4)):
            cr = coef_ref[2 + 2 * s, :, re_sl[lb]]
            ci = coef_ref[3 + 2 * s, :, re_sl[lb]]
            sr = pltpu.roll(fr, k, axis=0)
            si = pltpu.roll(fi, k, axis=0)
            fr, fi = fr + cr * sr - ci * si, fi + cr * si + ci * sr
        qr = coef_ref[8, :, re_sl[lb]]
        qi = coef_ref[9, :, re_sl[lb]]
        cbr = carry_ref[:, re_sl[lb]]
        cbi = carry_ref[:, im_sl[lb]]
        fr, fi = fr + qr * cbr - qi * cbi, fi + qr * cbi + qi * cbr
        starts.append(jnp.where(row_id == 0, cbr, pltpu.roll(fr, 1, axis=0)))
        starts.append(jnp.where(row_id == 0, cbi, pltpu.roll(fi, 1, axis=0)))
        carry_ref[:, re_sl[lb]] = jnp.broadcast_to(fr[SUBLANES - 1:, :], fr.shape)
        carry_ref[:, im_sl[lb]] = jnp.broadcast_to(fi[SUBLANES - 1:, :], fi.shape)

    def pass2(j, c):
        rows = group_rows(j)
        for lb in range(n_lane_blocks):
            pr = pw_ref[rows, re_sl[lb]]
            pi_ = pw_ref[rows, im_sl[lb]]
            sr, si = starts[2 * lb], starts[2 * lb + 1]
            x_scr[rows, re_sl[lb]] += pr * sr - pi_ * si
            x_scr[rows, im_sl[lb]] += pr * si + pi_ * sr
        return c

    lax.fori_loop(0, g, pass2, 0)

    y = jnp.dot(x_scr[...].astype(BF16), cmat_ref[...], preferred_element_type=F32) + d_ref[...] * u
    z = jax.nn.gelu(y, approximate=True)
    gate = jnp.dot(z.astype(BF16), gluw_ref[...], preferred_element_type=F32) + glub_ref[...]
    out = z * (1.0 / (1.0 + jnp.exp(-gate)))
    _group_norm_pairs([out[:, :LANES], out[:, LANES:]], og_ref, y_scr)
    for j in range(g):
        for h, sl in enumerate(halves):
            o_ref[h, pl.ds(j, SUBLANES, stride=g), :] = y_scr[j * SUBLANES:(j + 1) * SUBLANES, sl]


def _s5(ua, bsz, seq, bmat, coef, pw, cmat, dvec, gluw, glub, og):
    n = ua.shape[1]
    t = pw.shape[0]
    nt = seq // t
    row = lambda b, j: (0, b * nt + j, 0)
    c2 = lambda b, j: (0, 0)
    c3 = lambda b, j: (0, 0, 0)
    return pl.pallas_call(
        _s5_kernel,
        grid=(bsz, nt),
        in_specs=[
            pl.BlockSpec((2, t, LANES), row),
            pl.BlockSpec((GROUP_WIDTH, 2 * SSM_LANES), c2),
            pl.BlockSpec((10, SUBLANES, SSM_LANES), c3),
            pl.BlockSpec((t, 2 * SSM_LANES), c2),
            pl.BlockSpec((2 * SSM_LANES, GROUP_WIDTH), c2),
            pl.BlockSpec((1, GROUP_WIDTH), c2),
            pl.BlockSpec((GROUP_WIDTH, GROUP_WIDTH), c2),
            pl.BlockSpec((1, GROUP_WIDTH), c2),
            pl.BlockSpec((1, GROUP_WIDTH), c2),
        ],
        out_specs=pl.BlockSpec((2, t, LANES), row),
        out_shape=jax.ShapeDtypeStruct((2, n, LANES), F32),
        scratch_shapes=[pltpu.VMEM((t, GROUP_WIDTH), F32),
                        pltpu.VMEM((t, 2 * SSM_LANES), F32),
                        pltpu.VMEM((t, GROUP_WIDTH), F32),
                        pltpu.VMEM((SUBLANES, 2 * SSM_LANES), F32)],
        compiler_params=_cparams(("arbitrary", "arbitrary")),
        name="s5",
    )(ua, bmat, coef, pw, cmat, dvec, gluw, glub, og)


def _s5_constants(lam_re, lam_im, log_dt, b_re, b_im, c_re, c_im, tile):
    lr = lam_re.astype(F32)
    li = lam_im.astype(F32)
    dt = jnp.exp(log_dt.astype(F32))[:, None]
    mag = jnp.exp(lr * dt)
    a_re = mag * jnp.cos(li * dt)
    a_im = mag * jnp.sin(li * dt)
    den = lr * lr + li * li
    n_re = a_re - 1.0
    n_im = a_im
    k_re = (n_re * lr + n_im * li) / den
    k_im = (n_im * lr - n_re * li) / den
    br = b_re.astype(F32)
    bi = b_im.astype(F32)
    bb_re = k_re[..., None] * br - k_im[..., None] * bi
    bb_im = k_re[..., None] * bi + k_im[..., None] * br
    eye = jnp.eye(SSM_GROUPS, dtype=F32)
    bm_re = jnp.einsum('gph,gk->ghkp', bb_re, eye).reshape(GROUP_WIDTH, SSM_LANES)
    bm_im = jnp.einsum('gph,gk->ghkp', bb_im, eye).reshape(GROUP_WIDTH, SSM_LANES)
    bmat = jnp.concatenate([bm_re, bm_im], axis=1).astype(BF16)
    cm_re = jnp.einsum('ghp,gk->gpkh', c_re.astype(F32), eye).reshape(SSM_LANES, GROUP_WIDTH)
    cm_im = jnp.einsum('ghp,gk->gpkh', c_im.astype(F32), eye).reshape(SSM_LANES, GROUP_WIDTH)
    cmat = jnp.concatenate([cm_re, -cm_im], axis=0).astype(BF16)

    ar = a_re.reshape(1, SSM_LANES)
    ai = a_im.reshape(1, SSM_LANES)

    def cmul(x, y):
        return x[0] * y[0] - x[1] * y[1], x[0] * y[1] + x[1] * y[0]

    groups = tile // SUBLANES
    pows = [(ar, ai)]
    for _ in range(groups - 1):
        pows.append(cmul(pows[-1], (ar, ai)))
    pw = jnp.concatenate([jnp.repeat(jnp.concatenate([p[0] for p in pows], axis=0), SUBLANES, axis=0),
                          jnp.repeat(jnp.concatenate([p[1] for p in pows], axis=0), SUBLANES, axis=0)],
                         axis=1)
    chunk = pows[groups - 1]
    cpows = [chunk]
    for _ in range(SUBLANES - 1):
        cpows.append(cmul(cpows[-1], chunk))
    rows = jnp.arange(SUBLANES, dtype=I32)[:, None]
    ones = jnp.ones((SUBLANES, 1), F32)
    planes = [ones * ar, ones * ai]
    for k in (1, 2, 4):
        m = (rows >= k).astype(F32)
        planes += [m * cpows[k - 1][0], m * cpows[k - 1][1]]
    planes += [jnp.concatenate([p[0] for p in cpows], axis=0),
               jnp.concatenate([p[1] for p in cpows], axis=0)]
    coef = jnp.stack(planes, axis=0)
    return bmat, coef, pw, cmat


def _pool_kernel(v_ref, w_ref, g_ref, o_ref, ext_ref):
    j = pl.program_id(1)
    t = v_ref.shape[0]

    @pl.when(j == 0)
    def _():
        ext_ref[0:POOL_HALO, :] = jnp.zeros((POOL_HALO, GROUP_WIDTH), F32)

    v = v_ref[...]
    ext_ref[POOL_HALO:POOL_HALO + t, :] = v
    cur = ext_ref[...]
    width = 1
    wins = {}
    while width < POOL_WINDOWS[-1]:
        cur = cur + pltpu.roll(cur, width, axis=0)
        width *= 2
        wins[width] = cur[POOL_HALO:, :]
    lane = _lane_iota(v.shape)
    pos = (j * t + lax.broadcasted_iota(I32, v.shape, 0) + 1).astype(F32)
    mean = None
    for gi, w in enumerate(POOL_WINDOWS):
        m = wins[w] / jnp.minimum(pos, float(w))
        mean = m if mean is None else jnp.where(lane >= gi * HEAD_DIM, m, mean)
    mixed = jnp.dot((mean - v).astype(BF16), w_ref[...], preferred_element_type=F32)
    _group_norm_pairs([mixed[:, :LANES], mixed[:, LANES:]], g_ref, o_ref)
    ext_ref[0:POOL_HALO, :] = v[t - POOL_HALO:, :]


def _pool(ub, bsz, seq, w_blk, g):
    n = ub.shape[0]
    t = min(T_POOL, seq)
    nt = seq // t
    row = lambda b, j: (b * nt + j, 0)
    c2 = lambda b, j: (0, 0)
    return pl.pallas_call(
        _pool_kernel,
        grid=(bsz, nt),
        in_specs=[pl.BlockSpec((t, GROUP_WIDTH), row),
                  pl.BlockSpec((GROUP_WIDTH, GROUP_WIDTH), c2),
                  pl.BlockSpec((1, GROUP_WIDTH), c2)],
        out_specs=pl.BlockSpec((t, GROUP_WIDTH), row),
        out_shape=jax.ShapeDtypeStruct((n, GROUP_WIDTH), BF16),
        scratch_shapes=[pltpu.VMEM((t + POOL_HALO, GROUP_WIDTH), F32)],
        compiler_params=_cparams(("arbitrary", "arbitrary")),
        name="pool",
    )(ub, w_blk, g)


def _fox_kernel(qt_ref, kt_ref, qa_ref, ka_ref, vat_ref, g_ref, o_ref, m_ref, acc_ref):
    p_id = pl.program_id(1)
    qi = qt_ref[p_id]
    ki = kt_ref[p_id]
    tq = qa_ref.shape[1]
    tk = ka_ref.shape[1]

    @pl.when(ki == 0)
    def _():
        m_ref[...] = jnp.full_like(m_ref, NEG_INF)
        acc_ref[...] = jnp.zeros_like(acc_ref)

    def step(on_diagonal):
        if on_diagonal:
            causal = (lax.broadcasted_iota(I32, (tk, tq), 0) + (ki * tk - qi * tq)
                      <= lax.broadcasted_iota(I32, (tk, tq), 1))

        def scores(hd):
            return lax.dot_general(ka_ref[hd], qa_ref[hd], (((1,), (1,)), ((), ())),
                                   preferred_element_type=F32)

        s_next = scores(0)
        for hd in range(N_HEADS):
            s_t = s_next
            if hd + 1 < N_HEADS:
                s_next = scores(hd + 1)
            if on_diagonal:
                s_t = jnp.where(causal, s_t, NEG_INF)
            m_prev = m_ref[hd]
            m_new = jnp.maximum(m_prev, jnp.max(s_t, axis=0, keepdims=True))
            alpha = jnp.exp2(m_prev - m_new)
            p_t = jnp.exp2(s_t - m_new)
            acc_ref[hd] = alpha * acc_ref[hd] + jnp.dot(vat_ref[hd], p_t.astype(BF16),
                                                        preferred_element_type=F32)
            m_ref[hd] = m_new

    ratio = tq // tk
    first_masked = qi * ratio

    @pl.when(ki < first_masked)
    def _():
        step(False)

    @pl.when(ki >= first_masked)
    def _():
        step(True)

    @pl.when(ki == first_masked + ratio - 1)
    def _():
        heads = []
        for hd in range(N_HEADS):
            acc_t = acc_ref[hd]
            heads.append(acc_t[:HEAD_DIM] * (1.0 / acc_t[HEAD_DIM:HEAD_DIM + 1]))
        o_t = jnp.concatenate(heads, axis=0)
        ss = jnp.sum(o_t * o_t, axis=0, keepdims=True)
        o_t = o_t * lax.rsqrt(ss * (1.0 / GROUP_WIDTH) + EPS)
        o_ref[...] = (jnp.transpose(o_t) * g_ref[...]).astype(o_ref.dtype)


def _fox(qa, ka, vat, bsz, seq, g):
    n = qa.shape[1]
    tk = min(T_ATT, seq)
    tq = min(T_ATT_Q, seq)
    nq, nk = seq // tq, seq // tk
    ratio = tq // tk
    pairs = [(q, k) for q in range(nq) for k in range(ratio * (q + 1))]
    qt = jnp.asarray([p[0] for p in pairs], I32)
    kt = jnp.asarray([p[1] for p in pairs], I32)
    qmap = lambda b, p, qt, kt: (0, b * nq + qt[p], 0)
    kmap = lambda b, p, qt, kt: (0, b * nk + kt[p], 0)
    grid_spec = pltpu.PrefetchScalarGridSpec(
        num_scalar_prefetch=2,
        grid=(bsz, len(pairs)),
        in_specs=[pl.BlockSpec((N_HEADS, tq, LANES), qmap),
                  pl.BlockSpec((N_HEADS, tk, LANES), kmap),
                  pl.BlockSpec((N_HEADS, LANES, tk), lambda b, p, qt, kt: (0, 0, b * nk + kt[p])),
                  pl.BlockSpec((1, GROUP_WIDTH), lambda b, p, qt, kt: (0, 0))],
        out_specs=pl.BlockSpec((tq, GROUP_WIDTH), lambda b, p, qt, kt: (b * nq + qt[p], 0)),
        scratch_shapes=[pltpu.VMEM((N_HEADS, 1, tq), F32),
                        pltpu.VMEM((N_HEADS, LANES, tq), F32)],
    )
    return pl.pallas_call(
        _fox_kernel,
        grid_spec=grid_spec,
        out_shape=jax.ShapeDtypeStruct((n, GROUP_WIDTH), BF16),
        compiler_params=_cparams(("arbitrary", "arbitrary")),
        name="fox",
    )(qt, kt, qa, ka, vat, g)


def _memkv_kernel(mem_ref, g_ref, w_ref, kg_ref, mk_ref, mvt_ref):
    x = mem_ref[0]
    h = x * lax.rsqrt(jnp.mean(x * x, axis=-1, keepdims=True) + EPS) * g_ref[...]
    kv = jnp.dot(h.astype(BF16), w_ref[...], preferred_element_type=F32)
    m = x.shape[0]
    v_t = jnp.transpose(kv[:, GROUP_WIDTH:])
    tail = jnp.where(lax.broadcasted_iota(I32, (HEAD_DIM, m), 0) == 0, 1.0, 0.0)
    for hd in range(N_HEADS):
        pair, odd = hd // 2, hd % 2
        ks = _head_slab(kv[:, pair * LANES:(pair + 1) * LANES], odd)
        mk_ref[0, hd] = _head_norm(ks, kg_ref[...]).astype(BF16)
        mvt_ref[0, hd] = jnp.concatenate([v_t[hd * HEAD_DIM:(hd + 1) * HEAD_DIM], tail],
                                         axis=0).astype(BF16)


def _memkv(mem, g, w_kv, kg):
    bsz, m, d = mem.shape
    c2 = lambda b: (0, 0)
    out = jax.ShapeDtypeStruct((bsz, N_HEADS, m, LANES), BF16)
    out_t = jax.ShapeDtypeStruct((bsz, N_HEADS, LANES, m), BF16)
    return pl.pallas_call(
        _memkv_kernel,
        grid=(bsz,),
        in_specs=[pl.BlockSpec((1, m, d), lambda b: (b, 0, 0)),
                  pl.BlockSpec((1, d), c2),
                  pl.BlockSpec((d, 2 * GROUP_WIDTH), c2),
                  pl.BlockSpec((1, LANES), c2)],
        out_specs=(pl.BlockSpec((1, N_HEADS, m, LANES), lambda b: (b, 0, 0, 0)),
                   pl.BlockSpec((1, N_HEADS, LANES, m), lambda b: (b, 0, 0, 0))),
        out_shape=(out, out_t),
        compiler_params=_cparams(("arbitrary",)),
        name="memkv",
    )(mem, g, w_kv, kg)


def _memattn_kernel(qm_ref, mk_ref, mvt_ref, g_ref, o_ref):
    heads = []
    for hd in range(N_HEADS):
        s_t = lax.dot_general(mk_ref[0, hd], qm_ref[hd], (((1,), (1,)), ((), ())),
                              preferred_element_type=F32)
        p_t = jnp.exp2(s_t - jnp.max(s_t, axis=0, keepdims=True))
        acc_t = jnp.dot(mvt_ref[0, hd], p_t.astype(BF16), preferred_element_type=F32)
        heads.append(acc_t[:HEAD_DIM] * (1.0 / acc_t[HEAD_DIM:HEAD_DIM + 1]))
    o_t = jnp.concatenate(heads, axis=0)
    ss = jnp.sum(o_t * o_t, axis=0, keepdims=True)
    o_t = o_t * lax.rsqrt(ss * (1.0 / GROUP_WIDTH) + EPS)
    o_ref[...] = (jnp.transpose(o_t) * g_ref[...]).astype(o_ref.dtype)


def _memattn(qm, mk, mv, bsz, seq, g):
    n = qm.shape[1]
    m = mk.shape[2]
    t = min(T_ATT, seq)
    nt = seq // t
    return pl.pallas_call(
        _memattn_kernel,
        grid=(bsz, nt),
        in_specs=[pl.BlockSpec((N_HEADS, t, LANES), lambda b, j: (0, b * nt + j, 0)),
                  pl.BlockSpec((1, N_HEADS, m, LANES), lambda b, j: (b, 0, 0, 0)),
                  pl.BlockSpec((1, N_HEADS, LANES, m), lambda b, j: (b, 0, 0, 0)),
                  pl.BlockSpec((1, GROUP_WIDTH), lambda b, j: (0, 0))],
        out_specs=pl.BlockSpec((t, GROUP_WIDTH), lambda b, j: (b * nt + j, 0)),
        out_shape=jax.ShapeDtypeStruct((n, GROUP_WIDTH), BF16),
        compiler_params=_cparams(("arbitrary", "arbitrary")),
        name="memattn",
    )(qm, mk, mv, g)


def _outproj_kernel(x_ref, ya_ref, yb_ref, yc_ref, ym_ref, w_ref, fg_ref, rw_ref, rb_ref, us_ref,
                    x1_ref, hn_ref, idx_ref, gate_ref, rank_ref, cnt_ref, toff_ref, carry_ref):
    @pl.when(pl.program_id(0) == 0)
    def _():
        carry_ref[...] = jnp.zeros_like(carry_ref)

    merged = jnp.concatenate([ya_ref[0].astype(BF16), ya_ref[1].astype(BF16), yb_ref[...],
                              yc_ref[...], ym_ref[...]], axis=1)
    acc = x_ref[...] + jnp.dot(merged, w_ref[...], preferred_element_type=F32)
    x1_ref[...] = acc
    hn = acc * lax.rsqrt(jnp.mean(acc * acc, axis=-1, keepdims=True) + EPS) * fg_ref[...]
    _store_token_tiles(hn_ref, hn)
    n_exp = rb_ref.shape[0]
    tm = hn.shape[0]
    hn_hi = hn.astype(BF16)
    hn_lo = (hn - hn_hi.astype(F32)).astype(BF16)
    parts = jnp.dot(jnp.concatenate([hn_hi, hn_lo], axis=0), rw_ref[...],
                    preferred_element_type=F32)
    top = jnp.transpose(parts[:tm])
    bot = jnp.transpose(parts[tm:])
    logits = (top[:n_exp] + top[n_exp:2 * n_exp] + bot[:n_exp] + bot[n_exp:2 * n_exp]
              + rb_ref[...])
    e_iota = lax.broadcasted_iota(I32, (n_exp, tm), 0).astype(F32)
    work = logits
    vals, onehots = [], []
    for k in range(TOP_K):
        m = jnp.max(work, axis=0, keepdims=True)
        sel = jnp.min(jnp.where(work == m, e_iota, float(n_exp)), axis=0, keepdims=True)
        hot = e_iota == sel
        idx_ref[k:k + 1, :] = sel.astype(I32)
        vals.append(m)
        onehots.append(hot.astype(F32))
        work = jnp.where(hot, NEG_INF, work)
    exps = [jnp.exp(v - vals[0]) for v in vals]
    denom = exps[0] + exps[1] + exps[2] + exps[3]
    for k in range(TOP_K):
        gate_ref[k:k + 1, :] = exps[k] / denom
    stacked = jnp.concatenate(onehots, axis=0).astype(BF16)
    prefix = jnp.dot(stacked, us_ref[...], preferred_element_type=F32)
    base = carry_ref[...]
    base_sq = jnp.concatenate([base, jnp.zeros((LANES - n_exp, LANES), F32)], axis=0)
    toff_ref[...] = jnp.transpose(base_sq)[:SUBLANES, :]
    for k in range(TOP_K):
        hot = onehots[k]
        pk = prefix[k * n_exp:(k + 1) * n_exp, :]
        rank = jnp.sum(hot * (pk + base[:, 0:1]), axis=0, keepdims=True)
        rank_ref[k:k + 1, :] = rank.astype(I32)
        base = base + jnp.sum(hot, axis=1, keepdims=True)
    carry_ref[...] = base
    cnt_ref[...] = base


def _outproj(x2, ya, yb, yc, ym, w_out, fg, rw, rb, ustrict):
    n, d = x2.shape
    n_exp = rb.shape[0]
    tm = min(TM_OUT, n)
    row = lambda i: (i, 0)
    col = lambda i: (0, i)
    const = lambda i: (0, 0)
    return pl.pallas_call(
        _outproj_kernel,
        grid=(n // tm,),
        in_specs=[pl.BlockSpec((tm, d), row)]
        + [pl.BlockSpec((2, tm, LANES), lambda i: (0, i, 0))]
        + [pl.BlockSpec((tm, GROUP_WIDTH), row)] * 3
        + [pl.BlockSpec((d, d), const),
           pl.BlockSpec((1, d), const),
           pl.BlockSpec((d, LANES), const),
           pl.BlockSpec((n_exp, 1), const),
           pl.BlockSpec((tm, tm), const)],
        out_specs=(pl.BlockSpec((tm, d), row),
                   pl.BlockSpec((tm * ROW_CHUNKS, LANES), row),
                   pl.BlockSpec((TOP_K, tm), col),
                   pl.BlockSpec((TOP_K, tm), col),
                   pl.BlockSpec((TOP_K, tm), col),
                   pl.BlockSpec((n_exp, LANES), const),
                   pl.BlockSpec((SUBLANES, LANES), row)),
        out_shape=(jax.ShapeDtypeStruct((n, d), F32),
                   jax.ShapeDtypeStruct((n * ROW_CHUNKS, LANES), F32),
                   jax.ShapeDtypeStruct((TOP_K, n), I32),
                   jax.ShapeDtypeStruct((TOP_K, n), F32),
                   jax.ShapeDtypeStruct((TOP_K, n), I32),
                   jax.ShapeDtypeStruct((n_exp, LANES), F32),
                   jax.ShapeDtypeStruct((n // tm * SUBLANES, LANES), F32)),
        scratch_shapes=[pltpu.VMEM((n_exp, LANES), F32)],
        compiler_params=_cparams(("arbitrary",)),
        name="outproj",
    )(x2, ya, yb, yc, ym, w_out, fg, rw, rb, ustrict)


def _plan_kernel(cnt_ref, idx_ref, rank_ref, toff_ref, dest_ref, meta_ref, start_ref, win_ref,
                 *, n_exp, block_rows):
    def body(e, off):
        start_ref[e] = off
        return off + (cnt_ref[e] + block_rows - 1) // block_rows * block_rows

    total = lax.fori_loop(0, n_exp, body, jnp.int32(0))
    idx = idx_ref[...]
    dest = rank_ref[...]
    blk_start = lax.broadcasted_iota(I32, meta_ref.shape, 1) * block_rows
    blk_e = jnp.zeros(meta_ref.shape, I32)
    lane = _lane_iota((SUBLANES, LANES))
    start_l = jnp.zeros((SUBLANES, LANES), I32)
    cnt_l = jnp.zeros((SUBLANES, LANES), I32)
    for e in range(n_exp):
        dest = dest + jnp.where(idx == e, start_ref[e], 0)
        start_l = jnp.where(lane == e, start_ref[e], start_l)
        cnt_l = jnp.where(lane == e, cnt_ref[e], cnt_l)
        if e > 0:
            blk_e = blk_e + (blk_start >= start_ref[e]).astype(I32)
    dest_ref[...] = dest
    sub = lax.broadcasted_iota(I32, meta_ref.shape, 0)
    meta_ref[...] = jnp.where(sub == 0, blk_e, total // block_rows)

    toff = toff_ref[...].astype(I32)
    n_rows = toff.shape[0]
    nxt = jnp.concatenate([toff[SUBLANES:], cnt_l], axis=0) if n_rows > SUBLANES else cnt_l
    first = jnp.concatenate([start_l] * (n_rows // SUBLANES), axis=0) + toff
    aligned = jnp.bitwise_and(first, -BF16_ROWS)
    lanes = _lane_iota(toff.shape)
    span = jnp.where(lanes < n_exp, first - aligned + (nxt - toff), 0)
    n_win = jnp.right_shift(span + (WINDOW_ROWS - 1), WINDOW_SHIFT)
    win_ref[...] = jnp.where(lanes < n_exp, aligned, pltpu.roll(n_win, n_exp, axis=1))


def _plan(cnt, idx, rank, toff, n_blocks):
    n_exp = cnt.shape[0]
    nb_pad = (n_blocks + LANES - 1) // LANES * LANES
    return pl.pallas_call(
        functools.partial(_plan_kernel, n_exp=n_exp, block_rows=TM_EXPERT),
        in_specs=[pl.BlockSpec(memory_space=pltpu.SMEM),
                  pl.BlockSpec(memory_space=pltpu.VMEM),
                  pl.BlockSpec(memory_space=pltpu.VMEM),
                  pl.BlockSpec(memory_space=pltpu.VMEM)],
        out_specs=(pl.BlockSpec(memory_space=pltpu.VMEM),
                   pl.BlockSpec(memory_space=pltpu.VMEM),
                   pl.BlockSpec(memory_space=pltpu.SMEM),
                   pl.BlockSpec(memory_space=pltpu.VMEM)),
        out_shape=(jax.ShapeDtypeStruct(idx.shape, I32),
                   jax.ShapeDtypeStruct((SUBLANES, nb_pad), I32),
                   jax.ShapeDtypeStruct((n_exp,), I32),
                   jax.ShapeDtypeStruct(toff.shape, I32)),
        compiler_params=pltpu.CompilerParams(vmem_limit_bytes=VMEM_LIMIT),
        name="plan",
    )(cnt, idx, rank, toff)


def _tile_rows(row):
    return pl.ds(pl.multiple_of(row * ROW_CHUNKS, ROW_CHUNKS), ROW_CHUNKS)


def _dispatch_kernel(dest_ref, cnt_ref, start_ref, hn_ref, xs_ref, zero_ref, sem, zsem,
                     *, n_exp, block_rows):
    tm = dest_ref.shape[1]

    def pad_copy(e, r):
        return pltpu.make_async_copy(zero_ref.at[pl.ds(0, ROW_CHUNKS)],
                                     xs_ref.at[_tile_rows(start_ref[e] + r)], zsem)

    def pad_chunk_copy(e, c):
        rows = pl.ds(pl.multiple_of((start_ref[e] + c * PAD_CHUNK) * ROW_CHUNKS, PAD_CHUNK * ROW_CHUNKS),
                     PAD_CHUNK * ROW_CHUNKS)
        return pltpu.make_async_copy(zero_ref.at[pl.ds(0, PAD_CHUNK * ROW_CHUNKS)], xs_ref.at[rows],
                                     zsem)

    def pad_bounds(e):
        cnt = cnt_ref[e]
        hi = (cnt + block_rows - 1) // block_rows * block_rows
        mid = jnp.minimum((cnt + PAD_CHUNK - 1) // PAD_CHUNK * PAD_CHUNK, hi)
        return cnt, mid, hi

    @pl.when(pl.program_id(0) == 0)
    def _():
        zero_ref[...] = jnp.zeros_like(zero_ref)

        def pad_start(e, c):
            lo, mid, hi = pad_bounds(e)
            lax.fori_loop(lo, mid, lambda r, c: (pad_copy(e, r).start(), c)[1], c)
            return lax.fori_loop(mid // PAD_CHUNK, hi // PAD_CHUNK,
                                 lambda j, c: (pad_chunk_copy(e, j).start(), c)[1], c)

        def pad_wait(e, c):
            lo, mid, hi = pad_bounds(e)
            lax.fori_loop(lo, mid, lambda r, c: (pad_copy(e, r).wait(), c)[1], c)
            return lax.fori_loop(mid // PAD_CHUNK, hi // PAD_CHUNK,
                                 lambda j, c: (pad_chunk_copy(e, j).wait(), c)[1], c)

        lax.fori_loop(0, n_exp, pad_start, 0)
        lax.fori_loop(0, n_exp, pad_wait, 0)

        last = n_exp - 1
        used = start_ref[last] + pad_bounds(last)[2]
        n_blocks = xs_ref.shape[0] // (ROW_CHUNKS * block_rows)

        def tail_copy(b):
            rows = pl.ds(pl.multiple_of(b * (block_rows * ROW_CHUNKS), block_rows * ROW_CHUNKS),
                         block_rows * ROW_CHUNKS)
            return pltpu.make_async_copy(zero_ref, xs_ref.at[rows], zsem)

        lax.fori_loop(used // block_rows, n_blocks, lambda b, c: (tail_copy(b).start(), c)[1], 0)
        lax.fori_loop(used // block_rows, n_blocks, lambda b, c: (tail_copy(b).wait(), c)[1], 0)

    def copy(k, t):
        return pltpu.make_async_copy(hn_ref.at[_tile_rows(t)],
                                     xs_ref.at[_tile_rows(dest_ref[k, t])], sem)

    def start(t, c):
        for k in range(TOP_K):
            copy(k, t).start(priority=k % 2)
        return c

    def wait(t, c):
        for k in range(TOP_K):
            copy(k, t).wait()
        return c

    lax.fori_loop(0, tm, start, 0, unroll=4)
    lax.fori_loop(0, tm, wait, 0, unroll=8)


def _dispatch(dest, cnt, starts, hn, cap):
    n = hn.shape[0] // ROW_CHUNKS
    tm = min(T_MOVE, n)
    return pl.pallas_call(
        functools.partial(_dispatch_kernel, n_exp=cnt.shape[0], block_rows=TM_EXPERT),
        grid=(n // tm,),
        in_specs=[pl.BlockSpec((TOP_K, tm), lambda i: (0, i), memory_space=pltpu.SMEM),
                  pl.BlockSpec(memory_space=pltpu.SMEM),
                  pl.BlockSpec(memory_space=pltpu.SMEM),
                  pl.BlockSpec((tm * ROW_CHUNKS, LANES), lambda i: (i, 0))],
        out_specs=pl.BlockSpec(memory_space=pl.ANY),
        out_shape=jax.ShapeDtypeStruct((cap * ROW_CHUNKS, LANES), hn.dtype),
        scratch_shapes=[pltpu.VMEM((TM_EXPERT * ROW_CHUNKS, LANES), hn.dtype),
                        pltpu.SemaphoreType.DMA, pltpu.SemaphoreType.DMA],
        compiler_params=_cparams(("arbitrary",)),
        name="dispatch",
    )(dest, cnt, starts, hn)


def _experts_kernel(meta_ref, xs_ref, wg_ref, bg_ref, wu_ref, bu_ref, wd_ref, bd_ref, ys_ref,
                    wg_s, wu_s, wd_s):
    i = pl.program_id(0)
    tm = xs_ref.shape[0] // ROW_CHUNKS
    in_use = i < meta_ref[1, 0]
    new_expert = jnp.logical_or(i == 0, meta_ref[0, i] != meta_ref[0, jnp.maximum(i - 1, 0)])

    @pl.when(jnp.logical_not(in_use))
    def _():
        ys_ref[...] = jnp.zeros_like(ys_ref)

    @pl.when(jnp.logical_and(in_use, new_expert))
    def _():
        wg_s[...] = wg_ref[0, 0].astype(BF16)
        wu_s[...] = wu_ref[0, 0].astype(BF16)
        wd_s[...] = wd_ref[0, 0].astype(BF16)

    @pl.when(in_use)
    def _():
        x = _load_token_tiles(xs_ref, tm).astype(BF16)
        gl = jnp.dot(x, wg_s[...], preferred_element_type=F32) + bg_ref[0, 0]
        up = jnp.dot(x, wu_s[...], preferred_element_type=F32) + bu_ref[0, 0]
        gl = jnp.minimum(gl, SWIGLU_LIMIT)
        up = jnp.clip(up, -SWIGLU_LIMIT, SWIGLU_LIMIT)
        act = gl * (1.0 / (1.0 + jnp.exp(-SWIGLU_ALPHA * gl)))
        hmid = ((up + 1.0) * act).astype(BF16)
        y = jnp.dot(hmid, wd_s[...], preferred_element_type=F32) + bd_ref[0, 0]
        ys_ref[...] = y.astype(ys_ref.dtype)


def _experts(meta, xs, layer, wg, bg, wu, bu, wd, bd):
    cap = xs.shape[0] // ROW_CHUNKS
    _, _, d, de = wg.shape
    tm = TM_EXPERT
    n_blocks = cap // tm

    def blk(i, meta):
        return jnp.minimum(i, meta[1, 0] - 1)

    rows = lambda i, meta: (blk(i, meta), 0)
    wmap = lambda i, meta: (layer, meta[0, blk(i, meta)], 0, 0)
    grid_spec = pltpu.PrefetchScalarGridSpec(
        num_scalar_prefetch=1,
        grid=(n_blocks,),
        in_specs=[pl.BlockSpec((tm * ROW_CHUNKS, LANES), rows),
                  pl.BlockSpec((1, 1, d, de), wmap),
                  pl.BlockSpec((1, 1, 1, de), wmap),
                  pl.BlockSpec((1, 1, d, de), wmap),
                  pl.BlockSpec((1, 1, 1, de), wmap),
                  pl.BlockSpec((1, 1, de, d), wmap),
                  pl.BlockSpec((1, 1, 1, d), wmap)],
        out_specs=pl.BlockSpec((tm, d), lambda i, meta: (i, 0)),
        scratch_shapes=[pltpu.VMEM((d, de), BF16), pltpu.VMEM((d, de), BF16),
                        pltpu.VMEM((de, d), BF16)],
    )
    return pl.pallas_call(
        _experts_kernel,
        grid_spec=grid_spec,
        out_shape=jax.ShapeDtypeStruct((cap, d), BF16),
        compiler_params=_cparams(("arbitrary",)),
        name="experts",
    )(meta, xs, wg, bg, wu, bu, wd, bd)


def _combine_kernel(win_ref, dest_ref, idx_ref, gate_ref, x1_ref, ys_ref, o_ref, buf_ref, one_ref,
                    sem, one_sem, *, n_exp):
    i = pl.program_id(0)
    n_tiles = pl.num_programs(0)
    tm = x1_ref.shape[0]
    last_window = ys_ref.shape[0] - WINDOW_ROWS

    def window_start(tile, e, rnd):
        return jnp.minimum(win_ref[tile, e] + rnd * WINDOW_ROWS, last_window)

    def ys_window(tile, e, rnd):
        return ys_ref.at[pl.ds(pl.multiple_of(window_start(tile, e, rnd), BF16_ROWS), WINDOW_ROWS)]

    def buf_window(slot, w):
        return buf_ref.at[slot, pl.ds(pl.multiple_of(w * WINDOW_ROWS, WINDOW_ROWS), WINDOW_ROWS)]

    def for_spills(tile, fn):
        def body(e, j):
            hit = jnp.logical_and(win_ref[tile, n_exp + e] > 1, j < SPILL_WINDOWS)

            @pl.when(hit)
            def _():
                fn(e, j)

            return j + hit.astype(I32)

        lax.fori_loop(0, n_exp, body, jnp.int32(0))

    def fetch(tile, slot):
        for e in range(n_exp):
            pltpu.make_async_copy(ys_window(tile, e, 0), buf_window(slot, e),
                                  sem.at[slot]).start(priority=e % 2)
        for_spills(tile, lambda e, j: pltpu.make_async_copy(
            ys_window(tile, e, 1), buf_window(slot, n_exp + j), sem.at[slot]).start())

    def drain(tile, slot):
        for e in range(n_exp):
            pltpu.make_async_copy(ys_window(tile, e, 0), buf_window(slot, e), sem.at[slot]).wait()
        for_spills(tile, lambda e, j: pltpu.make_async_copy(
            ys_window(tile, e, 1), buf_window(slot, n_exp + j), sem.at[slot]).wait())

    @pl.when(i == 0)
    def _():
        fetch(0, 0)

    @pl.when(i + 1 < n_tiles)
    def _():
        fetch(i + 1, (i + 1) % 2)

    idx = idx_ref[...]
    dest = dest_ref[...]
    gates = gate_ref[...]
    chunk = SEL_CHUNK_WINDOWS * WINDOW_ROWS
    lane = _lane_iota((tm, chunk)).astype(F32)

    first_tok = jnp.zeros(idx.shape, I32)
    clamped_tok = jnp.zeros(idx.shape, I32)
    for e in range(n_exp):
        first_tok = jnp.where(idx == e, win_ref[i, e], first_tok)
        clamped_tok = jnp.where(idx == e, window_start(i, e, 0), clamped_tok)
    local = dest - clamped_tok
    col = jnp.where(dest - first_tok < WINDOW_ROWS, idx * WINDOW_ROWS + local, -1)
    key = idx * KEY_STRIDE + (dest - first_tok)
    packed = jnp.concatenate([col.astype(F32), gates, key.astype(F32),
                              jnp.zeros((LANES - 3 * TOP_K, tm), F32)], axis=0)
    by_tok = jnp.transpose(packed)

    def spread(j, width):
        one = jnp.broadcast_to(by_tok[:, j:j + 1], (tm, LANES))
        return jnp.concatenate([one] * (width // LANES), axis=1) if width > LANES else one

    slot = i % 2
    drain(i, slot)
    cols = [spread(k, chunk) for k in range(TOP_K)]
    gts = [spread(TOP_K + k, chunk) for k in range(TOP_K)]
    total = x1_ref[...]
    for c in range(n_exp // SEL_CHUNK_WINDOWS):
        sel = jnp.zeros((tm, chunk), F32)
        for k in range(TOP_K):
            sel = jnp.where(lane == cols[k] - float(c * chunk), gts[k], sel)
        total = total + jnp.dot(sel.astype(BF16), buf_ref[slot, c * chunk:(c + 1) * chunk, :],
                                preferred_element_type=F32)
    o_ref[...] = total

    lane1 = _lane_iota((tm, LANES))
    keys1 = [spread(2 * TOP_K + k, LANES) for k in range(TOP_K)]
    gts1 = [spread(TOP_K + k, LANES) for k in range(TOP_K)]

    def add_window(e, rnd, rows):
        shift = win_ref[i, e] + rnd * WINDOW_ROWS - window_start(i, e, rnd)
        want = jnp.where(lane1 >= shift, lane1 + (e * KEY_STRIDE + rnd * WINDOW_ROWS - shift),
                         -1).astype(F32)
        sel = jnp.zeros((tm, LANES), F32)
        for k in range(TOP_K):
            sel = jnp.where(want == keys1[k], gts1[k], sel)
        o_ref[...] += jnp.dot(sel.astype(BF16), rows, preferred_element_type=F32)

    for_spills(i, lambda e, j: add_window(e, 1, buf_window(slot, n_exp + j)[...]))

    def on_demand(e, j):
        n_win = win_ref[i, n_exp + e]
        hit = n_win > 1
        ahead = jnp.logical_and(hit, j < SPILL_WINDOWS)

        def one(rnd, c):
            cp = pltpu.make_async_copy(ys_window(i, e, rnd), one_ref, one_sem)
            cp.start()
            cp.wait()
            add_window(e, rnd, one_ref[...])
            return c

        lax.fori_loop(jnp.where(ahead, 2, 1), n_win, one, 0)
        return j + hit.astype(I32)

    lax.fori_loop(0, n_exp, on_demand, jnp.int32(0))


def _combine(win, dest, idx, gates, x1, ys, n_exp):
    n, d = x1.shape
    tm = min(TM_OUT, n)
    vec = lambda i, win: (0, i)
    grid_spec = pltpu.PrefetchScalarGridSpec(
        num_scalar_prefetch=1,
        grid=(n // tm,),
        in_specs=[pl.BlockSpec((TOP_K, tm), vec),
                  pl.BlockSpec((TOP_K, tm), vec),
                  pl.BlockSpec((TOP_K, tm), vec),
                  pl.BlockSpec((tm, d), lambda i, win: (i, 0)),
                  pl.BlockSpec(memory_space=pl.ANY)],
        out_specs=pl.BlockSpec((tm, d), lambda i, win: (i, 0)),
        scratch_shapes=[pltpu.VMEM((2, (n_exp + SPILL_WINDOWS) * WINDOW_ROWS, d), ys.dtype),
                        pltpu.VMEM((WINDOW_ROWS, d), ys.dtype),
                        pltpu.SemaphoreType.DMA((2,)),
                        pltpu.SemaphoreType.DMA],
    )
    return pl.pallas_call(
        functools.partial(_combine_kernel, n_exp=n_exp),
        grid_spec=grid_spec,
        out_shape=jax.ShapeDtypeStruct((n, d), F32),
        compiler_params=_cparams(("arbitrary",)),
        name="combine",
    )(win, dest, idx, gates, x1, ys)


def _pad_lanes(v, width=LANES):
    v = v.astype(F32).reshape(1, -1)
    return jnp.pad(v, ((0, 0), (0, width - v.shape[1])))


def _layer(x2, mem, bsz, seq, p, layer, experts):
    n, d = x2.shape
    f32 = F32
    w_in = p['w_in']
    off_q = 2 * GROUP_WIDTH
    off_k, off_v, off_f = off_q + GROUP_WIDTH, off_q + 2 * GROUP_WIDTH, off_q + 3 * GROUP_WIDTH
    off_qm = off_f + N_HEADS
    w_f = w_in[:, off_f:off_qm]
    w_f_pad = jnp.pad(jnp.concatenate([w_f, w_f, w_f], axis=1), ((0, 0), (0, LANES - 3 * N_HEADS)))
    w_all = jnp.concatenate([w_in[:, :off_f], w_in[:, off_qm:], w_f_pad], axis=1).astype(BF16)
    fb = p['fox_forget_b'].astype(f32)
    fb_pad = _pad_lanes(jnp.concatenate([fb, fb, fb]))
    tm_in = min(TM_INPROJ, seq)
    tri = jnp.tril(jnp.ones((tm_in, tm_in), f32)).astype(BF16)

    sel = jnp.zeros((3, LANES, N_HEADS, LANES), f32)
    for part in range(3):
        for hd in range(N_HEADS):
            sel = sel.at[part, hd, hd, HEAD_DIM + part].set(1.0)
            sel = sel.at[part, hd, hd, HEAD_DIM + 3 + part].set(-1.0)
    sel = sel.reshape(3 * LANES, N_HEADS * LANES).astype(BF16)
    head_of = jnp.arange(GROUP_WIDTH) // HEAD_DIM
    ones_blk = (head_of[:, None] == head_of[None, :]).astype(BF16)
    ones2 = jnp.concatenate([ones_blk, ones_blk], axis=0)

    def head_gain(g):
        return jnp.tile(g.astype(f32).reshape(1, HEAD_DIM), (1, N_HEADS))

    ua, ub, qa, ka, va, qm = _inproj(
        x2, seq, p['mix_norm_g'].reshape(1, d).astype(f32), w_all,
        head_gain(p['fox_q_g']), head_gain(p['fox_k_g']), head_gain(p['mem_q_g']), fb_pad, tri,
        sel, ones2)

    bmat, coef, ssm_pw, cmat = _s5_constants(
        p['ssm_lambda_re'], p['ssm_lambda_im'], p['ssm_log_dt'], p['ssm_b_re'], p['ssm_b_im'],
        p['ssm_c_re'], p['ssm_c_im'], min(T_S5, seq))
    ya = _s5(ua, bsz, seq, bmat, coef, ssm_pw, cmat,
             p['ssm_d'].reshape(1, GROUP_WIDTH).astype(f32), p['ssm_glu_w'].astype(BF16),
             p['ssm_glu_b'].reshape(1, GROUP_WIDTH).astype(f32),
             p['ssm_out_g'].reshape(1, GROUP_WIDTH).astype(f32))

    pw = p['pool_w'].astype(f32)
    w_blk = jnp.zeros((GROUP_WIDTH, GROUP_WIDTH), f32)
    for gi in range(len(POOL_WINDOWS)):
        w_blk = w_blk.at[gi * HEAD_DIM:(gi + 1) * HEAD_DIM, gi * HEAD_DIM:(gi + 1) * HEAD_DIM].set(pw[gi])
    yb = _pool(ub, bsz, seq, w_blk.astype(BF16), p['pool_scale'].reshape(1, GROUP_WIDTH).astype(f32))

    yc = _fox(qa, ka, va, bsz, seq, p['fox_out_g'].reshape(1, GROUP_WIDTH).astype(f32))

    mk, mv = _memkv(mem, p['mem_norm_g'].reshape(1, d).astype(f32), p['mem_w_kv'].astype(BF16),
                    _pad_lanes(p['mem_k_g']))
    ym = _memattn(qm, mk, mv, bsz, seq, p['mem_out_g'].reshape(1, GROUP_WIDTH).astype(f32))

    n_exp = p['router_w'].shape[1]
    tm_out = min(TM_OUT, n)
    ustrict = jnp.triu(jnp.ones((tm_out, tm_out), f32), k=1).astype(BF16)
    rw = p['router_w'].astype(f32)
    rw_hi = rw.astype(BF16)
    rw_lo = (rw - rw_hi.astype(f32)).astype(BF16)
    rw_parts = jnp.pad(jnp.concatenate([rw_hi, rw_lo], axis=1), ((0, 0), (0, LANES - 2 * n_exp)))
    x1, hn, idx, gates, rank, cnt, toff = _outproj(
        x2, ya, yb, yc, ym, p['w_out'].astype(BF16), p['ffn_norm_g'].reshape(1, d).astype(f32),
        rw_parts, p['router_b'].reshape(n_exp, 1).astype(f32), ustrict)

    cap = n * TOP_K + n_exp * TM_EXPERT
    cnt_i = cnt[:, 0].astype(I32)
    dest, meta, starts, win = _plan(cnt_i, idx, rank, toff, cap // TM_EXPERT)
    xs = _dispatch(dest, cnt_i, starts, hn, cap)
    ys = _experts(meta, xs, layer, *experts)
    win_tab = win[::SUBLANES, :2 * n_exp]
    return _combine(win_tab, dest, idx, gates, x1, ys, n_exp)


_PARAM_NAMES = ('mix_norm_g', 'w_in', 'ssm_lambda_re', 'ssm_lambda_im', 'ssm_log_dt',
                'ssm_b_re', 'ssm_b_im', 'ssm_c_re', 'ssm_c_im', 'ssm_d', 'ssm_glu_w', 'ssm_glu_b',
                'ssm_out_g', 'pool_w', 'pool_scale', 'fox_forget_b', 'fox_q_g', 'fox_k_g',
                'fox_out_g', 'mem_norm_g', 'mem_w_kv', 'mem_q_g', 'mem_k_g', 'mem_out_g', 'w_out',
                'ffn_norm_g', 'router_w', 'router_b', 'exp_w_gate', 'exp_b_gate', 'exp_w_up',
                'exp_b_up', 'exp_w_down', 'exp_b_down')


def kernel(x, mem, mix_norm_g, w_in, ssm_lambda_re, ssm_lambda_im, ssm_log_dt, ssm_b_re, ssm_b_im,
           ssm_c_re, ssm_c_im, ssm_d, ssm_glu_w, ssm_glu_b, ssm_out_g, pool_w, pool_scale,
           fox_forget_b, fox_q_g, fox_k_g, fox_out_g, mem_norm_g, mem_w_kv, mem_q_g, mem_k_g,
           mem_out_g, w_out, ffn_norm_g, router_w, router_b, exp_w_gate, exp_b_gate, exp_w_up,
           exp_b_up, exp_w_down, exp_b_down):
    stacked = dict(zip(_PARAM_NAMES, (
        mix_norm_g, w_in, ssm_lambda_re, ssm_lambda_im, ssm_log_dt, ssm_b_re, ssm_b_im, ssm_c_re,
        ssm_c_im, ssm_d, ssm_glu_w, ssm_glu_b, ssm_out_g, pool_w, pool_scale, fox_forget_b,
        fox_q_g, fox_k_g, fox_out_g, mem_norm_g, mem_w_kv, mem_q_g, mem_k_g, mem_out_g, w_out,
        ffn_norm_g, router_w, router_b, exp_w_gate, exp_b_gate, exp_w_up, exp_b_up, exp_w_down,
        exp_b_down)))
    bsz, seq, d = x.shape
    depth = w_in.shape[0]
    x2 = x.reshape(bsz * seq, d).astype(F32)
    mem = mem.astype(F32)
    expert_names = ('exp_w_gate', 'exp_b_gate', 'exp_w_up', 'exp_b_up', 'exp_w_down', 'exp_b_down')
    experts = tuple(stacked[k].astype(F32) if stacked[k].ndim == 4
                    else stacked[k].astype(F32)[:, :, None, :] for k in expert_names)
    for layer in range(depth):
        x2 = _layer(x2, mem, bsz, seq,
                    {k: v[layer] for k, v in stacked.items() if k not in expert_names},
                    layer, experts)
    return x2.reshape(bsz, seq, d).astype(x.dtype)
```

```python
import functools
import math

import jax
import jax.numpy as jnp
from jax import lax
from jax.experimental import pallas as pl
from jax.experimental.pallas import tpu as pltpu

F32 = jnp.float32
BF16 = jnp.bfloat16
I32 = jnp.int32

EPS = 1e-6
HEAD_DIM = 64
N_HEADS = 4
GROUP_WIDTH = 256
LANES = 128
SUBLANES = 8
ROW_CHUNKS = 8
SSM_GROUPS = 16
SSM_CH = 16
SSM_STATE = 64
SSM_LANES = SSM_GROUPS * SSM_STATE
POOL_WINDOWS = (2, 4, 8, 16)
POOL_HALO = 16
TOP_K = 4
SWIGLU_LIMIT = 7.0
SWIGLU_ALPHA = 1.702
VMEM_LIMIT = 56 * 1024 * 1024

TM_INPROJ = 512
T_S5 = 512
T_POOL = 4096
T_ATT = 512
T_ATT_Q = 1024
TM_OUT = 512
TM_EXPERT = 512
T_MOVE = 256
PAD_CHUNK = 64
BF16_ROWS = 16
WINDOW_SHIFT = 7
WINDOW_ROWS = 1 << WINDOW_SHIFT
SEL_CHUNK_WINDOWS = 8
SPILL_WINDOWS = 8
KEY_STRIDE = 1 << 16

NEG_INF = float("-inf")
LOG2E = 1.4426950408889634


def _cparams(sem):
    return pltpu.CompilerParams(dimension_semantics=sem, vmem_limit_bytes=VMEM_LIMIT)


def _lane_iota(shape):
    return lax.broadcasted_iota(I32, shape, len(shape) - 1)


def _split3(x):
    hi = x.astype(BF16).astype(F32)
    r = x - hi
    mid = r.astype(BF16).astype(F32)
    lo = r - mid
    return hi, mid, lo


def _head_slab(slab, odd):
    lane = _lane_iota(slab.shape)
    if odd:
        slab = pltpu.roll(slab, HEAD_DIM, axis=1)
    return jnp.where(lane < HEAD_DIM, slab, 0.0)


def _head_norm(xh, gain):
    ss = jnp.sum(xh * xh, axis=-1, keepdims=True)
    return xh * jnp.broadcast_to(lax.rsqrt(ss * (1.0 / HEAD_DIM) + EPS), xh.shape) * gain


def _divide_by_denominator(acc):
    inv = 1.0 / acc[:, HEAD_DIM:HEAD_DIM + 1]
    return acc * jnp.broadcast_to(inv, acc.shape)


def _join_heads(o_even, o_odd):
    lane = _lane_iota(o_even.shape)
    return jnp.where(lane < HEAD_DIM, o_even, pltpu.roll(o_odd, HEAD_DIM, axis=1))


def _store_token_tiles(ref, val):
    t = val.shape[0]
    for s in range(ROW_CHUNKS):
        ref[pl.ds(s, t, stride=ROW_CHUNKS), :] = val[:, s * LANES:(s + 1) * LANES]


def _load_token_tiles(ref, t):
    return jnp.concatenate([ref[pl.ds(s, t, stride=ROW_CHUNKS), :] for s in range(ROW_CHUNKS)],
                           axis=1)


def _group_norm_pairs(pairs, gain_ref, out_ref):
    ss = jnp.sum(sum(p * p for p in pairs), axis=-1, keepdims=True)
    scale = jnp.broadcast_to(lax.rsqrt(ss * (1.0 / GROUP_WIDTH) + EPS), pairs[0].shape)
    for i, p in enumerate(pairs):
        sl = slice(i * LANES, (i + 1) * LANES)
        out_ref[:, sl] = (p * scale * gain_ref[:, sl]).astype(out_ref.dtype)


COL_A, COL_B, COL_Q, COL_K, COL_V, COL_QM, COL_F = 0, 256, 512, 768, 1024, 1280, 1536
IN_COLS_PADDED = COL_F + LANES


def _inproj_kernel(x_ref, g_ref, w_ref, qg_ref, kg_ref, mqg_ref, fb_ref, tri_ref, sel_ref, ones_ref,
                   ua_ref, ub_ref, qa_ref, ka_ref, vat_ref, qm_ref, carry_ref, *, tiles_per_seq):
    i = pl.program_id(0)

    @pl.when(i % tiles_per_seq == 0)
    def _():
        carry_ref[...] = jnp.zeros_like(carry_ref)

    x = x_ref[...]
    h = x * lax.rsqrt(jnp.mean(x * x, axis=-1, keepdims=True) + EPS) * g_ref[...]
    proj = jnp.dot(h.astype(BF16), w_ref[...], preferred_element_type=F32)
    ua_ref[0] = proj[:, COL_A:COL_A + LANES]
    ua_ref[1] = proj[:, COL_A + LANES:COL_A + GROUP_WIDTH]
    ub_ref[...] = proj[:, COL_B:COL_B + GROUP_WIDTH]

    z = proj[:, COL_F:COL_F + LANES] + fb_ref[...]
    lane = _lane_iota(z.shape)
    logf = jnp.minimum(z, 0.0) - jnp.log(1.0 + jnp.exp(-jnp.abs(z)))
    hi, mid, lo = _split3(logf)
    packed = jnp.where(lane < 4, hi, jnp.where(lane < 8, mid, jnp.where(lane < 12, lo, 0.0)))
    cs = jnp.dot(tri_ref[...], packed.astype(BF16), preferred_element_type=F32)
    cum = cs + pltpu.roll(cs, LANES - 4, axis=1) + pltpu.roll(cs, LANES - 8, axis=1)
    cum = cum + carry_ref[...]
    carry_ref[...] = cum[cum.shape[0] - 1:, :]

    scale = HEAD_DIM ** -0.5 * LOG2E
    c_parts = jnp.concatenate(_split3(cum * LOG2E), axis=1).astype(BF16)
    bias = jnp.dot(c_parts, sel_ref[...], preferred_element_type=F32)

    def heads_normed(col, gain_ref):
        t = proj[:, col:col + GROUP_WIDTH]
        sq = t * t
        sq_hi = sq.astype(BF16)
        sq_lo = (sq - sq_hi.astype(F32)).astype(BF16)
        ss = jnp.dot(jnp.concatenate([sq_hi, sq_lo], axis=1), ones_ref[...],
                     preferred_element_type=F32)
        return t * lax.rsqrt(ss * (1.0 / HEAD_DIM) + EPS) * gain_ref[...]

    qn_all = heads_normed(COL_Q, qg_ref) * scale
    kn_all = heads_normed(COL_K, kg_ref)
    mn_all = heads_normed(COL_QM, mqg_ref) * scale
    q_bias = (lane >= 64) & (lane < 67)
    k_bias = (lane >= 67) & (lane < 70)
    for hd in range(N_HEADS):
        pair, odd = hd // 2, hd % 2
        pair_sl = slice(pair * LANES, (pair + 1) * LANES)
        b = bias[:, hd * LANES:(hd + 1) * LANES]
        qn = _head_slab(qn_all[:, pair_sl], odd)
        qa_ref[hd] = jnp.where(q_bias, b, jnp.where(k_bias, 1.0, qn)).astype(BF16)
        kn = _head_slab(kn_all[:, pair_sl], odd)
        ka_ref[hd] = jnp.where(k_bias, b, jnp.where(q_bias, 1.0, kn)).astype(BF16)
        qm_ref[hd] = _head_slab(mn_all[:, pair_sl], odd).astype(BF16)
    tm = proj.shape[0]
    v_t = jnp.transpose(proj[:, COL_V:COL_V + GROUP_WIDTH])
    tail = jnp.where(lax.broadcasted_iota(I32, (HEAD_DIM, tm), 0) == 0, 1.0, 0.0)
    for hd in range(N_HEADS):
        vat_ref[hd] = jnp.concatenate([v_t[hd * HEAD_DIM:(hd + 1) * HEAD_DIM], tail],
                                      axis=0).astype(BF16)


def _inproj(x2, seq, mix_g, w_all, qg, kg, mqg, fb, tri, sel, ones):
    n, d = x2.shape
    tm = min(TM_INPROJ, seq)
    grid = (n // tm,)
    const = lambda i: (0, 0)
    row = lambda i: (i, 0)
    hrow = lambda i: (0, i, 0)
    out_shape = (
        jax.ShapeDtypeStruct((2, n, LANES), F32),
        jax.ShapeDtypeStruct((n, GROUP_WIDTH), F32),
        jax.ShapeDtypeStruct((N_HEADS, n, LANES), BF16),
        jax.ShapeDtypeStruct((N_HEADS, n, LANES), BF16),
        jax.ShapeDtypeStruct((N_HEADS, LANES, n), BF16),
        jax.ShapeDtypeStruct((N_HEADS, n, LANES), BF16),
    )
    return pl.pallas_call(
        functools.partial(_inproj_kernel, tiles_per_seq=seq // tm),
        grid=grid,
        in_specs=[
            pl.BlockSpec((tm, d), row),
            pl.BlockSpec((1, d), const),
            pl.BlockSpec((d, IN_COLS_PADDED), const),
            pl.BlockSpec((1, GROUP_WIDTH), const),
            pl.BlockSpec((1, GROUP_WIDTH), const),
            pl.BlockSpec((1, GROUP_WIDTH), const),
            pl.BlockSpec((1, LANES), const),
            pl.BlockSpec((tm, tm), const),
            pl.BlockSpec((3 * LANES, N_HEADS * LANES), const),
            pl.BlockSpec((2 * GROUP_WIDTH, GROUP_WIDTH), const),
        ],
        out_specs=(
            pl.BlockSpec((2, tm, LANES), hrow),
            pl.BlockSpec((tm, GROUP_WIDTH), row),
            pl.BlockSpec((N_HEADS, tm, LANES), hrow),
            pl.BlockSpec((N_HEADS, tm, LANES), hrow),
            pl.BlockSpec((N_HEADS, LANES, tm), lambda i: (0, 0, i)),
            pl.BlockSpec((N_HEADS, tm, LANES), hrow),
        ),
        out_shape=out_shape,
        scratch_shapes=[pltpu.VMEM((1, LANES), F32)],
        compiler_params=_cparams(("arbitrary",)),
        name="inproj",
    )(x2, mix_g, w_all, qg, kg, mqg, fb, tri, sel, ones)


def _s5_kernel(u_ref, bmat_ref, coef_ref, pw_ref, cmat_ref, d_ref, gluw_ref, glub_ref, og_ref,
               o_ref, u_scr, x_scr, y_scr, carry_ref):
    @pl.when(pl.program_id(1) == 0)
    def _():
        carry_ref[...] = jnp.zeros_like(carry_ref)

    t = u_ref.shape[1]
    g = t // SUBLANES
    halves = [slice(h * LANES, (h + 1) * LANES) for h in range(GROUP_WIDTH // LANES)]
    for j in range(g):
        for h, sl in enumerate(halves):
            u_scr[j * SUBLANES:(j + 1) * SUBLANES, sl] = u_ref[h, pl.ds(j, SUBLANES, stride=g), :]
    u = u_scr[...]
    x_scr[...] = jnp.dot(u.astype(BF16), bmat_ref[...], preferred_element_type=F32)
    n_lane_blocks = SSM_LANES // LANES
    re_sl = [slice(lb * LANES, (lb + 1) * LANES) for lb in range(n_lane_blocks)]
    im_sl = [slice(SSM_LANES + lb * LANES, SSM_LANES + (lb + 1) * LANES)
             for lb in range(n_lane_blocks)]

    def group_rows(j):
        return pl.ds(pl.multiple_of(j * SUBLANES, SUBLANES), SUBLANES)

    def pass1(j, h):
        rows = group_rows(j)
        new = []
        for lb in range(n_lane_blocks):
            ar = coef_ref[0, :, re_sl[lb]]
            ai = coef_ref[1, :, re_sl[lb]]
            hr, hi = h[2 * lb], h[2 * lb + 1]
            nr = ar * hr - ai * hi + x_scr[rows, re_sl[lb]]
            ni = ar * hi + ai * hr + x_scr[rows, im_sl[lb]]
            x_scr[rows, re_sl[lb]] = nr
            x_scr[rows, im_sl[lb]] = ni
            new += [nr, ni]
        return tuple(new)

    zero = jnp.zeros((SUBLANES, LANES), F32)
    finals = lax.fori_loop(0, g, pass1, (zero,) * (2 * n_lane_blocks))

    row_id = lax.broadcasted_iota(I32, (SUBLANES, LANES), 0)
    starts = []
    for lb in range(n_lane_blocks):
        fr, fi = finals[2 * lb], finals[2 * lb + 1]
        for s, k in enumerate((1, 2, 4)):
            cr = coef_ref[2 + 2 * s, :, re_sl[lb]]
            ci = coef_ref[3 + 2 * s, :, re_sl[lb]]
            sr = pltpu.roll(fr, k, axis=0)
            si = pltpu.roll(fi, k, axis=0)
            fr, fi = fr + cr * sr - ci * si, fi + cr * si + ci * sr
        qr = coef_ref[8, :, re_sl[lb]]
        qi = coef_ref[9, :, re_sl[lb]]
        cbr = carry_ref[:, re_sl[lb]]
        cbi = carry_ref[:, im_sl[lb]]
        fr, fi = fr + qr * cbr - qi * cbi, fi + qr * cbi + qi * cbr
        starts.append(jnp.where(row_id == 0, cbr, pltpu.roll(fr, 1, axis=0)))
        starts.append(jnp.where(row_id == 0, cbi, pltpu.roll(fi, 1, axis=0)))
        carry_ref[:, re_sl[lb]] = jnp.broadcast_to(fr[SUBLANES - 1:, :], fr.shape)
        carry_ref[:, im_sl[lb]] = jnp.broadcast_to(fi[SUBLANES - 1:, :], fi.shape)

    def pass2(j, c):
        rows = group_rows(j)
        for lb in range(n_lane_blocks):
            pr = pw_ref[rows, re_sl[lb]]
            pi_ = pw_ref[rows, im_sl[lb]]
            sr, si = starts[2 * lb], starts[2 * lb + 1]
            x_scr[rows, re_sl[lb]] += pr * sr - pi_ * si
            x_scr[rows, im_sl[lb]] += pr * si + pi_ * sr
        return c

    lax.fori_loop(0, g, pass2, 0)

    y = jnp.dot(x_scr[...].astype(BF16), cmat_ref[...], preferred_element_type=F32) + d_ref[...] * u
    z = jax.nn.gelu(y, approximate=True)
    gate = jnp.dot(z.astype(BF16), gluw_ref[...], preferred_element_type=F32) + glub_ref[...]
    out = z * (1.0 / (1.0 + jnp.exp(-gate)))
    _group_norm_pairs([out[:, :LANES], out[:, LANES:]], og_ref, y_scr)
    for j in range(g):
        for h, sl in enumerate(halves):
            o_ref[h, pl.ds(j, SUBLANES, stride=g), :] = y_scr[j * SUBLANES:(j + 1) * SUBLANES, sl]


def _s5(ua, bsz, seq, bmat, coef, pw, cmat, dvec, gluw, glub, og):
    n = ua.shape[1]
    t = pw.shape[0]
    nt = seq // t
    row = lambda b, j: (0, b * nt + j, 0)
    c2 = lambda b, j: (0, 0)
    c3 = lambda b, j: (0, 0, 0)
    return pl.pallas_call(
        _s5_kernel,
        grid=(bsz, nt),
        in_specs=[
            pl.BlockSpec((2, t, LANES), row),
            pl.BlockSpec((GROUP_WIDTH, 2 * SSM_LANES), c2),
            pl.BlockSpec((10, SUBLANES, SSM_LANES), c3),
            pl.BlockSpec((t, 2 * SSM_LANES), c2),
            pl.BlockSpec((2 * SSM_LANES, GROUP_WIDTH), c2),
            pl.BlockSpec((1, GROUP_WIDTH), c2),
            pl.BlockSpec((GROUP_WIDTH, GROUP_WIDTH), c2),
            pl.BlockSpec((1, GROUP_WIDTH), c2),
            pl.BlockSpec((1, GROUP_WIDTH), c2),
        ],
        out_specs=pl.BlockSpec((2, t, LANES), row),
        out_shape=jax.ShapeDtypeStruct((2, n, LANES), F32),
        scratch_shapes=[pltpu.VMEM((t, GROUP_WIDTH), F32),
                        pltpu.VMEM((t, 2 * SSM_LANES), F32),
                        pltpu.VMEM((t, GROUP_WIDTH), F32),
                        pltpu.VMEM((SUBLANES, 2 * SSM_LANES), F32)],
        compiler_params=_cparams(("arbitrary", "arbitrary")),
        name="s5",
    )(ua, bmat, coef, pw, cmat, dvec, gluw, glub, og)


def _s5_constants(lam_re, lam_im, log_dt, b_re, b_im, c_re, c_im, tile):
    lr = lam_re.astype(F32)
    li = lam_im.astype(F32)
    dt = jnp.exp(log_dt.astype(F32))[:, None]
    mag = jnp.exp(lr * dt)
    a_re = mag * jnp.cos(li * dt)
    a_im = mag * jnp.sin(li * dt)
    den = lr * lr + li * li
    n_re = a_re - 1.0
    n_im = a_im
    k_re = (n_re * lr + n_im * li) / den
    k_im = (n_im * lr - n_re * li) / den
    br = b_re.astype(F32)
    bi = b_im.astype(F32)
    bb_re = k_re[..., None] * br - k_im[..., None] * bi
    bb_im = k_re[..., None] * bi + k_im[..., None] * br
    eye = jnp.eye(SSM_GROUPS, dtype=F32)
    bm_re = jnp.einsum('gph,gk->ghkp', bb_re, eye).reshape(GROUP_WIDTH, SSM_LANES)
    bm_im = jnp.einsum('gph,gk->ghkp', bb_im, eye).reshape(GROUP_WIDTH, SSM_LANES)
    bmat = jnp.concatenate([bm_re, bm_im], axis=1).astype(BF16)
    cm_re = jnp.einsum('ghp,gk->gpkh', c_re.astype(F32), eye).reshape(SSM_LANES, GROUP_WIDTH)
    cm_im = jnp.einsum('ghp,gk->gpkh', c_im.astype(F32), eye).reshape(SSM_LANES, GROUP_WIDTH)
    cmat = jnp.concatenate([cm_re, -cm_im], axis=0).astype(BF16)

    ar = a_re.reshape(1, SSM_LANES)
    ai = a_im.reshape(1, SSM_LANES)

    def cmul(x, y):
        return x[0] * y[0] - x[1] * y[1], x[0] * y[1] + x[1] * y[0]

    groups = tile // SUBLANES
    pows = [(ar, ai)]
    for _ in range(groups - 1):
        pows.append(cmul(pows[-1], (ar, ai)))
    pw = jnp.concatenate([jnp.repeat(jnp.concatenate([p[0] for p in pows], axis=0), SUBLANES, axis=0),
                          jnp.repeat(jnp.concatenate([p[1] for p in pows], axis=0), SUBLANES, axis=0)],
                         axis=1)
    chunk = pows[groups - 1]
    cpows = [chunk]
    for _ in range(SUBLANES - 1):
        cpows.append(cmul(cpows[-1], chunk))
    rows = jnp.arange(SUBLANES, dtype=I32)[:, None]
    ones = jnp.ones((SUBLANES, 1), F32)
    planes = [ones * ar, ones * ai]
    for k in (1, 2, 4):
        m = (rows >= k).astype(F32)
        planes += [m * cpows[k - 1][0], m * cpows[k - 1][1]]
    planes += [jnp.concatenate([p[0] for p in cpows], axis=0),
               jnp.concatenate([p[1] for p in cpows], axis=0)]
    coef = jnp.stack(planes, axis=0)
    return bmat, coef, pw, cmat


def _pool_kernel(v_ref, w_ref, g_ref, o_ref, ext_ref):
    j = pl.program_id(1)
    t = v_ref.shape[0]

    @pl.when(j == 0)
    def _():
        ext_ref[0:POOL_HALO, :] = jnp.zeros((POOL_HALO, GROUP_WIDTH), F32)

    v = v_ref[...]
    ext_ref[POOL_HALO:POOL_HALO + t, :] = v
    cur = ext_ref[...]
    width = 1
    wins = {}
    while width < POOL_WINDOWS[-1]:
        cur = cur + pltpu.roll(cur, width, axis=0)
        width *= 2
        wins[width] = cur[POOL_HALO:, :]
    lane = _lane_iota(v.shape)
    pos = (j * t + lax.broadcasted_iota(I32, v.shape, 0) + 1).astype(F32)
    mean = None
    for gi, w in enumerate(POOL_WINDOWS):
        m = wins[w] / jnp.minimum(pos, float(w))
        mean = m if mean is None else jnp.where(lane >= gi * HEAD_DIM, m, mean)
    mixed = jnp.dot((mean - v).astype(BF16), w_ref[...], preferred_element_type=F32)
    _group_norm_pairs([mixed[:, :LANES], mixed[:, LANES:]], g_ref, o_ref)
    ext_ref[0:POOL_HALO, :] = v[t - POOL_HALO:, :]


def _pool(ub, bsz, seq, w_blk, g):
    n = ub.shape[0]
    t = min(T_POOL, seq)
    nt = seq // t
    row = lambda b, j: (b * nt + j, 0)
    c2 = lambda b, j: (0, 0)
    return pl.pallas_call(
        _pool_kernel,
        grid=(bsz, nt),
        in_specs=[pl.BlockSpec((t, GROUP_WIDTH), row),
                  pl.BlockSpec((GROUP_WIDTH, GROUP_WIDTH), c2),
                  pl.BlockSpec((1, GROUP_WIDTH), c2)],
        out_specs=pl.BlockSpec((t, GROUP_WIDTH), row),
        out_shape=jax.ShapeDtypeStruct((n, GROUP_WIDTH), BF16),
        scratch_shapes=[pltpu.VMEM((t + POOL_HALO, GROUP_WIDTH), F32)],
        compiler_params=_cparams(("arbitrary", "arbitrary")),
        name="pool",
    )(ub, w_blk, g)


def _fox_kernel(qt_ref, kt_ref, qa_ref, ka_ref, vat_ref, g_ref, o_ref, m_ref, acc_ref):
    p_id = pl.program_id(1)
    qi = qt_ref[p_id]
    ki = kt_ref[p_id]
    tq = qa_ref.shape[1]
    tk = ka_ref.shape[1]

    @pl.when(ki == 0)
    def _():
        m_ref[...] = jnp.full_like(m_ref, NEG_INF)
        acc_ref[...] = jnp.zeros_like(acc_ref)

    def step(on_diagonal):
        if on_diagonal:
            causal = (lax.broadcasted_iota(I32, (tk, tq), 0) + (ki * tk - qi * tq)
                      <= lax.broadcasted_iota(I32, (tk, tq), 1))

        def scores(hd):
            return lax.dot_general(ka_ref[hd], qa_ref[hd], (((1,), (1,)), ((), ())),
                                   preferred_element_type=F32)

        s_next = scores(0)
        for hd in range(N_HEADS):
            s_t = s_next
            if hd + 1 < N_HEADS:
                s_next = scores(hd + 1)
            if on_diagonal:
                s_t = jnp.where(causal, s_t, NEG_INF)
            m_prev = m_ref[hd]
            m_new = jnp.maximum(m_prev, jnp.max(s_t, axis=0, keepdims=True))
            alpha = jnp.exp2(m_prev - m_new)
            p_t = jnp.exp2(s_t - m_new)
            acc_ref[hd] = alpha * acc_ref[hd] + jnp.dot(vat_ref[hd], p_t.astype(BF16),
                                                        preferred_element_type=F32)
            m_ref[hd] = m_new

    ratio = tq // tk
    first_masked = qi * ratio

    @pl.when(ki < first_masked)
    def _():
        step(False)

    @pl.when(ki >= first_masked)
    def _():
        step(True)

    @pl.when(ki == first_masked + ratio - 1)
    def _():
        heads = []
        for hd in range(N_HEADS):
            acc_t = acc_ref[hd]
            heads.append(acc_t[:HEAD_DIM] * (1.0 / acc_t[HEAD_DIM:HEAD_DIM + 1]))
        o_t = jnp.concatenate(heads, axis=0)
        ss = jnp.sum(o_t * o_t, axis=0, keepdims=True)
        o_t = o_t * lax.rsqrt(ss * (1.0 / GROUP_WIDTH) + EPS)
        o_ref[...] = (jnp.transpose(o_t) * g_ref[...]).astype(o_ref.dtype)


def _fox(qa, ka, vat, bsz, seq, g):
    n = qa.shape[1]
    tk = min(T_ATT, seq)
    tq = min(T_ATT_Q, seq)
    nq, nk = seq // tq, seq // tk
    ratio = tq // tk
    pairs = [(q, k) for q in range(nq) for k in range(ratio * (q + 1))]
    qt = jnp.asarray([p[0] for p in pairs], I32)
    kt = jnp.asarray([p[1] for p in pairs], I32)
    qmap = lambda b, p, qt, kt: (0, b * nq + qt[p], 0)
    kmap = lambda b, p, qt, kt: (0, b * nk + kt[p], 0)
    grid_spec = pltpu.PrefetchScalarGridSpec(
        num_scalar_prefetch=2,
        grid=(bsz, len(pairs)),
        in_specs=[pl.BlockSpec((N_HEADS, tq, LANES), qmap),
                  pl.BlockSpec((N_HEADS, tk, LANES), kmap),
                  pl.BlockSpec((N_HEADS, LANES, tk), lambda b, p, qt, kt: (0, 0, b * nk + kt[p])),
                  pl.BlockSpec((1, GROUP_WIDTH), lambda b, p, qt, kt: (0, 0))],
        out_specs=pl.BlockSpec((tq, GROUP_WIDTH), lambda b, p, qt, kt: (b * nq + qt[p], 0)),
        scratch_shapes=[pltpu.VMEM((N_HEADS, 1, tq), F32),
                        pltpu.VMEM((N_HEADS, LANES, tq), F32)],
    )
    return pl.pallas_call(
        _fox_kernel,
        grid_spec=grid_spec,
        out_shape=jax.ShapeDtypeStruct((n, GROUP_WIDTH), BF16),
        compiler_params=_cparams(("arbitrary", "arbitrary")),
        name="fox",
    )(qt, kt, qa, ka, vat, g)


def _memkv_kernel(mem_ref, g_ref, w_ref, kg_ref, mk_ref, mvt_ref):
    x = mem_ref[0]
    h = x * lax.rsqrt(jnp.mean(x * x, axis=-1, keepdims=True) + EPS) * g_ref[...]
    kv = jnp.dot(h.astype(BF16), w_ref[...], preferred_element_type=F32)
    m = x.shape[0]
    v_t = jnp.transpose(kv[:, GROUP_WIDTH:])
    tail = jnp.where(lax.broadcasted_iota(I32, (HEAD_DIM, m), 0) == 0, 1.0, 0.0)
    for hd in range(N_HEADS):
        pair, odd = hd // 2, hd % 2
        ks = _head_slab(kv[:, pair * LANES:(pair + 1) * LANES], odd)
        mk_ref[0, hd] = _head_norm(ks, kg_ref[...]).astype(BF16)
        mvt_ref[0, hd] = jnp.concatenate([v_t[hd * HEAD_DIM:(hd + 1) * HEAD_DIM], tail],
                                         axis=0).astype(BF16)


def _memkv(mem, g, w_kv, kg):
    bsz, m, d = mem.shape
    c2 = lambda b: (0, 0)
    out = jax.ShapeDtypeStruct((bsz, N_HEADS, m, LANES), BF16)
    out_t = jax.ShapeDtypeStruct((bsz, N_HEADS, LANES, m), BF16)
    return pl.pallas_call(
        _memkv_kernel,
        grid=(bsz,),
        in_specs=[pl.BlockSpec((1, m, d), lambda b: (b, 0, 0)),
                  pl.BlockSpec((1, d), c2),
                  pl.BlockSpec((d, 2 * GROUP_WIDTH), c2),
                  pl.BlockSpec((1, LANES), c2)],
        out_specs=(pl.BlockSpec((1, N_HEADS, m, LANES), lambda b: (b, 0, 0, 0)),
                   pl.BlockSpec((1, N_HEADS, LANES, m), lambda b: (b, 0, 0, 0))),
        out_shape=(out, out_t),
        compiler_params=_cparams(("arbitrary",)),
        name="memkv",
    )(mem, g, w_kv, kg)


def _memattn_kernel(qm_ref, mk_ref, mvt_ref, g_ref, o_ref):
    heads = []
    for hd in range(N_HEADS):
        s_t = lax.dot_general(mk_ref[0, hd], qm_ref[hd], (((1,), (1,)), ((), ())),
                              preferred_element_type=F32)
        p_t = jnp.exp2(s_t - jnp.max(s_t, axis=0, keepdims=True))
        acc_t = jnp.dot(mvt_ref[0, hd], p_t.astype(BF16), preferred_element_type=F32)
        heads.append(acc_t[:HEAD_DIM] * (1.0 / acc_t[HEAD_DIM:HEAD_DIM + 1]))
    o_t = jnp.concatenate(heads, axis=0)
    ss = jnp.sum(o_t * o_t, axis=0, keepdims=True)
    o_t = o_t * lax.rsqrt(ss * (1.0 / GROUP_WIDTH) + EPS)
    o_ref[...] = (jnp.transpose(o_t) * g_ref[...]).astype(o_ref.dtype)


def _memattn(qm, mk, mv, bsz, seq, g):
    n = qm.shape[1]
    m = mk.shape[2]
    t = min(T_ATT, seq)
    nt = seq // t
    return pl.pallas_call(
        _memattn_kernel,
        grid=(bsz, nt),
        in_specs=[pl.BlockSpec((N_HEADS, t, LANES), lambda b, j: (0, b * nt + j, 0)),
                  pl.BlockSpec((1, N_HEADS, m, LANES), lambda b, j: (b, 0, 0, 0)),
                  pl.BlockSpec((1, N_HEADS, LANES, m), lambda b, j: (b, 0, 0, 0)),
                  pl.BlockSpec((1, GROUP_WIDTH), lambda b, j: (0, 0))],
        out_specs=pl.BlockSpec((t, GROUP_WIDTH), lambda b, j: (b * nt + j, 0)),
        out_shape=jax.ShapeDtypeStruct((n, GROUP_WIDTH), BF16),
        compiler_params=_cparams(("arbitrary", "arbitrary")),
        name="memattn",
    )(qm, mk, mv, g)


def _outproj_kernel(x_ref, ya_ref, yb_ref, yc_ref, ym_ref, w_ref, fg_ref, rw_ref, rb_ref, us_ref,
                    x1_ref, hn_ref, idx_ref, gate_ref, rank_ref, cnt_ref, toff_ref, carry_ref):
    @pl.when(pl.program_id(0) == 0)
    def _():
        carry_ref[...] = jnp.zeros_like(carry_ref)

    merged = jnp.concatenate([ya_ref[0].astype(BF16), ya_ref[1].astype(BF16), yb_ref[...],
                              yc_ref[...], ym_ref[...]], axis=1)
    acc = x_ref[...] + jnp.dot(merged, w_ref[...], preferred_element_type=F32)
    x1_ref[...] = acc
    hn = acc * lax.rsqrt(jnp.mean(acc * acc, axis=-1, keepdims=True) + EPS) * fg_ref[...]
    _store_token_tiles(hn_ref, hn)
    n_exp = rb_ref.shape[0]
    tm = hn.shape[0]
    hn_hi = hn.astype(BF16)
    hn_lo = (hn - hn_hi.astype(F32)).astype(BF16)
    parts = jnp.dot(jnp.concatenate([hn_hi, hn_lo], axis=0), rw_ref[...],
                    preferred_element_type=F32)
    top = jnp.transpose(parts[:tm])
    bot = jnp.transpose(parts[tm:])
    logits = (top[:n_exp] + top[n_exp:2 * n_exp] + bot[:n_exp] + bot[n_exp:2 * n_exp]
              + rb_ref[...])
    e_iota = lax.broadcasted_iota(I32, (n_exp, tm), 0).astype(F32)
    work = logits
    vals, onehots = [], []
    for k in range(TOP_K):
        m = jnp.max(work, axis=0, keepdims=True)
        sel = jnp.min(jnp.where(work == m, e_iota, float(n_exp)), axis=0, keepdims=True)
        hot = e_iota == sel
        idx_ref[k:k + 1, :] = sel.astype(I32)
        vals.append(m)
        onehots.append(hot.astype(F32))
        work = jnp.where(hot, NEG_INF, work)
    exps = [jnp.exp(v - vals[0]) for v in vals]
    denom = exps[0] + exps[1] + exps[2] + exps[3]
    for k in range(TOP_K):
        gate_ref[k:k + 1, :] = exps[k] / denom
    stacked = jnp.concatenate(onehots, axis=0).astype(BF16)
    prefix = jnp.dot(stacked, us_ref[...], preferred_element_type=F32)
    base = carry_ref[...]
    base_sq = jnp.concatenate([base, jnp.zeros((LANES - n_exp, LANES), F32)], axis=0)
    toff_ref[...] = jnp.transpose(base_sq)[:SUBLANES, :]
    for k in range(TOP_K):
        hot = onehots[k]
        pk = prefix[k * n_exp:(k + 1) * n_exp, :]
        rank = jnp.sum(hot * (pk + base[:, 0:1]), axis=0, keepdims=True)
        rank_ref[k:k + 1, :] = rank.astype(I32)
        base = base + jnp.sum(hot, axis=1, keepdims=True)
    carry_ref[...] = base
    cnt_ref[...] = base


def _outproj(x2, ya, yb, yc, ym, w_out, fg, rw, rb, ustrict):
    n, d = x2.shape
    n_exp = rb.shape[0]
    tm = min(TM_OUT, n)
    row = lambda i: (i, 0)
    col = lambda i: (0, i)
    const = lambda i: (0, 0)
    return pl.pallas_call(
        _outproj_kernel,
        grid=(n // tm,),
        in_specs=[pl.BlockSpec((tm, d), row)]
        + [pl.BlockSpec((2, tm, LANES), lambda i: (0, i, 0))]
        + [pl.BlockSpec((tm, GROUP_WIDTH), row)] * 3
        + [pl.BlockSpec((d, d), const),
           pl.BlockSpec((1, d), const),
           pl.BlockSpec((d, LANES), const),
           pl.BlockSpec((n_exp, 1), const),
           pl.BlockSpec((tm, tm), const)],
        out_specs=(pl.BlockSpec((tm, d), row),
                   pl.BlockSpec((tm * ROW_CHUNKS, LANES), row),
                   pl.BlockSpec((TOP_K, tm), col),
                   pl.BlockSpec((TOP_K, tm), col),
                   pl.BlockSpec((TOP_K, tm), col),
                   pl.BlockSpec((n_exp, LANES), const),
                   pl.BlockSpec((SUBLANES, LANES), row)),
        out_shape=(jax.ShapeDtypeStruct((n, d), F32),
                   jax.ShapeDtypeStruct((n * ROW_CHUNKS, LANES), F32),
                   jax.ShapeDtypeStruct((TOP_K, n), I32),
                   jax.ShapeDtypeStruct((TOP_K, n), F32),
                   jax.ShapeDtypeStruct((TOP_K, n), I32),
                   jax.ShapeDtypeStruct((n_exp, LANES), F32),
                   jax.ShapeDtypeStruct((n // tm * SUBLANES, LANES), F32)),
        scratch_shapes=[pltpu.VMEM((n_exp, LANES), F32)],
        compiler_params=_cparams(("arbitrary",)),
        name="outproj",
    )(x2, ya, yb, yc, ym, w_out, fg, rw, rb, ustrict)


def _plan_kernel(cnt_ref, idx_ref, rank_ref, toff_ref, dest_ref, meta_ref, start_ref, win_ref,
                 *, n_exp, block_rows):
    def body(e, off):
        start_ref[e] = off
        return off + (cnt_ref[e] + block_rows - 1) // block_rows * block_rows

    total = lax.fori_loop(0, n_exp, body, jnp.int32(0))
    idx = idx_ref[...]
    dest = rank_ref[...]
    blk_start = lax.broadcasted_iota(I32, meta_ref.shape, 1) * block_rows
    blk_e = jnp.zeros(meta_ref.shape, I32)
    lane = _lane_iota((SUBLANES, LANES))
    start_l = jnp.zeros((SUBLANES, LANES), I32)
    cnt_l = jnp.zeros((SUBLANES, LANES), I32)
    for e in range(n_exp):
        dest = dest + jnp.where(idx == e, start_ref[e], 0)
        start_l = jnp.where(lane == e, start_ref[e], start_l)
        cnt_l = jnp.where(lane == e, cnt_ref[e], cnt_l)
        if e > 0:
            blk_e = blk_e + (blk_start >= start_ref[e]).astype(I32)
    dest_ref[...] = dest
    sub = lax.broadcasted_iota(I32, meta_ref.shape, 0)
    meta_ref[...] = jnp.where(sub == 0, blk_e, total // block_rows)

    toff = toff_ref[...].astype(I32)
    n_rows = toff.shape[0]
    nxt = jnp.concatenate([toff[SUBLANES:], cnt_l], axis=0) if n_rows > SUBLANES else cnt_l
    first = jnp.concatenate([start_l] * (n_rows // SUBLANES), axis=0) + toff
    aligned = jnp.bitwise_and(first, -BF16_ROWS)
    lanes = _lane_iota(toff.shape)
    span = jnp.where(lanes < n_exp, first - aligned + (nxt - toff), 0)
    n_win = jnp.right_shift(span + (WINDOW_ROWS - 1), WINDOW_SHIFT)
    win_ref[...] = jnp.where(lanes < n_exp, aligned, pltpu.roll(n_win, n_exp, axis=1))


def _plan(cnt, idx, rank, toff, n_blocks):
    n_exp = cnt.shape[0]
    nb_pad = (n_blocks + LANES - 1) // LANES * LANES
    return pl.pallas_call(
        functools.partial(_plan_kernel, n_exp=n_exp, block_rows=TM_EXPERT),
        in_specs=[pl.BlockSpec(memory_space=pltpu.SMEM),
                  pl.BlockSpec(memory_space=pltpu.VMEM),
                  pl.BlockSpec(memory_space=pltpu.VMEM),
                  pl.BlockSpec(memory_space=pltpu.VMEM)],
        out_specs=(pl.BlockSpec(memory_space=pltpu.VMEM),
                   pl.BlockSpec(memory_space=pltpu.VMEM),
                   pl.BlockSpec(memory_space=pltpu.SMEM),
                   pl.BlockSpec(memory_space=pltpu.VMEM)),
        out_shape=(jax.ShapeDtypeStruct(idx.shape, I32),
                   jax.ShapeDtypeStruct((SUBLANES, nb_pad), I32),
                   jax.ShapeDtypeStruct((n_exp,), I32),
                   jax.ShapeDtypeStruct(toff.shape, I32)),
        compiler_params=pltpu.CompilerParams(vmem_limit_bytes=VMEM_LIMIT),
        name="plan",
    )(cnt, idx, rank, toff)


def _tile_rows(row):
    return pl.ds(pl.multiple_of(row * ROW_CHUNKS, ROW_CHUNKS), ROW_CHUNKS)


def _dispatch_kernel(dest_ref, cnt_ref, start_ref, hn_ref, xs_ref, zero_ref, sem, zsem,
                     *, n_exp, block_rows):
    tm = dest_ref.shape[1]

    def pad_copy(e, r):
        return pltpu.make_async_copy(zero_ref.at[pl.ds(0, ROW_CHUNKS)],
                                     xs_ref.at[_tile_rows(start_ref[e] + r)], zsem)

    def pad_chunk_copy(e, c):
        rows = pl.ds(pl.multiple_of((start_ref[e] + c * PAD_CHUNK) * ROW_CHUNKS, PAD_CHUNK * ROW_CHUNKS),
                     PAD_CHUNK * ROW_CHUNKS)
        return pltpu.make_async_copy(zero_ref.at[pl.ds(0, PAD_CHUNK * ROW_CHUNKS)], xs_ref.at[rows],
                                     zsem)

    def pad_bounds(e):
        cnt = cnt_ref[e]
        hi = (cnt + block_rows - 1) // block_rows * block_rows
        mid = jnp.minimum((cnt + PAD_CHUNK - 1) // PAD_CHUNK * PAD_CHUNK, hi)
        return cnt, mid, hi

    @pl.when(pl.program_id(0) == 0)
    def _():
        zero_ref[...] = jnp.zeros_like(zero_ref)

        def pad_start(e, c):
            lo, mid, hi = pad_bounds(e)
            lax.fori_loop(lo, mid, lambda r, c: (pad_copy(e, r).start(), c)[1], c)
            return lax.fori_loop(mid // PAD_CHUNK, hi // PAD_CHUNK,
                                 lambda j, c: (pad_chunk_copy(e, j).start(), c)[1], c)

        def pad_wait(e, c):
            lo, mid, hi = pad_bounds(e)
            lax.fori_loop(lo, mid, lambda r, c: (pad_copy(e, r).wait(), c)[1], c)
            return lax.fori_loop(mid // PAD_CHUNK, hi // PAD_CHUNK,
                                 lambda j, c: (pad_chunk_copy(e, j).wait(), c)[1], c)

        lax.fori_loop(0, n_exp, pad_start, 0)
        lax.fori_loop(0, n_exp, pad_wait, 0)

        last = n_exp - 1
        used = start_ref[last] + pad_bounds(last)[2]
        n_blocks = xs_ref.shape[0] // (ROW_CHUNKS * block_rows)

        def tail_copy(b):
            rows = pl.ds(pl.multiple_of(b * (block_rows * ROW_CHUNKS), block_rows * ROW_CHUNKS),
                         block_rows * ROW_CHUNKS)
            return pltpu.make_async_copy(zero_ref, xs_ref.at[rows], zsem)

        lax.fori_loop(used // block_rows, n_blocks, lambda b, c: (tail_copy(b).start(), c)[1], 0)
        lax.fori_loop(used // block_rows, n_blocks, lambda b, c: (tail_copy(b).wait(), c)[1], 0)

    def copy(k, t):
        return pltpu.make_async_copy(hn_ref.at[_tile_rows(t)],
                                     xs_ref.at[_tile_rows(dest_ref[k, t])], sem)

    def start(t, c):
        for k in range(TOP_K):
            copy(k, t).start(priority=k % 2)
        return c

    def wait(t, c):
        for k in range(TOP_K):
            copy(k, t).wait()
        return c

    lax.fori_loop(0, tm, start, 0, unroll=4)
    lax.fori_loop(0, tm, wait, 0, unroll=8)


def _dispatch(dest, cnt, starts, hn, cap):
    n = hn.shape[0] // ROW_CHUNKS
    tm = min(T_MOVE, n)
    return pl.pallas_call(
        functools.partial(_dispatch_kernel, n_exp=cnt.shape[0], block_rows=TM_EXPERT),
        grid=(n // tm,),
        in_specs=[pl.BlockSpec((TOP_K, tm), lambda i: (0, i), memory_space=pltpu.SMEM),
                  pl.BlockSpec(memory_space=pltpu.SMEM),
                  pl.BlockSpec(memory_space=pltpu.SMEM),
                  pl.BlockSpec((tm * ROW_CHUNKS, LANES), lambda i: (i, 0))],
        out_specs=pl.BlockSpec(memory_space=pl.ANY),
        out_shape=jax.ShapeDtypeStruct((cap * ROW_CHUNKS, LANES), hn.dtype),
        scratch_shapes=[pltpu.VMEM((TM_EXPERT * ROW_CHUNKS, LANES), hn.dtype),
                        pltpu.SemaphoreType.DMA, pltpu.SemaphoreType.DMA],
        compiler_params=_cparams(("arbitrary",)),
        name="dispatch",
    )(dest, cnt, starts, hn)


def _experts_kernel(meta_ref, xs_ref, wg_ref, bg_ref, wu_ref, bu_ref, wd_ref, bd_ref, ys_ref,
                    wg_s, wu_s, wd_s):
    i = pl.program_id(0)
    tm = xs_ref.shape[0] // ROW_CHUNKS
    in_use = i < meta_ref[1, 0]
    new_expert = jnp.logical_or(i == 0, meta_ref[0, i] != meta_ref[0, jnp.maximum(i - 1, 0)])

    @pl.when(jnp.logical_not(in_use))
    def _():
        ys_ref[...] = jnp.zeros_like(ys_ref)

    @pl.when(jnp.logical_and(in_use, new_expert))
    def _():
        wg_s[...] = wg_ref[0, 0].astype(BF16)
        wu_s[...] = wu_ref[0, 0].astype(BF16)
        wd_s[...] = wd_ref[0, 0].astype(BF16)

    @pl.when(in_use)
    def _():
        x = _load_token_tiles(xs_ref, tm).astype(BF16)
        gl = jnp.dot(x, wg_s[...], preferred_element_type=F32) + bg_ref[0, 0]
        up = jnp.dot(x, wu_s[...], preferred_element_type=F32) + bu_ref[0, 0]
        gl = jnp.minimum(gl, SWIGLU_LIMIT)
        up = jnp.clip(up, -SWIGLU_LIMIT, SWIGLU_LIMIT)
        act = gl * (1.0 / (1.0 + jnp.exp(-SWIGLU_ALPHA * gl)))
        hmid = ((up + 1.0) * act).astype(BF16)
        y = jnp.dot(hmid, wd_s[...], preferred_element_type=F32) + bd_ref[0, 0]
        ys_ref[...] = y.astype(ys_ref.dtype)


def _experts(meta, xs, layer, wg, bg, wu, bu, wd, bd):
    cap = xs.shape[0] // ROW_CHUNKS
    _, _, d, de = wg.shape
    tm = TM_EXPERT
    n_blocks = cap // tm

    def blk(i, meta):
        return jnp.minimum(i, meta[1, 0] - 1)

    rows = lambda i, meta: (blk(i, meta), 0)
    wmap = lambda i, meta: (layer, meta[0, blk(i, meta)], 0, 0)
    grid_spec = pltpu.PrefetchScalarGridSpec(
        num_scalar_prefetch=1,
        grid=(n_blocks,),
        in_specs=[pl.BlockSpec((tm * ROW_CHUNKS, LANES), rows),
                  pl.BlockSpec((1, 1, d, de), wmap),
                  pl.BlockSpec((1, 1, 1, de), wmap),
                  pl.BlockSpec((1, 1, d, de), wmap),
                  pl.BlockSpec((1, 1, 1, de), wmap),
                  pl.BlockSpec((1, 1, de, d), wmap),
                  pl.BlockSpec((1, 1, 1, d), wmap)],
        out_specs=pl.BlockSpec((tm, d), lambda i, meta: (i, 0)),
        scratch_shapes=[pltpu.VMEM((d, de), BF16), pltpu.VMEM((d, de), BF16),
                        pltpu.VMEM((de, d), BF16)],
    )
    return pl.pallas_call(
        _experts_kernel,
        grid_spec=grid_spec,
        out_shape=jax.ShapeDtypeStruct((cap, d), BF16),
        compiler_params=_cparams(("arbitrary",)),
        name="experts",
    )(meta, xs, wg, bg, wu, bu, wd, bd)


def _combine_kernel(win_ref, dest_ref, idx_ref, gate_ref, x1_ref, ys_ref, o_ref, buf_ref, one_ref,
                    sem, one_sem, *, n_exp):
    i = pl.program_id(0)
    n_tiles = pl.num_programs(0)
    tm = x1_ref.shape[0]
    last_window = ys_ref.shape[0] - WINDOW_ROWS

    def window_start(tile, e, rnd):
        return jnp.minimum(win_ref[tile, e] + rnd * WINDOW_ROWS, last_window)

    def ys_window(tile, e, rnd):
        return ys_ref.at[pl.ds(pl.multiple_of(window_start(tile, e, rnd), BF16_ROWS), WINDOW_ROWS)]

    def buf_window(slot, w):
        return buf_ref.at[slot, pl.ds(pl.multiple_of(w * WINDOW_ROWS, WINDOW_ROWS), WINDOW_ROWS)]

    def for_spills(tile, fn):
        def body(e, j):
            hit = jnp.logical_and(win_ref[tile, n_exp + e] > 1, j < SPILL_WINDOWS)

            @pl.when(hit)
            def _():
                fn(e, j)

            return j + hit.astype(I32)

        lax.fori_loop(0, n_exp, body, jnp.int32(0))

    def fetch(tile, slot):
        for e in range(n_exp):
            pltpu.make_async_copy(ys_window(tile, e, 0), buf_window(slot, e),
                                  sem.at[slot]).start(priority=e % 2)
        for_spills(tile, lambda e, j: pltpu.make_async_copy(
            ys_window(tile, e, 1), buf_window(slot, n_exp + j), sem.at[slot]).start())

    def drain(tile, slot):
        for e in range(n_exp):
            pltpu.make_async_copy(ys_window(tile, e, 0), buf_window(slot, e), sem.at[slot]).wait()
        for_spills(tile, lambda e, j: pltpu.make_async_copy(
            ys_window(tile, e, 1), buf_window(slot, n_exp + j), sem.at[slot]).wait())

    @pl.when(i == 0)
    def _():
        fetch(0, 0)

    @pl.when(i + 1 < n_tiles)
    def _():
        fetch(i + 1, (i + 1) % 2)

    idx = idx_ref[...]
    dest = dest_ref[...]
    gates = gate_ref[...]
    chunk = SEL_CHUNK_WINDOWS * WINDOW_ROWS
    lane = _lane_iota((tm, chunk)).astype(F32)

    first_tok = jnp.zeros(idx.shape, I32)
    clamped_tok = jnp.zeros(idx.shape, I32)
    for e in range(n_exp):
        first_tok = jnp.where(idx == e, win_ref[i, e], first_tok)
        clamped_tok = jnp.where(idx == e, window_start(i, e, 0), clamped_tok)
    local = dest - clamped_tok
    col = jnp.where(dest - first_tok < WINDOW_ROWS, idx * WINDOW_ROWS + local, -1)
    key = idx * KEY_STRIDE + (dest - first_tok)
    packed = jnp.concatenate([col.astype(F32), gates, key.astype(F32),
                              jnp.zeros((LANES - 3 * TOP_K, tm), F32)], axis=0)
    by_tok = jnp.transpose(packed)

    def spread(j, width):
        one = jnp.broadcast_to(by_tok[:, j:j + 1], (tm, LANES))
        return jnp.concatenate([one] * (width // LANES), axis=1) if width > LANES else one

    slot = i % 2
    drain(i, slot)
    cols = [spread(k, chunk) for k in range(TOP_K)]
    gts = [spread(TOP_K + k, chunk) for k in range(TOP_K)]
    total = x1_ref[...]
    for c in range(n_exp // SEL_CHUNK_WINDOWS):
        sel = jnp.zeros((tm, chunk), F32)
        for k in range(TOP_K):
            sel = jnp.where(lane == cols[k] - float(c * chunk), gts[k], sel)
        total = total + jnp.dot(sel.astype(BF16), buf_ref[slot, c * chunk:(c + 1) * chunk, :],
                                preferred_element_type=F32)
    o_ref[...] = total

    lane1 = _lane_iota((tm, LANES))
    keys1 = [spread(2 * TOP_K + k, LANES) for k in range(TOP_K)]
    gts1 = [spread(TOP_K + k, LANES) for k in range(TOP_K)]

    def add_window(e, rnd, rows):
        shift = win_ref[i, e] + rnd * WINDOW_ROWS - window_start(i, e, rnd)
        want = jnp.where(lane1 >= shift, lane1 + (e * KEY_STRIDE + rnd * WINDOW_ROWS - shift),
                         -1).astype(F32)
        sel = jnp.zeros((tm, LANES), F32)
        for k in range(TOP_K):
            sel = jnp.where(want == keys1[k], gts1[k], sel)
        o_ref[...] += jnp.dot(sel.astype(BF16), rows, preferred_element_type=F32)

    for_spills(i, lambda e, j: add_window(e, 1, buf_window(slot, n_exp + j)[...]))

    def on_demand(e, j):
        n_win = win_ref[i, n_exp + e]
        hit = n_win > 1
        ahead = jnp.logical_and(hit, j < SPILL_WINDOWS)

        def one(rnd, c):
            cp = pltpu.make_async_copy(ys_window(i, e, rnd), one_ref, one_sem)
            cp.start()
            cp.wait()
            add_window(e, rnd, one_ref[...])
            return c

        lax.fori_loop(jnp.where(ahead, 2, 1), n_win, one, 0)
        return j + hit.astype(I32)

    lax.fori_loop(0, n_exp, on_demand, jnp.int32(0))


def _combine(win, dest, idx, gates, x1, ys, n_exp):
    n, d = x1.shape
    tm = min(TM_OUT, n)
    vec = lambda i, win: (0, i)
    grid_spec = pltpu.PrefetchScalarGridSpec(
        num_scalar_prefetch=1,
        grid=(n // tm,),
        in_specs=[pl.BlockSpec((TOP_K, tm), vec),
                  pl.BlockSpec((TOP_K, tm), vec),
                  pl.BlockSpec((TOP_K, tm), vec),
                  pl.BlockSpec((tm, d), lambda i, win: (i, 0)),
                  pl.BlockSpec(memory_space=pl.ANY)],
        out_specs=pl.BlockSpec((tm, d), lambda i, win: (i, 0)),
        scratch_shapes=[pltpu.VMEM((2, (n_exp + SPILL_WINDOWS) * WINDOW_ROWS, d), ys.dtype),
                        pltpu.VMEM((WINDOW_ROWS, d), ys.dtype),
                        pltpu.SemaphoreType.DMA((2,)),
                        pltpu.SemaphoreType.DMA],
    )
    return pl.pallas_call(
        functools.partial(_combine_kernel, n_exp=n_exp),
        grid_spec=grid_spec,
        out_shape=jax.ShapeDtypeStruct((n, d), F32),
        compiler_params=_cparams(("arbitrary",)),
        name="combine",
    )(win, dest, idx, gates, x1, ys)


def _pad_lanes(v, width=LANES):
    v = v.astype(F32).reshape(1, -1)
    return jnp.pad(v, ((0, 0), (0, width - v.shape[1])))


def _layer(x2, mem, bsz, seq, p, layer, experts):
    n, d = x2.shape
    f32 = F32
    w_in = p['w_in']
    off_q = 2 * GROUP_WIDTH
    off_k, off_v, off_f = off_q + GROUP_WIDTH, off_q + 2 * GROUP_WIDTH, off_q + 3 * GROUP_WIDTH
    off_qm = off_f + N_HEADS
    w_f = w_in[:, off_f:off_qm]
    w_f_pad = jnp.pad(jnp.concatenate([w_f, w_f, w_f], axis=1), ((0, 0), (0, LANES - 3 * N_HEADS)))
    w_all = jnp.concatenate([w_in[:, :off_f], w_in[:, off_qm:], w_f_pad], axis=1).astype(BF16)
    fb = p['fox_forget_b'].astype(f32)
    fb_pad = _pad_lanes(jnp.concatenate([fb, fb, fb]))
    tm_in = min(TM_INPROJ, seq)
    tri = jnp.tril(jnp.ones((tm_in, tm_in), f32)).astype(BF16)

    sel = jnp.zeros((3, LANES, N_HEADS, LANES), f32)
    for part in range(3):
        for hd in range(N_HEADS):
            sel = sel.at[part, hd, hd, HEAD_DIM + part].set(1.0)
            sel = sel.at[part, hd, hd, HEAD_DIM + 3 + part].set(-1.0)
    sel = sel.reshape(3 * LANES, N_HEADS * LANES).astype(BF16)
    head_of = jnp.arange(GROUP_WIDTH) // HEAD_DIM
    ones_blk = (head_of[:, None] == head_of[None, :]).astype(BF16)
    ones2 = jnp.concatenate([ones_blk, ones_blk], axis=0)

    def head_gain(g):
        return jnp.tile(g.astype(f32).reshape(1, HEAD_DIM), (1, N_HEADS))

    ua, ub, qa, ka, va, qm = _inproj(
        x2, seq, p['mix_norm_g'].reshape(1, d).astype(f32), w_all,
        head_gain(p['fox_q_g']), head_gain(p['fox_k_g']), head_gain(p['mem_q_g']), fb_pad, tri,
        sel, ones2)

    bmat, coef, ssm_pw, cmat = _s5_constants(
        p['ssm_lambda_re'], p['ssm_lambda_im'], p['ssm_log_dt'], p['ssm_b_re'], p['ssm_b_im'],
        p['ssm_c_re'], p['ssm_c_im'], min(T_S5, seq))
    ya = _s5(ua, bsz, seq, bmat, coef, ssm_pw, cmat,
             p['ssm_d'].reshape(1, GROUP_WIDTH).astype(f32), p['ssm_glu_w'].astype(BF16),
             p['ssm_glu_b'].reshape(1, GROUP_WIDTH).astype(f32),
             p['ssm_out_g'].reshape(1, GROUP_WIDTH).astype(f32))

    pw = p['pool_w'].astype(f32)
    w_blk = jnp.zeros((GROUP_WIDTH, GROUP_WIDTH), f32)
    for gi in range(len(POOL_WINDOWS)):
        w_blk = w_blk.at[gi * HEAD_DIM:(gi + 1) * HEAD_DIM, gi * HEAD_DIM:(gi + 1) * HEAD_DIM].set(pw[gi])
    yb = _pool(ub, bsz, seq, w_blk.astype(BF16), p['pool_scale'].reshape(1, GROUP_WIDTH).astype(f32))

    yc = _fox(qa, ka, va, bsz, seq, p['fox_out_g'].reshape(1, GROUP_WIDTH).astype(f32))

    mk, mv = _memkv(mem, p['mem_norm_g'].reshape(1, d).astype(f32), p['mem_w_kv'].astype(BF16),
                    _pad_lanes(p['mem_k_g']))
    ym = _memattn(qm, mk, mv, bsz, seq, p['mem_out_g'].reshape(1, GROUP_WIDTH).astype(f32))

    n_exp = p['router_w'].shape[1]
    tm_out = min(TM_OUT, n)
    ustrict = jnp.triu(jnp.ones((tm_out, tm_out), f32), k=1).astype(BF16)
    rw = p['router_w'].astype(f32)
    rw_hi = rw.astype(BF16)
    rw_lo = (rw - rw_hi.astype(f32)).astype(BF16)
    rw_parts = jnp.pad(jnp.concatenate([rw_hi, rw_lo], axis=1), ((0, 0), (0, LANES - 2 * n_exp)))
    x1, hn, idx, gates, rank, cnt, toff = _outproj(
        x2, ya, yb, yc, ym, p['w_out'].astype(BF16), p['ffn_norm_g'].reshape(1, d).astype(f32),
        rw_parts, p['router_b'].reshape(n_exp, 1).astype(f32), ustrict)

    cap = n * TOP_K + n_exp * TM_EXPERT
    cnt_i = cnt[:, 0].astype(I32)
    dest, meta, starts, win = _plan(cnt_i, idx, rank, toff, cap // TM_EXPERT)
    xs = _dispatch(dest, cnt_i, starts, hn, cap)
    ys = _experts(meta, xs, layer, *experts)
    win_tab = win[::SUBLANES, :2 * n_exp]
    return _combine(win_tab, dest, idx, gates, x1, ys, n_exp)


_PARAM_NAMES = ('mix_norm_g', 'w_in', 'ssm_lambda_re', 'ssm_lambda_im', 'ssm_log_dt',
                'ssm_b_re', 'ssm_b_im', 'ssm_c_re', 'ssm_c_im', 'ssm_d', 'ssm_glu_w', 'ssm_glu_b',
                'ssm_out_g', 'pool_w', 'pool_scale', 'fox_forget_b', 'fox_q_g', 'fox_k_g',
                'fox_out_g', 'mem_norm_g', 'mem_w_kv', 'mem_q_g', 'mem_k_g', 'mem_out_g', 'w_out',
                'ffn_norm_g', 'router_w', 'router_b', 'exp_w_gate', 'exp_b_gate', 'exp_w_up',
                'exp_b_up', 'exp_w_down', 'exp_b_down')


def kernel(x, mem, mix_norm_g, w_in, ssm_lambda_re, ssm_lambda_im, ssm_log_dt, ssm_b_re, ssm_b_im,
           ssm_c_re, ssm_c_im, ssm_d, ssm_glu_w, ssm_glu_b, ssm_out_g, pool_w, pool_scale,
           fox_forget_b, fox_q_g, fox_k_g, fox_out_g, mem_norm_g, mem_w_kv, mem_q_g, mem_k_g,
           mem_out_g, w_out, ffn_norm_g, router_w, router_b, exp_w_gate, exp_b_gate, exp_w_up,
           exp_b_up, exp_w_down, exp_b_down):
    stacked = dict(zip(_PARAM_NAMES, (
        mix_norm_g, w_in, ssm_lambda_re, ssm_lambda_im, ssm_log_dt, ssm_b_re, ssm_b_im, ssm_c_re,
        ssm_c_im, ssm_d, ssm_glu_w, ssm_glu_b, ssm_out_g, pool_w, pool_scale, fox_forget_b,
        fox_q_g, fox_k_g, fox_out_g, mem_norm_g, mem_w_kv, mem_q_g, mem_k_g, mem_out_g, w_out,
        ffn_norm_g, router_w, router_b, exp_w_gate, exp_b_gate, exp_w_up, exp_b_up, exp_w_down,
        exp_b_down)))
    bsz, seq, d = x.shape
    depth = w_in.shape[0]
    x2 = x.reshape(bsz * seq, d).astype(F32)
    mem = mem.astype(F32)
    expert_names = ('exp_w_gate', 'exp_b_gate', 'exp_w_up', 'exp_b_up', 'exp_w_down', 'exp_b_down')
    experts = tuple(stacked[k].astype(F32) if stacked[k].ndim == 4
                    else stacked[k].astype(F32)[:, :, None, :] for k in expert_names)
    for layer in range(depth):
        x2 = _layer(x2, mem, bsz, seq,
                    {k: v[layer] for k, v in stacked.items() if k not in expert_names},
                    layer, experts)
    return x2.reshape(bsz, seq, d).astype(x.dtype)
```
